```python
import math
import jax, jax.numpy as jnp
from jax import lax
import numpy as np

D_MODEL = 1024
BATCH = 8
SEQ = 2048
DEPTH = 1

MIX_WIDTH = D_MODEL
ATTN_WIDTH = MIX_WIDTH // 2
SSM_WIDTH = MIX_WIDTH - ATTN_WIDTH
ATTN_HEADS = 4
ATTN_VDIM = ATTN_WIDTH // ATTN_HEADS
ATTN_QKDIM = ATTN_VDIM // 2
Q_BLOCK = 128
SSM_GROUP = 16
SSM_GROUPS = SSM_WIDTH // SSM_GROUP
SSM_STATE = 64
D_FF = 4 * D_MODEL
EPS = 1e-6
IN_WIDTH = 3 * ATTN_WIDTH + SSM_WIDTH

kernel_name = 'hybrid_diffattn_s5_encoder_block'


def rmsnorm(x, g):
    xf = x.astype(jnp.float32)
    y = xf * lax.rsqrt(jnp.mean(xf * xf, axis=-1, keepdims=True) + EPS)
    return (y * g.astype(jnp.float32)).astype(x.dtype)


def alibi_slopes(n_heads):
    return jnp.asarray([2.0 ** (-8.0 * (h + 1) / n_heads) for h in range(n_heads)], dtype=jnp.float32)


def diff_attention(q, k, v, lam):
    b, s, h, _, d = q.shape
    nblk = s // Q_BLOCK
    scale = d ** -0.5
    slopes = alibi_slopes(h)
    kpos = jnp.arange(s, dtype=jnp.int32)
    qb = q.reshape(b, nblk, Q_BLOCK, h, 2, d).transpose(1, 0, 2, 3, 4, 5)

    def block(args):
        qi, i = args
        qpos = i * Q_BLOCK + jnp.arange(Q_BLOCK, dtype=jnp.int32)
        dist = jnp.abs(qpos[:, None] - kpos[None, :]).astype(jnp.float32)
        bias = -slopes[:, None, None] * dist[None]
        sc = jnp.einsum('bqhcd,bkhcd->bhcqk', qi, k).astype(jnp.float32) * scale
        p = jax.nn.softmax(sc + bias[None, :, None], axis=-1)
        a = p[:, :, 0] - lam * p[:, :, 1]
        return jnp.einsum('bhqk,bkhe->bqhe', a.astype(v.dtype), v)

    o = lax.map(block, (qb, jnp.arange(nblk, dtype=jnp.int32)))
    return o.transpose(1, 0, 2, 3, 4).reshape(b, s, h, v.shape[-1])


def s5_direction(u, lam_re, lam_im, log_dt, b_re, b_im, c_re, c_im):
    dt = jnp.exp(log_dt)[:, None]
    mag = jnp.exp(lam_re * dt)
    ang = lam_im * dt
    ar = mag * jnp.cos(ang)
    ai = mag * jnp.sin(ang)
    nr = ar - 1.0
    den = lam_re * lam_re + lam_im * lam_im
    cr = (nr * lam_re + ai * lam_im) / den
    ci = (ai * lam_re - nr * lam_im) / den
    bbr = cr[..., None] * b_re - ci[..., None] * b_im
    bbi = cr[..., None] * b_im + ci[..., None] * b_re
    bu_r = jnp.einsum('bsgh,gph->bsgp', u, bbr)
    bu_i = jnp.einsum('bsgh,gph->bsgp', u, bbi)
    a_r = jnp.broadcast_to(ar, bu_r.shape)
    a_i = jnp.broadcast_to(ai, bu_r.shape)

    def combine(e1, e2):
        a1r, a1i, b1r, b1i = e1
        a2r, a2i, b2r, b2i = e2
        return (a2r * a1r - a2i * a1i,
                a2r * a1i + a2i * a1r,
                a2r * b1r - a2i * b1i + b2r,
                a2r * b1i + a2i * b1r + b2i)

    _, _, xr, xi = lax.associative_scan(combine, (a_r, a_i, bu_r, bu_i), axis=1)
    return jnp.einsum('bsgp,ghp->bsgh', xr, c_re) - jnp.einsum('bsgp,ghp->bsgh', xi, c_im)


def bi_s5_glu(u, lam_re, lam_im, log_dt, b_re, b_im, c_re, c_im, d_skip, w_glu, b_glu):
    bsz, s, _ = u.shape
    uf = u.astype(jnp.float32).reshape(bsz, s, SSM_GROUPS, SSM_GROUP)
    f32 = lambda t: t.astype(jnp.float32)
    y_f = s5_direction(uf, f32(lam_re[0]), f32(lam_im[0]), f32(log_dt[0]), f32(b_re[0]), f32(b_im[0]), f32(c_re[0]), f32(c_im[0]))
    y_b = s5_direction(uf[:, ::-1], f32(lam_re[1]), f32(lam_im[1]), f32(log_dt[1]), f32(b_re[1]), f32(b_im[1]), f32(c_re[1]), f32(c_im[1]))[:, ::-1]
    y = (y_f + y_b).reshape(bsz, s, SSM_WIDTH) + f32(d_skip) * u.astype(jnp.float32)
    z = jax.nn.gelu(y, approximate=False)
    out = z * jax.nn.sigmoid(z @ f32(w_glu) + f32(b_glu))
    return out.astype(u.dtype)


def setup_inputs(seed: int = 0) -> dict:
    key = jax.random.key(seed)
    ks = jax.random.split(key, 24)
    f = jnp.float32
    nrm = lambda k, shape, std: jax.random.normal(k, shape, f) * std
    x = jax.random.normal(ks[0], (BATCH, SEQ, D_MODEL), f)
    norm1_g = 1.0 + nrm(ks[1], (DEPTH, D_MODEL), 0.02)
    w_in = nrm(ks[2], (DEPTH, D_MODEL, IN_WIDTH), D_MODEL ** -0.5)
    lam_q1 = nrm(ks[3], (DEPTH, ATTN_QKDIM), 0.1)
    lam_k1 = nrm(ks[4], (DEPTH, ATTN_QKDIM), 0.1)
    lam_q2 = nrm(ks[5], (DEPTH, ATTN_QKDIM), 0.1)
    lam_k2 = nrm(ks[6], (DEPTH, ATTN_QKDIM), 0.1)
    subln_g = 1.0 + nrm(ks[7], (DEPTH, ATTN_VDIM), 0.02)
    n_idx = jnp.arange(SSM_STATE, dtype=f)
    sshape = (DEPTH, 2, SSM_GROUPS, SSM_STATE)
    ssm_lam_re = -0.5 * jnp.exp(nrm(ks[8], sshape, 0.02))
    ssm_lam_im = math.pi * n_idx + nrm(ks[9], sshape, 0.02)
    ssm_log_dt = jax.random.uniform(ks[10], (DEPTH, 2, SSM_GROUPS), f, math.log(1e-3), math.log(1e-1))
    bshape = (DEPTH, 2, SSM_GROUPS, SSM_STATE, SSM_GROUP)
    cshape = (DEPTH, 2, SSM_GROUPS, SSM_GROUP, SSM_STATE)
    ssm_b_re = nrm(ks[11], bshape, (2.0 * SSM_GROUP) ** -0.5)
    ssm_b_im = nrm(ks[12], bshape, (2.0 * SSM_GROUP) ** -0.5)
    ssm_c_re = nrm(ks[13], cshape, (2.0 * SSM_STATE) ** -0.5)
    ssm_c_im = nrm(ks[14], cshape, (2.0 * SSM_STATE) ** -0.5)
    ssm_d = nrm(ks[15], (DEPTH, SSM_WIDTH), 1.0)
    w_glu = nrm(ks[16], (DEPTH, SSM_WIDTH, SSM_WIDTH), SSM_WIDTH ** -0.5)
    b_glu = nrm(ks[17], (DEPTH, SSM_WIDTH), 0.02)
    w_out = nrm(ks[18], (DEPTH, MIX_WIDTH, D_MODEL), MIX_WIDTH ** -0.5)
    norm2_g = 1.0 + nrm(ks[19], (DEPTH, D_MODEL), 0.02)
    w_mlp1 = nrm(ks[20], (DEPTH, D_MODEL, D_FF), D_MODEL ** -0.5)
    w_mlp2 = nrm(ks[21], (DEPTH, D_FF, D_MODEL), D_FF ** -0.5)
    final_g = 1.0 + nrm(ks[22], (D_MODEL,), 0.02)
    return {'x': x, 'norm1_g': norm1_g, 'w_in': w_in, 'lam_q1': lam_q1, 'lam_k1': lam_k1,
            'lam_q2': lam_q2, 'lam_k2': lam_k2, 'subln_g': subln_g,
            'ssm_lam_re': ssm_lam_re, 'ssm_lam_im': ssm_lam_im, 'ssm_log_dt': ssm_log_dt,
            'ssm_b_re': ssm_b_re, 'ssm_b_im': ssm_b_im, 'ssm_c_re': ssm_c_re, 'ssm_c_im': ssm_c_im,
            'ssm_d': ssm_d, 'w_glu': w_glu, 'b_glu': b_glu, 'w_out': w_out, 'norm2_g': norm2_g,
            'w_mlp1': w_mlp1, 'w_mlp2': w_mlp2, 'final_g': final_g}


def reference(x, norm1_g, w_in, lam_q1, lam_k1, lam_q2, lam_k2, subln_g,
              ssm_lam_re, ssm_lam_im, ssm_log_dt, ssm_b_re, ssm_b_im, ssm_c_re, ssm_c_im,
              ssm_d, w_glu, b_glu, w_out, norm2_g, w_mlp1, w_mlp2, final_g):
    bsz, s, _ = x.shape
    h = x
    for l in range(DEPTH):
        lambda_init = 0.8 - 0.6 * math.exp(-0.3 * l)
        hn = rmsnorm(h, norm1_g[l])
        proj = hn @ w_in[l]
        q = proj[..., :ATTN_WIDTH].reshape(bsz, s, ATTN_HEADS, 2, ATTN_QKDIM)
        k = proj[..., ATTN_WIDTH:2 * ATTN_WIDTH].reshape(bsz, s, ATTN_HEADS, 2, ATTN_QKDIM)
        v = proj[..., 2 * ATTN_WIDTH:3 * ATTN_WIDTH].reshape(bsz, s, ATTN_HEADS, ATTN_VDIM)
        u = proj[..., 3 * ATTN_WIDTH:]
        lam = (jnp.exp(jnp.sum(lam_q1[l].astype(jnp.float32) * lam_k1[l].astype(jnp.float32)))
               - jnp.exp(jnp.sum(lam_q2[l].astype(jnp.float32) * lam_k2[l].astype(jnp.float32)))
               + lambda_init)
        att = diff_attention(q, k, v, lam)
        att = rmsnorm(att, subln_g[l]) * (1.0 - lambda_init)
        ssm = bi_s5_glu(u, ssm_lam_re[l], ssm_lam_im[l], ssm_log_dt[l], ssm_b_re[l], ssm_b_im[l],
                        ssm_c_re[l], ssm_c_im[l], ssm_d[l], w_glu[l], b_glu[l])
        mix = jnp.concatenate([att.reshape(bsz, s, ATTN_WIDTH).astype(h.dtype), ssm.astype(h.dtype)], axis=-1)
        h = h + mix @ w_out[l]
        hn2 = rmsnorm(h, norm2_g[l])
        h = h + jnp.square(jax.nn.relu(hn2 @ w_mlp1[l])) @ w_mlp2[l]
    return rmsnorm(h, final_g)
```

```python
import functools
import math

import jax
import jax.numpy as jnp
from jax import lax
from jax.experimental import pallas as pl
from jax.experimental.pallas import tpu as pltpu

F32 = jnp.float32
BF16 = jnp.bfloat16

EPS = 1e-6
ATTN_HEADS = 4
SSM_GROUP = 16
SSM_STATE = 64
LANES = 128
GROUPS_PER_TILE = LANES // SSM_GROUP
VMEM_LIMIT = 56 * 1024 * 1024


def _rms(x, g):
    return x * lax.rsqrt(jnp.mean(x * x, axis=-1, keepdims=True) + EPS) * g


def _inproj_kernel(x_ref, g_ref, w_ref, qkv_ref, u_ref, *, q_width, qkv_width, scale, chunk):
    hn = _rms(x_ref[...], g_ref[...]).astype(BF16)
    n_out = w_ref.shape[1]
    for c0 in range(0, n_out, chunk):
        p = jnp.dot(hn, w_ref[:, c0:c0 + chunk], preferred_element_type=F32)
        if c0 < q_width:
            p = p * scale
        if c0 < qkv_width:
            qkv_ref[:, c0:c0 + chunk] = p.astype(BF16)
        else:
            u_ref[:, c0 - qkv_width:c0 - qkv_width + chunk] = p


def _inproj(x2, g1, w_in, *, q_width, qkv_width, scale, tm=512):
    m, d = x2.shape
    n_out = w_in.shape[1]
    kern = functools.partial(_inproj_kernel, q_width=q_width, qkv_width=qkv_width, scale=scale, chunk=512)
    return pl.pallas_call(
        kern,
        grid=(m // tm,),
        in_specs=[
            pl.BlockSpec((tm, d), lambda i: (i, 0)),
            pl.BlockSpec((1, d), lambda i: (0, 0)),
            pl.BlockSpec((d, n_out), lambda i: (0, 0)),
        ],
        out_specs=[
            pl.BlockSpec((tm, qkv_width), lambda i: (i, 0)),
            pl.BlockSpec((tm, n_out - qkv_width), lambda i: (i, 0)),
        ],
        out_shape=[
            jax.ShapeDtypeStruct((m, qkv_width), BF16),
            jax.ShapeDtypeStruct((m, n_out - qkv_width), F32),
        ],
        compiler_params=pltpu.CompilerParams(
            dimension_semantics=("arbitrary",), vmem_limit_bytes=VMEM_LIMIT),
        name="inproj",
    )(x2, g1, w_in)


def _attn_kernel(lq1_ref, lk1_ref, lq2_ref, lk2_ref, sg_ref, q_ref, k_ref, v_ref, o_ref,
                 *, qb, lambda_init):
    h = pl.program_id(1)
    qi = pl.program_id(2)
    s_len = k_ref.shape[1]
    dh = q_ref.shape[2] // 2

    lam = (jnp.exp(jnp.sum(lq1_ref[...] * lk1_ref[...], axis=-1, keepdims=True))
           - jnp.exp(jnp.sum(lq2_ref[...] * lk2_ref[...], axis=-1, keepdims=True))
           + lambda_init)

    slope = jnp.float32(2.0 ** -8.0)
    for hh in range(ATTN_HEADS - 1):
        slope = jnp.where(h == hh, jnp.float32(2.0 ** (-8.0 * (hh + 1) / ATTN_HEADS)), slope)
    qpos = qi * qb + lax.broadcasted_iota(jnp.int32, (qb, s_len), 0)
    kpos = lax.broadcasted_iota(jnp.int32, (qb, s_len), 1)
    bias = jnp.abs(qpos - kpos).astype(F32) * (-slope)

    q = q_ref[0]
    k = k_ref[0]
    v = v_ref[0]
    probs = []
    for c in range(2):
        sc = lax.dot_general(q[:, c * dh:(c + 1) * dh], k[:, c * dh:(c + 1) * dh],
                             (((1,), (1,)), ((), ())), preferred_element_type=F32)
        sc = sc + bias
        mx = jnp.max(sc, axis=-1, keepdims=True)
        e = jnp.exp(sc - mx)
        probs.append(e / jnp.sum(e, axis=-1, keepdims=True))
    a = probs[0] - lam * probs[1]
    o = jnp.dot(a.astype(BF16), v, preferred_element_type=F32)
    o = _rms(o, sg_ref[...]) * (1.0 - lambda_init)
    o_ref[0] = o.astype(o_ref.dtype)


def _attention(qkv3, lq1, lk1, lq2, lk2, subln_g, *, lambda_init, qb=256):
    b, s, w = qkv3.shape
    hv = w // (3 * ATTN_HEADS)
    kern = functools.partial(_attn_kernel, qb=qb, lambda_init=lambda_init)
    small = lambda shape: pl.BlockSpec(shape, lambda bi, hi, qi: (0, 0))
    return pl.pallas_call(
        kern,
        grid=(b, ATTN_HEADS, s // qb),
        in_specs=[
            small(lq1.shape), small(lk1.shape), small(lq2.shape), small(lk2.shape), small(subln_g.shape),
            pl.BlockSpec((1, qb, hv), lambda bi, hi, qi: (bi, qi, hi)),
            pl.BlockSpec((1, s, hv), lambda bi, hi, qi: (bi, 0, ATTN_HEADS + hi)),
            pl.BlockSpec((1, s, hv), lambda bi, hi, qi: (bi, 0, 2 * ATTN_HEADS + hi)),
        ],
        out_specs=pl.BlockSpec((1, qb, hv), lambda bi, hi, qi: (bi, qi, hi)),
        out_shape=jax.ShapeDtypeStruct((b, s, ATTN_HEADS * hv), BF16),
        compiler_params=pltpu.CompilerParams(
            dimension_semantics=("arbitrary", "arbitrary", "arbitrary"), vmem_limit_bytes=VMEM_LIMIT),
        name="diff_attn",
    )(lq1, lk1, lq2, lk2, subln_g, qkv3, qkv3, qkv3)


def _s5_disc_kernel(lr_ref, li_ref, ldt_ref, bre_ref, bim_ref, ar_ref, ai_ref, bbr_ref, bbi_ref):
    lr = lr_ref[...]
    li = li_ref[...]
    dt = jnp.exp(ldt_ref[...])
    mag = jnp.exp(lr * dt)
    ang = li * dt
    ar = mag * jnp.cos(ang)
    ai = mag * jnp.sin(ang)
    nr = ar - 1.0
    den = lr * lr + li * li
    cr = (nr * lr + ai * li) / den
    ci = (ai * lr - nr * li) / den
    b_re = bre_ref[...]
    b_im = bim_ref[...]
    ar_ref[...] = ar
    ai_ref[...] = ai
    bbr_ref[...] = cr * b_re - ci * b_im
    bbi_ref[...] = cr * b_im + ci * b_re


def _s5_discretise(lam_re, lam_im, log_dt, b_re, b_im):
    nd, g, p, hc = b_re.shape
    flat = lambda t: t.astype(F32).reshape(nd * g, p * hc)
    rep = lambda t: flat(jnp.broadcast_to(t, b_re.shape))
    args = (rep(lam_re[..., None]), rep(lam_im[..., None]), rep(log_dt[..., None, None]), flat(b_re), flat(b_im))
    shp = jax.ShapeDtypeStruct((nd * g, p * hc), F32)
    ar, ai, bbr, bbi = pl.pallas_call(_s5_disc_kernel, out_shape=[shp] * 4, name="s5_disc")(*args)
    unflat = lambda t: t.reshape(nd, g, p, hc)
    return unflat(ar)[..., 0], unflat(ai)[..., 0], unflat(bbr), unflat(bbi)


def _s5_scan_kernel(uf_ref, ub_ref, a_ref, bre_ref, bim_ref, cre_ref, cim_ref, yf_ref, yb_ref,
                    xs_re, xs_im, st_ref, *, tt, nb, strip):
    n_tiles = uf_ref.shape[2] // LANES
    tile_states = GROUPS_PER_TILE * SSM_STATE
    n_state = n_tiles * tile_states

    @pl.when(pl.program_id(0) == 0)
    def _():
        st_ref[...] = jnp.zeros_like(st_ref)

    for d, (u_ref, y_ref) in enumerate(((uf_ref, yf_ref), (ub_ref, yb_ref))):
        u = u_ref[...].reshape(tt * nb, n_tiles * LANES).astype(BF16)
        for o in range(n_tiles):
            uo = u[:, o * LANES:(o + 1) * LANES]
            sl = slice(o * tile_states, (o + 1) * tile_states)
            xs_re[:, sl] = jnp.dot(uo, bre_ref[d, o], preferred_element_type=F32)
            xs_im[:, sl] = jnp.dot(uo, bim_ref[d, o], preferred_element_type=F32)

        for l0 in range(0, n_state, strip):
            ls = slice(l0, l0 + strip)
            ar = jnp.broadcast_to(a_ref[2 * d:2 * d + 1, ls], (nb, strip))
            ai = jnp.broadcast_to(a_ref[2 * d + 1:2 * d + 2, ls], (nb, strip))

            def body(j, carry, ar=ar, ai=ai, ls=ls, d=d):
                xr, xi = carry
                t = j if d == 0 else tt - 1 - j
                rows = pl.ds(pl.multiple_of(t * nb, nb), nb)
                nxr = ar * xr - ai * xi + xs_re[rows, ls]
                nxi = ar * xi + ai * xr + xs_im[rows, ls]
                xs_re[rows, ls] = nxr
                xs_im[rows, ls] = nxi
                return nxr, nxi

            xr, xi = lax.fori_loop(0, tt, body, (st_ref[2 * d, :, ls], st_ref[2 * d + 1, :, ls]), unroll=8)
            st_ref[2 * d, :, ls] = xr
            st_ref[2 * d + 1, :, ls] = xi

        for o in range(n_tiles):
            sl = slice(o * tile_states, (o + 1) * tile_states)
            yo = (jnp.dot(xs_re[:, sl].astype(BF16), cre_ref[d, o], preferred_element_type=F32)
                  - jnp.dot(xs_im[:, sl].astype(BF16), cim_ref[d, o], preferred_element_type=F32))
            y_ref[:, :, o * LANES:(o + 1) * LANES] = yo.reshape(tt, nb, LANES)


def _s5_scan(u_t, a_all, bre, bim, cre, cim, *, tt=64):
    s, nb, w = u_t.shape
    nt = s // tt
    n_state = a_all.shape[1]
    kern = functools.partial(_s5_scan_kernel, tt=tt, nb=nb, strip=512)
    full = lambda arr: pl.BlockSpec(arr.shape, lambda i: (0,) * arr.ndim)
    blk = (tt, nb, w)
    return pl.pallas_call(
        kern,
        grid=(nt,),
        in_specs=[
            pl.BlockSpec(blk, lambda i: (i, 0, 0)),
            pl.BlockSpec(blk, lambda i: (nt - 1 - i, 0, 0)),
            full(a_all), full(bre), full(bim), full(cre), full(cim),
        ],
        out_specs=[
            pl.BlockSpec(blk, lambda i: (i, 0, 0)),
            pl.BlockSpec(blk, lambda i: (nt - 1 - i, 0, 0)),
        ],
        out_shape=[jax.ShapeDtypeStruct(u_t.shape, F32)] * 2,
        scratch_shapes=[
            pltpu.VMEM((tt * nb, n_state), F32),
            pltpu.VMEM((tt * nb, n_state), F32),
            pltpu.VMEM((4, nb, n_state), F32),
        ],
        compiler_params=pltpu.CompilerParams(
            dimension_semantics=("arbitrary",), vmem_limit_bytes=VMEM_LIMIT),
        name="s5_scan",
    )(u_t, u_t, a_all, bre, bim, cre, cim)


def _block_diag_tiles(m):
    g, k, n = m.shape
    t = g // GROUPS_PER_TILE
    m = m.reshape(t, GROUPS_PER_TILE, k, n)
    eye = jnp.eye(GROUPS_PER_TILE, dtype=m.dtype)
    out = m[:, :, :, None, :] * eye[None, :, None, :, None]
    return out.reshape(t, GROUPS_PER_TILE * k, GROUPS_PER_TILE * n)


def _post_kernel(x_ref, att_ref, yf_ref, yb_ref, u_ref, dsk_ref, wg_ref, bg_ref, wo_ref, g2_ref,
                 w1_ref, w2_ref, gf_ref, o_ref, *, ff_chunk, final_norm):
    aw = att_ref.shape[1]
    u = u_ref[...]
    y = yf_ref[...] + yb_ref[...] + dsk_ref[...] * u
    z = 0.5 * y * (1.0 + lax.erf(y * (1.0 / math.sqrt(2.0))))
    gate = jax.nn.sigmoid(jnp.dot(z.astype(BF16), wg_ref[...], preferred_element_type=F32) + bg_ref[...])
    ssm = (z * gate).astype(BF16)
    h = (x_ref[...]
         + jnp.dot(att_ref[...], wo_ref[:aw, :], preferred_element_type=F32)
         + jnp.dot(ssm, wo_ref[aw:, :], preferred_element_type=F32))
    hn2 = _rms(h, g2_ref[...]).astype(BF16)
    mlp = jnp.zeros_like(h)
    for f0 in range(0, w1_ref.shape[1], ff_chunk):
        a = jnp.maximum(jnp.dot(hn2, w1_ref[:, f0:f0 + ff_chunk], preferred_element_type=F32), 0.0)
        mlp = mlp + jnp.dot((a * a).astype(BF16), w2_ref[f0:f0 + ff_chunk, :], preferred_element_type=F32)
    h = h + mlp
    o_ref[...] = _rms(h, gf_ref[...]) if final_norm else h


def _post(x2, att, yf, yb, u, d_skip, w_glu, b_glu, w_out, g2, w1, w2, gf, *, final_norm, tm=256):
    m, d = x2.shape
    aw = att.shape[1]
    sw = u.shape[1]
    row = lambda width: pl.BlockSpec((tm, width), lambda i: (i, 0))
    const = lambda arr: pl.BlockSpec(arr.shape, lambda i: (0, 0))
    kern = functools.partial(_post_kernel, ff_chunk=1024, final_norm=final_norm)
    return pl.pallas_call(
        kern,
        grid=(m // tm,),
        in_specs=[row(d), row(aw), row(sw), row(sw), row(sw),
                  const(d_skip), const(w_glu), const(b_glu), const(w_out), const(g2),
                  const(w1), const(w2), const(gf)],
        out_specs=row(d),
        out_shape=jax.ShapeDtypeStruct((m, d), F32),
        compiler_params=pltpu.CompilerParams(
            dimension_semantics=("arbitrary",), vmem_limit_bytes=VMEM_LIMIT),
        name="post_mlp",
    )(x2, att, yf, yb, u, d_skip, w_glu, b_glu, w_out, g2, w1, w2, gf)


def kernel(x, norm1_g, w_in, lam_q1, lam_k1, lam_q2, lam_k2, subln_g, ssm_lam_re, ssm_lam_im, ssm_log_dt,
           ssm_b_re, ssm_b_im, ssm_c_re, ssm_c_im, ssm_d, w_glu, b_glu, w_out, norm2_g, w_mlp1, w_mlp2,
           final_g):
    bsz, s, d = x.shape
    depth = w_in.shape[0]
    ssm_w = ssm_d.shape[-1]
    attn_w = w_out.shape[1] - ssm_w
    qk_dim = attn_w // ATTN_HEADS // 2
    scale = float(qk_dim) ** -0.5
    m = bsz * s

    h2 = x.reshape(m, d).astype(F32)
    for l in range(depth):
        lambda_init = 0.8 - 0.6 * math.exp(-0.3 * l)
        qkv, u = _inproj(h2, norm1_g[l][None].astype(F32), w_in[l].astype(BF16),
                         q_width=attn_w, qkv_width=3 * attn_w, scale=scale)

        row = lambda t: t[l][None].astype(F32)
        att = _attention(qkv.reshape(bsz, s, 3 * attn_w), row(lam_q1), row(lam_k1), row(lam_q2), row(lam_k2),
                         row(subln_g), lambda_init=lambda_init)

        ar, ai, bbr, bbi = _s5_discretise(ssm_lam_re[l], ssm_lam_im[l], ssm_log_dt[l], ssm_b_re[l], ssm_b_im[l])
        nd, g, p = ar.shape
        a_all = jnp.stack([ar[0], ai[0], ar[1], ai[1]]).reshape(4, g * p)
        to_in = lambda t: jnp.stack([_block_diag_tiles(jnp.swapaxes(t[dd], 1, 2)) for dd in range(nd)]).astype(BF16)
        to_out = lambda t: jnp.stack([_block_diag_tiles(jnp.swapaxes(t[dd].astype(F32), 1, 2))
                                      for dd in range(nd)]).astype(BF16)
        u_t = jnp.swapaxes(u.reshape(bsz, s, ssm_w), 0, 1)
        yf_t, yb_t = _s5_scan(u_t, a_all, to_in(bbr), to_in(bbi), to_out(ssm_c_re[l]), to_out(ssm_c_im[l]))
        yf = jnp.swapaxes(yf_t, 0, 1).reshape(m, ssm_w)
        yb = jnp.swapaxes(yb_t, 0, 1).reshape(m, ssm_w)

        h2 = _post(h2, att.reshape(m, attn_w), yf, yb, u, row(ssm_d), w_glu[l].astype(BF16), row(b_glu),
                   w_out[l].astype(BF16), row(norm2_g), w_mlp1[l].astype(BF16), w_mlp2[l].astype(BF16),
                   final_g[None].astype(F32), final_norm=(l == depth - 1))
    return h2.reshape(bsz, s, d).astype(x.dtype)
```

```python
import functools
import math

import jax
import jax.numpy as jnp
from jax import lax
from jax.experimental import pallas as pl
from jax.experimental.pallas import tpu as pltpu

F32 = jnp.float32
BF16 = jnp.bfloat16

EPS = 1e-6
ATTN_HEADS = 4
SSM_GROUP = 16
SSM_STATE = 64
LANES = 128
BF16_SUBLANES = 16
GROUPS_PER_TILE = LANES // SSM_GROUP
VMEM_LIMIT = 56 * 1024 * 1024


def _rms(x, g):
    return x * lax.rsqrt(jnp.mean(x * x, axis=-1, keepdims=True) + EPS) * g


def _inproj_kernel(x_ref, g_ref, w_ref, qkv_ref, u_ref, *, q_width, qkv_width, scale, chunk):
    hn = _rms(x_ref[...], g_ref[...]).astype(BF16)
    n_out = w_ref.shape[1]
    for c0 in range(0, n_out, chunk):
        p = jnp.dot(hn, w_ref[:, c0:c0 + chunk], preferred_element_type=F32)
        if c0 < q_width:
            p = p * scale
        if c0 < qkv_width:
            qkv_ref[:, c0:c0 + chunk] = p.astype(BF16)
        else:
            u_ref[:, c0 - qkv_width:c0 - qkv_width + chunk] = p


def _inproj(x2, g1, w_in, *, q_width, qkv_width, scale, tm=512):
    m, d = x2.shape
    n_out = w_in.shape[1]
    kern = functools.partial(_inproj_kernel, q_width=q_width, qkv_width=qkv_width, scale=scale, chunk=512)
    return pl.pallas_call(
        kern,
        grid=(m // tm,),
        in_specs=[
            pl.BlockSpec((tm, d), lambda i: (i, 0)),
            pl.BlockSpec((1, d), lambda i: (0, 0)),
            pl.BlockSpec((d, n_out), lambda i: (0, 0)),
        ],
        out_specs=[
            pl.BlockSpec((tm, qkv_width), lambda i: (i, 0)),
            pl.BlockSpec((tm, n_out - qkv_width), lambda i: (i, 0)),
        ],
        out_shape=[
            jax.ShapeDtypeStruct((m, qkv_width), BF16),
            jax.ShapeDtypeStruct((m, n_out - qkv_width), F32),
        ],
        compiler_params=pltpu.CompilerParams(
            dimension_semantics=("arbitrary",), vmem_limit_bytes=VMEM_LIMIT),
        name="inproj",
    )(x2, g1, w_in)


POS_LOW_BITS = 3


def _attn_kernel(lq1_ref, lk1_ref, lq2_ref, lk2_ref, sg_ref, q_ref, k_ref, v_ref, o_ref,
                 kx_ref, vt_ref, s00_ref, s01_ref, s10_ref, s11_ref, e0_ref, e1_ref, *, qb, lambda_init):
    h = pl.program_id(1)
    s_refs, e_refs = ((s00_ref, s01_ref), (s10_ref, s11_ref)), (e0_ref, e1_ref)
    s_len, hv = k_ref.shape[1], k_ref.shape[2]
    dh = hv // 2
    low_mask = (1 << POS_LOW_BITS) - 1

    lam = (jnp.exp(jnp.sum(lq1_ref[...] * lk1_ref[...], axis=-1, keepdims=True))
           - jnp.exp(jnp.sum(lq2_ref[...] * lk2_ref[...], axis=-1, keepdims=True))
           + lambda_init)

    slope = jnp.float32(2.0 ** -8.0)
    for hh in range(ATTN_HEADS - 1):
        slope = jnp.where(h == hh, jnp.float32(2.0 ** (-8.0 * (hh + 1) / ATTN_HEADS)), slope)

    lane = lax.broadcasted_iota(jnp.int32, (1, hv), 1)

    def extras(base, vals):
        out = 0.0
        for e, val in enumerate(vals):
            out = jnp.where(lane == base + e, val, out)
        return out

    def split_pos(pos):
        lo = pos & low_mask
        return (pos - lo).astype(F32) * slope, lo.astype(F32) * slope

    data_lanes = (lane < dh, lane >= dh)
    extra_base = (dh, 0)

    cj_hi, cj_lo = split_pos(lax.broadcasted_iota(jnp.int32, (s_len, 1), 0))
    kf = k_ref[0].astype(F32)
    for c in range(2):
        kx = jnp.where(data_lanes[c], kf, extras(extra_base[c], [1.0, 1.0, cj_hi, cj_lo]))
        kx_ref[c] = kx.astype(BF16)
    vt_ref[:hv, :] = v_ref[0].astype(F32).T.astype(BF16)
    pad_row = lax.broadcasted_iota(jnp.int32, (vt_ref.shape[0] - hv, s_len), 0)
    vt_ref[hv:, :] = jnp.where(pad_row == 0, 1.0, 0.0).astype(BF16)

    def score_block(qi):
        lo, hi = qi * qb, (qi + 1) * qb
        ci_hi, ci_lo = split_pos(lo + lax.broadcasted_iota(jnp.int32, (qb, 1), 0))
        qf = q_ref[0, lo:hi, :].astype(F32)
        for c in range(2):
            def q_aug_t(sign, c=c):
                ex = extras(extra_base[c], [-sign * ci_hi, -sign * ci_lo, sign, sign])
                return jnp.where(data_lanes[c], qf, ex).T.astype(BF16)

            q_left, q_right = q_aug_t(1.0), q_aug_t(-1.0)
            dot = lambda r0, r1, qt, c=c: jnp.dot(kx_ref[c, r0:r1, :], qt, preferred_element_type=F32)
            s_ref = s_refs[qi % 2][c]
            if lo > 0:
                s_ref[:lo, :] = dot(0, lo, q_left)
            s_ref[lo:hi, :] = jnp.minimum(dot(lo, hi, q_left), dot(lo, hi, q_right))
            if hi < s_len:
                s_ref[hi:, :] = dot(hi, s_len, q_right)

    def finish_block(qi):
        lo, hi = qi * qb, (qi + 1) * qb
        outs = []
        for c in range(2):
            sc = s_refs[qi % 2][c][...]
            e_refs[c][...] = jnp.exp(sc - jnp.max(sc, axis=0, keepdims=True)).astype(BF16)
            acc = jnp.dot(vt_ref[...], e_refs[c][...], preferred_element_type=F32)
            outs.append(acc[:hv, :] / acc[hv:hv + 1, :])
        o = (outs[0] - lam * outs[1]).T
        o = _rms(o, sg_ref[...]) * (1.0 - lambda_init)
        o_ref[0, lo:hi, :] = o.astype(o_ref.dtype)

    n_blocks = s_len // qb
    score_block(0)
    for qi in range(n_blocks):
        if qi + 1 < n_blocks:
            score_block(qi + 1)
        finish_block(qi)


def _attention(qkv3, lq1, lk1, lq2, lk2, subln_g, *, lambda_init, qb=256):
    b, s, w = qkv3.shape
    hv = w // (3 * ATTN_HEADS)
    assert s <= 256 << POS_LOW_BITS and s % qb == 0
    kern = functools.partial(_attn_kernel, qb=qb, lambda_init=lambda_init)
    small = lambda shape: pl.BlockSpec(shape, lambda bi, hi: (0, 0))
    head = lambda off: pl.BlockSpec((1, s, hv), lambda bi, hi: (bi, 0, off + hi))
    return pl.pallas_call(
        kern,
        grid=(b, ATTN_HEADS),
        in_specs=[
            small(lq1.shape), small(lk1.shape), small(lq2.shape), small(lk2.shape), small(subln_g.shape),
            head(0), head(ATTN_HEADS), head(2 * ATTN_HEADS),
        ],
        out_specs=head(0),
        out_shape=jax.ShapeDtypeStruct((b, s, ATTN_HEADS * hv), BF16),
        scratch_shapes=[
            pltpu.VMEM((2, s, hv), BF16),
            pltpu.VMEM((hv + BF16_SUBLANES, s), BF16),
            pltpu.VMEM((s, qb), F32), pltpu.VMEM((s, qb), F32), pltpu.VMEM((s, qb), F32), pltpu.VMEM((s, qb), F32),
            pltpu.VMEM((s, qb), BF16), pltpu.VMEM((s, qb), BF16),
        ],
        compiler_params=pltpu.CompilerParams(
            dimension_semantics=("arbitrary", "arbitrary"), vmem_limit_bytes=VMEM_LIMIT),
        name="diff_attn",
    )(lq1, lk1, lq2, lk2, subln_g, qkv3, qkv3, qkv3)


def _s5_disc_kernel(lr_ref, li_ref, ldt_ref, bre_ref, bim_ref, ar_ref, ai_ref, bbr_ref, bbi_ref):
    lr = lr_ref[...]
    li = li_ref[...]
    dt = jnp.exp(ldt_ref[...])
    mag = jnp.exp(lr * dt)
    ang = li * dt
    ar = mag * jnp.cos(ang)
    ai = mag * jnp.sin(ang)
    nr = ar - 1.0
    den = lr * lr + li * li
    cr = (nr * lr + ai * li) / den
    ci = (ai * lr - nr * li) / den
    b_re = bre_ref[...]
    b_im = bim_ref[...]
    ar_ref[...] = ar
    ai_ref[...] = ai
    bbr_ref[...] = cr * b_re - ci * b_im
    bbi_ref[...] = cr * b_im + ci * b_re


def _s5_discretise(lam_re, lam_im, log_dt, b_re, b_im):
    nd, g, p, hc = b_re.shape
    flat = lambda t: t.astype(F32).reshape(nd * g, p * hc)
    rep = lambda t: flat(jnp.broadcast_to(t, b_re.shape))
    args = (rep(lam_re[..., None]), rep(lam_im[..., None]), rep(log_dt[..., None, None]), flat(b_re), flat(b_im))
    shp = jax.ShapeDtypeStruct((nd * g, p * hc), F32)
    ar, ai, bbr, bbi = pl.pallas_call(_s5_disc_kernel, out_shape=[shp] * 4, name="s5_disc")(*args)
    unflat = lambda t: t.reshape(nd, g, p, hc)
    return unflat(ar)[..., 0], unflat(ai)[..., 0], unflat(bbr), unflat(bbi)


def _s5_scan_kernel(uf_ref, ub_ref, a_ref, bre_ref, bim_ref, cre_ref, cim_ref, yf_ref, yb_ref,
                    xs_re, xs_im, st_ref, *, tt, nb, strip):
    n_tiles = uf_ref.shape[2] // LANES
    tile_states = GROUPS_PER_TILE * SSM_STATE
    n_state = n_tiles * tile_states

    @pl.when(pl.program_id(0) == 0)
    def _():
        st_ref[...] = jnp.zeros_like(st_ref)

    for d, (u_ref, y_ref) in enumerate(((uf_ref, yf_ref), (ub_ref, yb_ref))):
        u = u_ref[...].reshape(tt * nb, n_tiles * LANES).astype(BF16)
        for o in range(n_tiles):
            uo = u[:, o * LANES:(o + 1) * LANES]
            sl = slice(o * tile_states, (o + 1) * tile_states)
            xs_re[:, sl] = jnp.dot(uo, bre_ref[d, o], preferred_element_type=F32)
            xs_im[:, sl] = jnp.dot(uo, bim_ref[d, o], preferred_element_type=F32)

        for l0 in range(0, n_state, strip):
            ls = slice(l0, l0 + strip)
            ar = jnp.broadcast_to(a_ref[2 * d:2 * d + 1, ls], (nb, strip))
            ai = jnp.broadcast_to(a_ref[2 * d + 1:2 * d + 2, ls], (nb, strip))

            def body(j, carry, ar=ar, ai=ai, ls=ls, d=d):
                xr, xi = carry
                t = j if d == 0 else tt - 1 - j
                rows = pl.ds(pl.multiple_of(t * nb, nb), nb)
                nxr = ar * xr - ai * xi + xs_re[rows, ls]
                nxi = ar * xi + ai * xr + xs_im[rows, ls]
                xs_re[rows, ls] = nxr
                xs_im[rows, ls] = nxi
                return nxr, nxi

            xr, xi = lax.fori_loop(0, tt, body, (st_ref[2 * d, :, ls], st_ref[2 * d + 1, :, ls]), unroll=8)
            st_ref[2 * d, :, ls] = xr
            st_ref[2 * d + 1, :, ls] = xi

        for o in range(n_tiles):
            sl = slice(o * tile_states, (o + 1) * tile_states)
            yo = (jnp.dot(xs_re[:, sl].astype(BF16), cre_ref[d, o], preferred_element_type=F32)
                  - jnp.dot(xs_im[:, sl].astype(BF16), cim_ref[d, o], preferred_element_type=F32))
            y_ref[:, :, o * LANES:(o + 1) * LANES] = yo.reshape(tt, nb, LANES)


def _s5_scan(u_t, a_all, bre, bim, cre, cim, *, tt=64):
    s, nb, w = u_t.shape
    nt = s // tt
    n_state = a_all.shape[1]
    kern = functools.partial(_s5_scan_kernel, tt=tt, nb=nb, strip=512)
    full = lambda arr: pl.BlockSpec(arr.shape, lambda i: (0,) * arr.ndim)
    blk = (tt, nb, w)
    return pl.pallas_call(
        kern,
        grid=(nt,),
        in_specs=[
            pl.BlockSpec(blk, lambda i: (i, 0, 0)),
            pl.BlockSpec(blk, lambda i: (nt - 1 - i, 0, 0)),
            full(a_all), full(bre), full(bim), full(cre), full(cim),
        ],
        out_specs=[
            pl.BlockSpec(blk, lambda i: (i, 0, 0)),
            pl.BlockSpec(blk, lambda i: (nt - 1 - i, 0, 0)),
        ],
        out_shape=[jax.ShapeDtypeStruct(u_t.shape, F32)] * 2,
        scratch_shapes=[
            pltpu.VMEM((tt * nb, n_state), F32),
            pltpu.VMEM((tt * nb, n_state), F32),
            pltpu.VMEM((4, nb, n_state), F32),
        ],
        compiler_params=pltpu.CompilerParams(
            dimension_semantics=("arbitrary",), vmem_limit_bytes=VMEM_LIMIT),
        name="s5_scan",
    )(u_t, u_t, a_all, bre, bim, cre, cim)


def _block_diag_tiles(m):
    g, k, n = m.shape
    t = g // GROUPS_PER_TILE
    m = m.reshape(t, GROUPS_PER_TILE, k, n)
    eye = jnp.eye(GROUPS_PER_TILE, dtype=m.dtype)
    out = m[:, :, :, None, :] * eye[None, :, None, :, None]
    return out.reshape(t, GROUPS_PER_TILE * k, GROUPS_PER_TILE * n)


def _post_kernel(x_ref, att_ref, yf_ref, yb_ref, u_ref, dsk_ref, wg_ref, bg_ref, wo_ref, g2_ref,
                 w1_ref, w2_ref, gf_ref, o_ref, *, ff_chunk, final_norm):
    aw = att_ref.shape[1]
    u = u_ref[...]
    y = yf_ref[...] + yb_ref[...] + dsk_ref[...] * u
    z = 0.5 * y * (1.0 + lax.erf(y * (1.0 / math.sqrt(2.0))))
    gate = jax.nn.sigmoid(jnp.dot(z.astype(BF16), wg_ref[...], preferred_element_type=F32) + bg_ref[...])
    ssm = (z * gate).astype(BF16)
    h = (x_ref[...]
         + jnp.dot(att_ref[...], wo_ref[:aw, :], preferred_element_type=F32)
         + jnp.dot(ssm, wo_ref[aw:, :], preferred_element_type=F32))
    hn2 = _rms(h, g2_ref[...]).astype(BF16)
    mlp = jnp.zeros_like(h)
    for f0 in range(0, w1_ref.shape[1], ff_chunk):
        a = jnp.maximum(jnp.dot(hn2, w1_ref[:, f0:f0 + ff_chunk], preferred_element_type=F32), 0.0)
        mlp = mlp + jnp.dot((a * a).astype(BF16), w2_ref[f0:f0 + ff_chunk, :], preferred_element_type=F32)
    h = h + mlp
    o_ref[...] = _rms(h, gf_ref[...]) if final_norm else h


def _post(x2, att, yf, yb, u, d_skip, w_glu, b_glu, w_out, g2, w1, w2, gf, *, final_norm, tm=256):
    m, d = x2.shape
    aw = att.shape[1]
    sw = u.shape[1]
    row = lambda width: pl.BlockSpec((tm, width), lambda i: (i, 0))
    const = lambda arr: pl.BlockSpec(arr.shape, lambda i: (0, 0))
    kern = functools.partial(_post_kernel, ff_chunk=1024, final_norm=final_norm)
    return pl.pallas_call(
        kern,
        grid=(m // tm,),
        in_specs=[row(d), row(aw), row(sw), row(sw), row(sw),
                  const(d_skip), const(w_glu), const(b_glu), const(w_out), const(g2),
                  const(w1), const(w2), const(gf)],
        out_specs=row(d),
        out_shape=jax.ShapeDtypeStruct((m, d), F32),
        compiler_params=pltpu.CompilerParams(
            dimension_semantics=("arbitrary",), vmem_limit_bytes=VMEM_LIMIT),
        name="post_mlp",
    )(x2, att, yf, yb, u, d_skip, w_glu, b_glu, w_out, g2, w1, w2, gf)


def kernel(x, norm1_g, w_in, lam_q1, lam_k1, lam_q2, lam_k2, subln_g, ssm_lam_re, ssm_lam_im, ssm_log_dt,
           ssm_b_re, ssm_b_im, ssm_c_re, ssm_c_im, ssm_d, w_glu, b_glu, w_out, norm2_g, w_mlp1, w_mlp2,
           final_g):
    bsz, s, d = x.shape
    depth = w_in.shape[0]
    ssm_w = ssm_d.shape[-1]
    attn_w = w_out.shape[1] - ssm_w
    qk_dim = attn_w // ATTN_HEADS // 2
    scale = float(qk_dim) ** -0.5
    m = bsz * s

    h2 = x.reshape(m, d).astype(F32)
    for l in range(depth):
        lambda_init = 0.8 - 0.6 * math.exp(-0.3 * l)
        qkv, u = _inproj(h2, norm1_g[l][None].astype(F32), w_in[l].astype(BF16),
                         q_width=attn_w, qkv_width=3 * attn_w, scale=scale)

        row = lambda t: t[l][None].astype(F32)
        att = _attention(qkv.reshape(bsz, s, 3 * attn_w), row(lam_q1), row(lam_k1), row(lam_q2), row(lam_k2),
                         row(subln_g), lambda_init=lambda_init)

        ar, ai, bbr, bbi = _s5_discretise(ssm_lam_re[l], ssm_lam_im[l], ssm_log_dt[l], ssm_b_re[l], ssm_b_im[l])
        nd, g, p = ar.shape
        a_all = jnp.stack([ar[0], ai[0], ar[1], ai[1]]).reshape(4, g * p)
        to_in = lambda t: jnp.stack([_block_diag_tiles(jnp.swapaxes(t[dd], 1, 2)) for dd in range(nd)]).astype(BF16)
        to_out = lambda t: jnp.stack([_block_diag_tiles(jnp.swapaxes(t[dd].astype(F32), 1, 2))
                                      for dd in range(nd)]).astype(BF16)
        u_t = jnp.swapaxes(u.reshape(bsz, s, ssm_w), 0, 1)
        yf_t, yb_t = _s5_scan(u_t, a_all, to_in(bbr), to_in(bbi), to_out(ssm_c_re[l]), to_out(ssm_c_im[l]))
        yf = jnp.swapaxes(yf_t, 0, 1).reshape(m, ssm_w)
        yb = jnp.swapaxes(yb_t, 0, 1).reshape(m, ssm_w)

        h2 = _post(h2, att.reshape(m, attn_w), yf, yb, u, row(ssm_d), w_glu[l].astype(BF16), row(b_glu),
                   w_out[l].astype(BF16), row(norm2_g), w_mlp1[l].astype(BF16), w_mlp2[l].astype(BF16),
                   final_g[None].astype(F32), final_norm=(l == depth - 1))
    return h2.reshape(bsz, s, d).astype(x.dtype)
```

```python
import functools
import math

import jax
import jax.numpy as jnp
from jax import lax
from jax.experimental import pallas as pl
from jax.experimental.pallas import tpu as pltpu

F32 = jnp.float32
BF16 = jnp.bfloat16

EPS = 1e-6
ATTN_HEADS = 4
SSM_GROUP = 16
SSM_STATE = 64
LANES = 128
BF16_SUBLANES = 16
GROUPS_PER_TILE = LANES // SSM_GROUP
VMEM_LIMIT = 56 * 1024 * 1024


def _rms(x, g):
    return x * lax.rsqrt(jnp.mean(x * x, axis=-1, keepdims=True) + EPS) * g


def _inproj_kernel(x_ref, g_ref, w_ref, qkv_ref, u_ref, *, q_width, qkv_width, scale, chunk):
    hn = _rms(x_ref[...], g_ref[...]).astype(BF16)
    n_out = w_ref.shape[1]
    for c0 in range(0, n_out, chunk):
        p = jnp.dot(hn, w_ref[:, c0:c0 + chunk], preferred_element_type=F32)
        if c0 < q_width:
            p = p * scale
        if c0 < qkv_width:
            qkv_ref[:, c0:c0 + chunk] = p.astype(BF16)
        else:
            u_ref[:, c0 - qkv_width:c0 - qkv_width + chunk] = p


def _inproj(x2, g1, w_in, *, q_width, qkv_width, scale, tm=512):
    m, d = x2.shape
    n_out = w_in.shape[1]
    kern = functools.partial(_inproj_kernel, q_width=q_width, qkv_width=qkv_width, scale=scale, chunk=512)
    return pl.pallas_call(
        kern,
        grid=(m // tm,),
        in_specs=[
            pl.BlockSpec((tm, d), lambda i: (i, 0)),
            pl.BlockSpec((1, d), lambda i: (0, 0)),
            pl.BlockSpec((d, n_out), lambda i: (0, 0)),
        ],
        out_specs=[
            pl.BlockSpec((tm, qkv_width), lambda i: (i, 0)),
            pl.BlockSpec((tm, n_out - qkv_width), lambda i: (i, 0)),
        ],
        out_shape=[
            jax.ShapeDtypeStruct((m, qkv_width), BF16),
            jax.ShapeDtypeStruct((m, n_out - qkv_width), F32),
        ],
        compiler_params=pltpu.CompilerParams(
            dimension_semantics=("arbitrary",), vmem_limit_bytes=VMEM_LIMIT),
        name="inproj",
    )(x2, g1, w_in)


POS_LOW_BITS = 3
LOG2_E = 1.4426950408889634


def _attn_kernel(lq1_ref, lk1_ref, lq2_ref, lk2_ref, sg_ref, q_ref, k_ref, v_ref, o_ref,
                 kx_ref, vt_ref, s0_ref, s1_ref, s2_ref, s3_ref, e0_ref, e1_ref, *, qb, lambda_init):
    h = pl.program_id(1)
    s_refs, e_refs = (s0_ref, s1_ref, s2_ref, s3_ref), (e0_ref, e1_ref)
    s_len, hv = k_ref.shape[1], k_ref.shape[2]
    dh = hv // 2
    low_mask = (1 << POS_LOW_BITS) - 1

    lam = (jnp.exp(jnp.sum(lq1_ref[...] * lk1_ref[...], axis=-1, keepdims=True))
           - jnp.exp(jnp.sum(lq2_ref[...] * lk2_ref[...], axis=-1, keepdims=True))
           + lambda_init)

    slope = jnp.float32(2.0 ** -8.0)
    for hh in range(ATTN_HEADS - 1):
        slope = jnp.where(h == hh, jnp.float32(2.0 ** (-8.0 * (hh + 1) / ATTN_HEADS)), slope)

    c2 = jnp.full((1, 1), slope, F32) * LOG2_E
    c2_parts = []
    rest = c2
    for _ in range(3):
        part = rest.astype(BF16).astype(F32)
        c2_parts.append(part)
        rest = rest - part

    lane = lax.broadcasted_iota(jnp.int32, (1, hv), 1)

    def extras(vals):
        out = 0.0
        for e, val in enumerate(vals):
            out = jnp.where(lane == dh + e, val, out)
        return out

    def split_pos(pos):
        lo = pos & low_mask
        return (pos - lo).astype(F32), lo.astype(F32)

    data_lanes = (lane < dh, lane >= dh)
    to_half = lambda ex, c: ex if c == 0 else pltpu.roll(ex, dh, 1)

    j_hi, j_lo = split_pos(lax.broadcasted_iota(jnp.int32, (s_len, 1), 0))
    k_extra = extras(c2_parts + c2_parts + [j_hi] * 3 + [j_lo] * 3)
    kf = k_ref[0].astype(F32)
    for c in range(2):
        kx_ref[c] = jnp.where(data_lanes[c], kf, to_half(k_extra, c)).astype(BF16)
    vt_ref[:hv, :] = v_ref[0].astype(F32).T.astype(BF16)
    pad_row = lax.broadcasted_iota(jnp.int32, (vt_ref.shape[0] - hv, s_len), 0)
    vt_ref[hv:, :] = jnp.where(pad_row == 0, 1.0, 0.0).astype(BF16)

    off = lax.broadcasted_iota(jnp.int32, (qb, qb), 0) - lax.broadcasted_iota(jnp.int32, (qb, qb), 1)
    diag_bias = jnp.abs(off).astype(F32) * (-c2)

    q_side = {}

    def scores(n):
        qi, c = divmod(n, 2)
        lo, hi = qi * qb, (qi + 1) * qb
        if qi not in q_side:
            i_hi, i_lo = split_pos(lo + lax.broadcasted_iota(jnp.int32, (qb, 1), 0))
            q_side[qi] = (extras([-i_hi] * 3 + [-i_lo] * 3 + c2_parts + c2_parts),
                          q_ref[0, lo:hi, :].astype(F32))
        q_extra, qf = q_side[qi]
        aug_t = lambda ex: jnp.where(data_lanes[c], qf, to_half(ex, c)).T.astype(BF16)
        dot = lambda r0, r1, qt: jnp.dot(kx_ref[c, r0:r1, :], qt, preferred_element_type=F32)
        s_ref = s_refs[n % 4]
        if lo > 0:
            s_ref[:lo, :] = dot(0, lo, aug_t(q_extra))
        s_ref[lo:hi, :] = dot(lo, hi, aug_t(jnp.zeros_like(q_extra))) + diag_bias
        if hi < s_len:
            s_ref[hi:, :] = dot(hi, s_len, aug_t(-q_extra))

    def weighted_values(n):
        sc = s_refs[n % 4][...]
        e_refs[n % 2][...] = jnp.exp2(sc - jnp.max(sc, axis=0, keepdims=True)).astype(BF16)
        acc = jnp.dot(vt_ref[...], e_refs[n % 2][...], preferred_element_type=F32)
        return acc[:hv, :] / acc[hv:hv + 1, :]

    n_blocks = s_len // qb
    scores(0), scores(1)
    for qi in range(n_blocks):
        if qi + 1 < n_blocks:
            scores(2 * qi + 2), scores(2 * qi + 3)
        out_t = weighted_values(2 * qi), weighted_values(2 * qi + 1)
        o = (out_t[0] - lam * out_t[1]).T
        o = _rms(o, sg_ref[...]) * (1.0 - lambda_init)
        o_ref[0, qi * qb:(qi + 1) * qb, :] = o.astype(o_ref.dtype)


def _attention(qkv3, lq1, lk1, lq2, lk2, subln_g, *, lambda_init, qb=512):
    b, s, w = qkv3.shape
    hv = w // (3 * ATTN_HEADS)
    assert s <= 256 << POS_LOW_BITS and s % qb == 0
    kern = functools.partial(_attn_kernel, qb=qb, lambda_init=lambda_init)
    small = lambda shape: pl.BlockSpec(shape, lambda bi, hi: (0, 0))
    head = lambda off: pl.BlockSpec((1, s, hv), lambda bi, hi: (bi, 0, off + hi))
    return pl.pallas_call(
        kern,
        grid=(b, ATTN_HEADS),
        in_specs=[
            small(lq1.shape), small(lk1.shape), small(lq2.shape), small(lk2.shape), small(subln_g.shape),
            head(0), head(ATTN_HEADS), head(2 * ATTN_HEADS),
        ],
        out_specs=head(0),
        out_shape=jax.ShapeDtypeStruct((b, s, ATTN_HEADS * hv), BF16),
        scratch_shapes=[
            pltpu.VMEM((2, s, hv), BF16),
            pltpu.VMEM((hv + BF16_SUBLANES, s), BF16),
            pltpu.VMEM((s, qb), F32), pltpu.VMEM((s, qb), F32), pltpu.VMEM((s, qb), F32), pltpu.VMEM((s, qb), F32),
            pltpu.VMEM((s, qb), BF16), pltpu.VMEM((s, qb), BF16),
        ],
        compiler_params=pltpu.CompilerParams(
            dimension_semantics=("arbitrary", "arbitrary"), vmem_limit_bytes=VMEM_LIMIT),
        name="diff_attn",
    )(lq1, lk1, lq2, lk2, subln_g, qkv3, qkv3, qkv3)


def _s5_disc_kernel(lr_ref, li_ref, ldt_ref, bre_ref, bim_ref, ar_ref, ai_ref, bbr_ref, bbi_ref):
    lr = lr_ref[...]
    li = li_ref[...]
    dt = jnp.exp(ldt_ref[...])
    mag = jnp.exp(lr * dt)
    ang = li * dt
    ar = mag * jnp.cos(ang)
    ai = mag * jnp.sin(ang)
    nr = ar - 1.0
    den = lr * lr + li * li
    cr = (nr * lr + ai * li) / den
    ci = (ai * lr - nr * li) / den
    b_re = bre_ref[...]
    b_im = bim_ref[...]
    ar_ref[...] = ar
    ai_ref[...] = ai
    bbr_ref[...] = cr * b_re - ci * b_im
    bbi_ref[...] = cr * b_im + ci * b_re


def _s5_discretise(lam_re, lam_im, log_dt, b_re, b_im):
    nd, g, p, hc = b_re.shape
    flat = lambda t: t.astype(F32).reshape(nd * g, p * hc)
    rep = lambda t: flat(jnp.broadcast_to(t, b_re.shape))
    args = (rep(lam_re[..., None]), rep(lam_im[..., None]), rep(log_dt[..., None, None]), flat(b_re), flat(b_im))
    shp = jax.ShapeDtypeStruct((nd * g, p * hc), F32)
    ar, ai, bbr, bbi = pl.pallas_call(_s5_disc_kernel, out_shape=[shp] * 4, name="s5_disc")(*args)
    unflat = lambda t: t.reshape(nd, g, p, hc)
    return unflat(ar)[..., 0], unflat(ai)[..., 0], unflat(bbr), unflat(bbi)


def _s5_scan_kernel(uf_ref, ub_ref, a_ref, bre_ref, bim_ref, cre_ref, cim_ref, yf_ref, yb_ref,
                    xs_re, xs_im, st_ref, *, tt, nb, strip):
    n_tiles = uf_ref.shape[2] // LANES
    tile_states = GROUPS_PER_TILE * SSM_STATE
    n_state = n_tiles * tile_states

    @pl.when(pl.program_id(0) == 0)
    def _():
        st_ref[...] = jnp.zeros_like(st_ref)

    for d, (u_ref, y_ref) in enumerate(((uf_ref, yf_ref), (ub_ref, yb_ref))):
        u = u_ref[...].reshape(tt * nb, n_tiles * LANES).astype(BF16)
        for o in range(n_tiles):
            uo = u[:, o * LANES:(o + 1) * LANES]
            sl = slice(o * tile_states, (o + 1) * tile_states)
            xs_re[:, sl] = jnp.dot(uo, bre_ref[d, o], preferred_element_type=F32)
            xs_im[:, sl] = jnp.dot(uo, bim_ref[d, o], preferred_element_type=F32)

        for l0 in range(0, n_state, strip):
            ls = slice(l0, l0 + strip)
            ar = jnp.broadcast_to(a_ref[2 * d:2 * d + 1, ls], (nb, strip))
            ai = jnp.broadcast_to(a_ref[2 * d + 1:2 * d + 2, ls], (nb, strip))

            def body(j, carry, ar=ar, ai=ai, ls=ls, d=d):
                xr, xi = carry
                t = j if d == 0 else tt - 1 - j
                rows = pl.ds(pl.multiple_of(t * nb, nb), nb)
                nxr = ar * xr - ai * xi + xs_re[rows, ls]
                nxi = ar * xi + ai * xr + xs_im[rows, ls]
                xs_re[rows, ls] = nxr
                xs_im[rows, ls] = nxi
                return nxr, nxi

            xr, xi = lax.fori_loop(0, tt, body, (st_ref[2 * d, :, ls], st_ref[2 * d + 1, :, ls]), unroll=8)
            st_ref[2 * d, :, ls] = xr
            st_ref[2 * d + 1, :, ls] = xi

        for o in range(n_tiles):
            sl = slice(o * tile_states, (o + 1) * tile_states)
            yo = (jnp.dot(xs_re[:, sl].astype(BF16), cre_ref[d, o], preferred_element_type=F32)
                  - jnp.dot(xs_im[:, sl].astype(BF16), cim_ref[d, o], preferred_element_type=F32))
            y_ref[:, :, o * LANES:(o + 1) * LANES] = yo.reshape(tt, nb, LANES)


def _s5_scan(u_t, a_all, bre, bim, cre, cim, *, tt=64):
    s, nb, w = u_t.shape
    nt = s // tt
    n_state = a_all.shape[1]
    kern = functools.partial(_s5_scan_kernel, tt=tt, nb=nb, strip=512)
    full = lambda arr: pl.BlockSpec(arr.shape, lambda i: (0,) * arr.ndim)
    blk = (tt, nb, w)
    return pl.pallas_call(
        kern,
        grid=(nt,),
        in_specs=[
            pl.BlockSpec(blk, lambda i: (i, 0, 0)),
            pl.BlockSpec(blk, lambda i: (nt - 1 - i, 0, 0)),
            full(a_all), full(bre), full(bim), full(cre), full(cim),
        ],
        out_specs=[
            pl.BlockSpec(blk, lambda i: (i, 0, 0)),
            pl.BlockSpec(blk, lambda i: (nt - 1 - i, 0, 0)),
        ],
        out_shape=[jax.ShapeDtypeStruct(u_t.shape, F32)] * 2,
        scratch_shapes=[
            pltpu.VMEM((tt * nb, n_state), F32),
            pltpu.VMEM((tt * nb, n_state), F32),
            pltpu.VMEM((4, nb, n_state), F32),
        ],
        compiler_params=pltpu.CompilerParams(
            dimension_semantics=("arbitrary",), vmem_limit_bytes=VMEM_LIMIT),
        name="s5_scan",
    )(u_t, u_t, a_all, bre, bim, cre, cim)


def _block_diag_tiles(m):
    g, k, n = m.shape
    t = g // GROUPS_PER_TILE
    m = m.reshape(t, GROUPS_PER_TILE, k, n)
    eye = jnp.eye(GROUPS_PER_TILE, dtype=m.dtype)
    out = m[:, :, :, None, :] * eye[None, :, None, :, None]
    return out.reshape(t, GROUPS_PER_TILE * k, GROUPS_PER_TILE * n)


def _post_kernel(x_ref, att_ref, yf_ref, yb_ref, u_ref, dsk_ref, wg_ref, bg_ref, wo_ref, g2_ref,
                 w1_ref, w2_ref, gf_ref, o_ref, *, ff_chunk, final_norm):
    aw = att_ref.shape[1]
    u = u_ref[...]
    y = yf_ref[...] + yb_ref[...] + dsk_ref[...] * u
    z = 0.5 * y * (1.0 + lax.erf(y * (1.0 / math.sqrt(2.0))))
    gate = jax.nn.sigmoid(jnp.dot(z.astype(BF16), wg_ref[...], preferred_element_type=F32) + bg_ref[...])
    ssm = (z * gate).astype(BF16)
    h = (x_ref[...]
         + jnp.dot(att_ref[...], wo_ref[:aw, :], preferred_element_type=F32)
         + jnp.dot(ssm, wo_ref[aw:, :], preferred_element_type=F32))
    hn2 = _rms(h, g2_ref[...]).astype(BF16)
    mlp = jnp.zeros_like(h)
    for f0 in range(0, w1_ref.shape[1], ff_chunk):
        a = jnp.maximum(jnp.dot(hn2, w1_ref[:, f0:f0 + ff_chunk], preferred_element_type=F32), 0.0)
        mlp = mlp + jnp.dot((a * a).astype(BF16), w2_ref[f0:f0 + ff_chunk, :], preferred_element_type=F32)
    h = h + mlp
    o_ref[...] = _rms(h, gf_ref[...]) if final_norm else h


def _post(x2, att, yf, yb, u, d_skip, w_glu, b_glu, w_out, g2, w1, w2, gf, *, final_norm, tm=256):
    m, d = x2.shape
    aw = att.shape[1]
    sw = u.shape[1]
    row = lambda width: pl.BlockSpec((tm, width), lambda i: (i, 0))
    const = lambda arr: pl.BlockSpec(arr.shape, lambda i: (0, 0))
    kern = functools.partial(_post_kernel, ff_chunk=1024, final_norm=final_norm)
    return pl.pallas_call(
        kern,
        grid=(m // tm,),
        in_specs=[row(d), row(aw), row(sw), row(sw), row(sw),
                  const(d_skip), const(w_glu), const(b_glu), const(w_out), const(g2),
                  const(w1), const(w2), const(gf)],
        out_specs=row(d),
        out_shape=jax.ShapeDtypeStruct((m, d), F32),
        compiler_params=pltpu.CompilerParams(
            dimension_semantics=("arbitrary",), vmem_limit_bytes=VMEM_LIMIT),
        name="post_mlp",
    )(x2, att, yf, yb, u, d_skip, w_glu, b_glu, w_out, g2, w1, w2, gf)


def kernel(x, norm1_g, w_in, lam_q1, lam_k1, lam_q2, lam_k2, subln_g, ssm_lam_re, ssm_lam_im, ssm_log_dt,
           ssm_b_re, ssm_b_im, ssm_c_re, ssm_c_im, ssm_d, w_glu, b_glu, w_out, norm2_g, w_mlp1, w_mlp2,
           final_g):
    bsz, s, d = x.shape
    depth = w_in.shape[0]
    ssm_w = ssm_d.shape[-1]
    attn_w = w_out.shape[1] - ssm_w
    qk_dim = attn_w // ATTN_HEADS // 2
    scale = float(qk_dim) ** -0.5 * LOG2_E
    m = bsz * s

    h2 = x.reshape(m, d).astype(F32)
    for l in range(depth):
        lambda_init = 0.8 - 0.6 * math.exp(-0.3 * l)
        qkv, u = _inproj(h2, norm1_g[l][None].astype(F32), w_in[l].astype(BF16),
                         q_width=attn_w, qkv_width=3 * attn_w, scale=scale)

        row = lambda t: t[l][None].astype(F32)
        att = _attention(qkv.reshape(bsz, s, 3 * attn_w), row(lam_q1), row(lam_k1), row(lam_q2), row(lam_k2),
                         row(subln_g), lambda_init=lambda_init)

        ar, ai, bbr, bbi = _s5_discretise(ssm_lam_re[l], ssm_lam_im[l], ssm_log_dt[l], ssm_b_re[l], ssm_b_im[l])
        nd, g, p = ar.shape
        a_all = jnp.stack([ar[0], ai[0], ar[1], ai[1]]).reshape(4, g * p)
        to_in = lambda t: jnp.stack([_block_diag_tiles(jnp.swapaxes(t[dd], 1, 2)) for dd in range(nd)]).astype(BF16)
        to_out = lambda t: jnp.stack([_block_diag_tiles(jnp.swapaxes(t[dd].astype(F32), 1, 2))
                                      for dd in range(nd)]).astype(BF16)
        u_t = jnp.swapaxes(u.reshape(bsz, s, ssm_w), 0, 1)
        yf_t, yb_t = _s5_scan(u_t, a_all, to_in(bbr), to_in(bbi), to_out(ssm_c_re[l]), to_out(ssm_c_im[l]))
        yf = jnp.swapaxes(yf_t, 0, 1).reshape(m, ssm_w)
        yb = jnp.swapaxes(yb_t, 0, 1).reshape(m, ssm_w)

        h2 = _post(h2, att.reshape(m, attn_w), yf, yb, u, row(ssm_d), w_glu[l].astype(BF16), row(b_glu),
                   w_out[l].astype(BF16), row(norm2_g), w_mlp1[l].astype(BF16), w_mlp2[l].astype(BF16),
                   final_g[None].astype(F32), final_norm=(l == depth - 1))
    return h2.reshape(bsz, s, d).astype(x.dtype)
```

```python
import functools
import math

import jax
import jax.numpy as jnp
from jax import lax
from jax.experimental import pallas as pl
from jax.experimental.pallas import tpu as pltpu

F32 = jnp.float32
BF16 = jnp.bfloat16

EPS = 1e-6
ATTN_HEADS = 4
SSM_GROUP = 16
SSM_STATE = 64
LANES = 128
BF16_SUBLANES = 16
GROUPS_PER_TILE = LANES // SSM_GROUP
VMEM_LIMIT = 56 * 1024 * 1024


def _rms(x, g):
    return x * lax.rsqrt(jnp.mean(x * x, axis=-1, keepdims=True) + EPS) * g


def _inproj_kernel(x_ref, g_ref, w_ref, qkv_ref, u_ref, *, q_width, qkv_width, scale, chunk):
    hn = _rms(x_ref[...], g_ref[...]).astype(BF16)
    n_out = w_ref.shape[1]
    for c0 in range(0, n_out, chunk):
        p = jnp.dot(hn, w_ref[:, c0:c0 + chunk], preferred_element_type=F32)
        if c0 < q_width:
            p = p * scale
        if c0 < qkv_width:
            qkv_ref[:, c0:c0 + chunk] = p.astype(BF16)
        else:
            u_ref[:, c0 - qkv_width:c0 - qkv_width + chunk] = p


def _inproj(x2, g1, w_in, *, q_width, qkv_width, scale, tm=512):
    m, d = x2.shape
    n_out = w_in.shape[1]
    kern = functools.partial(_inproj_kernel, q_width=q_width, qkv_width=qkv_width, scale=scale, chunk=512)
    return pl.pallas_call(
        kern,
        grid=(m // tm,),
        in_specs=[
            pl.BlockSpec((tm, d), lambda i: (i, 0)),
            pl.BlockSpec((1, d), lambda i: (0, 0)),
            pl.BlockSpec((d, n_out), lambda i: (0, 0)),
        ],
        out_specs=[
            pl.BlockSpec((tm, qkv_width), lambda i: (i, 0)),
            pl.BlockSpec((tm, n_out - qkv_width), lambda i: (i, 0)),
        ],
        out_shape=[
            jax.ShapeDtypeStruct((m, qkv_width), BF16),
            jax.ShapeDtypeStruct((m, n_out - qkv_width), F32),
        ],
        compiler_params=pltpu.CompilerParams(
            dimension_semantics=("arbitrary",), vmem_limit_bytes=VMEM_LIMIT),
        name="inproj",
    )(x2, g1, w_in)


POS_LOW_BITS = 3
LOG2_E = 1.4426950408889634


def _attn_kernel(lq1_ref, lk1_ref, lq2_ref, lk2_ref, sg_ref, q_ref, k_ref, v_ref, o_ref,
                 kx_ref, vt_ref, s0_ref, s1_ref, s2_ref, s3_ref, e0_ref, e1_ref, *, qb, lambda_init):
    h = pl.program_id(1)
    s_refs, e_refs = (s0_ref, s1_ref, s2_ref, s3_ref), (e0_ref, e1_ref)
    s_len, hv = k_ref.shape[1], k_ref.shape[2]
    dh = hv // 2
    low_mask = (1 << POS_LOW_BITS) - 1

    lam = (jnp.exp(jnp.sum(lq1_ref[...] * lk1_ref[...], axis=-1, keepdims=True))
           - jnp.exp(jnp.sum(lq2_ref[...] * lk2_ref[...], axis=-1, keepdims=True))
           + lambda_init)

    slope = jnp.float32(2.0 ** -8.0)
    for hh in range(ATTN_HEADS - 1):
        slope = jnp.where(h == hh, jnp.float32(2.0 ** (-8.0 * (hh + 1) / ATTN_HEADS)), slope)

    c2 = jnp.full((1, 1), slope, F32) * LOG2_E
    c2_parts = []
    rest = c2
    for _ in range(3):
        part = rest.astype(BF16).astype(F32)
        c2_parts.append(part)
        rest = rest - part

    lane = lax.broadcasted_iota(jnp.int32, (1, hv), 1)

    def extras(vals):
        out = 0.0
        for e, val in enumerate(vals):
            out = jnp.where(lane == dh + e, val, out)
        return out

    def split_pos(pos):
        lo = pos & low_mask
        return (pos - lo).astype(F32), lo.astype(F32)

    data_lanes = (lane < dh, lane >= dh)
    to_half = lambda ex, c: ex if c == 0 else pltpu.roll(ex, dh, 1)

    j_hi, j_lo = split_pos(lax.broadcasted_iota(jnp.int32, (s_len, 1), 0))
    k_extra = extras(c2_parts + c2_parts + [j_hi] * 3 + [j_lo] * 3)
    kf = k_ref[0].astype(F32)
    for c in range(2):
        kx_ref[c] = jnp.where(data_lanes[c], kf, to_half(k_extra, c)).astype(BF16)
    vt_ref[:hv, :] = v_ref[0].astype(F32).T.astype(BF16)
    pad_row = lax.broadcasted_iota(jnp.int32, (vt_ref.shape[0] - hv, s_len), 0)
    vt_ref[hv:, :] = jnp.where(pad_row == 0, 1.0, 0.0).astype(BF16)

    off = lax.broadcasted_iota(jnp.int32, (qb, qb), 0) - lax.broadcasted_iota(jnp.int32, (qb, qb), 1)
    diag_bias = jnp.abs(off).astype(F32) * (-c2)

    q_side = {}

    def scores(n):
        qi, c = divmod(n, 2)
        lo, hi = qi * qb, (qi + 1) * qb
        if qi not in q_side:
            i_hi, i_lo = split_pos(lo + lax.broadcasted_iota(jnp.int32, (qb, 1), 0))
            q_side[qi] = (extras([-i_hi] * 3 + [-i_lo] * 3 + c2_parts + c2_parts),
                          q_ref[0, lo:hi, :].astype(F32))
        q_extra, qf = q_side[qi]
        aug_t = lambda ex: jnp.where(data_lanes[c], qf, to_half(ex, c)).T.astype(BF16)
        dot = lambda r0, r1, qt: jnp.dot(kx_ref[c, r0:r1, :], qt, preferred_element_type=F32)
        s_ref = s_refs[n % 4]
        if lo > 0:
            s_ref[:lo, :] = dot(0, lo, aug_t(q_extra))
        s_ref[lo:hi, :] = dot(lo, hi, aug_t(jnp.zeros_like(q_extra))) + diag_bias
        if hi < s_len:
            s_ref[hi:, :] = dot(hi, s_len, aug_t(-q_extra))

    def weighted_values(n):
        sc = s_refs[n % 4][...]
        e_refs[n % 2][...] = jnp.exp2(sc - jnp.max(sc, axis=0, keepdims=True)).astype(BF16)
        acc = jnp.dot(vt_ref[...], e_refs[n % 2][...], preferred_element_type=F32)
        return acc[:hv, :] / acc[hv:hv + 1, :]

    n_blocks = s_len // qb
    scores(0), scores(1)
    for qi in range(n_blocks):
        if qi + 1 < n_blocks:
            scores(2 * qi + 2), scores(2 * qi + 3)
        out_t = weighted_values(2 * qi), weighted_values(2 * qi + 1)
        o = (out_t[0] - lam * out_t[1]).T
        o = _rms(o, sg_ref[...]) * (1.0 - lambda_init)
        o_ref[0, qi * qb:(qi + 1) * qb, :] = o.astype(o_ref.dtype)


def _attention(qkv3, lq1, lk1, lq2, lk2, subln_g, *, lambda_init, qb=512):
    b, s, w = qkv3.shape
    hv = w // (3 * ATTN_HEADS)
    assert s <= 256 << POS_LOW_BITS and s % qb == 0
    kern = functools.partial(_attn_kernel, qb=qb, lambda_init=lambda_init)
    small = lambda shape: pl.BlockSpec(shape, lambda bi, hi: (0, 0))
    head = lambda off: pl.BlockSpec((1, s, hv), lambda bi, hi: (bi, 0, off + hi))
    return pl.pallas_call(
        kern,
        grid=(b, ATTN_HEADS),
        in_specs=[
            small(lq1.shape), small(lk1.shape), small(lq2.shape), small(lk2.shape), small(subln_g.shape),
            head(0), head(ATTN_HEADS), head(2 * ATTN_HEADS),
        ],
        out_specs=head(0),
        out_shape=jax.ShapeDtypeStruct((b, s, ATTN_HEADS * hv), BF16),
        scratch_shapes=[
            pltpu.VMEM((2, s, hv), BF16),
            pltpu.VMEM((hv + BF16_SUBLANES, s), BF16),
            pltpu.VMEM((s, qb), F32), pltpu.VMEM((s, qb), F32), pltpu.VMEM((s, qb), F32), pltpu.VMEM((s, qb), F32),
            pltpu.VMEM((s, qb), BF16), pltpu.VMEM((s, qb), BF16),
        ],
        compiler_params=pltpu.CompilerParams(
            dimension_semantics=("arbitrary", "arbitrary"), vmem_limit_bytes=VMEM_LIMIT),
        name="diff_attn",
    )(lq1, lk1, lq2, lk2, subln_g, qkv3, qkv3, qkv3)


S5_CHUNK = 8


def _s5_disc_kernel(lrb_ref, lib_ref, ldtb_ref, bre_ref, bim_ref, lrc_ref, lic_ref, ldtc_ref, cre_ref, cim_ref,
                    pbr_ref, pbi_ref, pcr_ref, pci_ref, akr_ref, aki_ref):
    def lam_bar(lr, li, ldt):
        dt = jnp.exp(ldt)
        mag = jnp.exp(lr * dt)
        ang = li * dt
        return mag * jnp.cos(ang), mag * jnp.sin(ang)

    cmul = lambda ar, ai, xr, xi: (ar * xr - ai * xi, ar * xi + ai * xr)

    lr, li = lrb_ref[...], lib_ref[...]
    ar, ai = lam_bar(lr, li, ldtb_ref[...])
    nr = ar - 1.0
    den = lr * lr + li * li
    cr = (nr * lr + ai * li) / den
    ci = (ai * lr - nr * li) / den
    b_re, b_im = bre_ref[...], bim_ref[...]
    pr, pi = cr * b_re - ci * b_im, cr * b_im + ci * b_re
    kr, ki = ar, ai
    for n in range(pbr_ref.shape[0]):
        pbr_ref[n] = pr
        pbi_ref[n] = pi
        pr, pi = cmul(ar, ai, pr, pi)
        if n > 0:
            kr, ki = cmul(ar, ai, kr, ki)
    akr_ref[...] = kr
    aki_ref[...] = ki

    arc, aic = lam_bar(lrc_ref[...], lic_ref[...], ldtc_ref[...])
    qr, qi = cre_ref[...], cim_ref[...]
    for n in range(pcr_ref.shape[0]):
        pcr_ref[n] = qr
        pci_ref[n] = -qi
        qr, qi = cmul(arc, aic, qr, qi)


def _s5_discretise(lam_re, lam_im, log_dt, b_re, b_im, c_re, c_im):
    nd, g, p, hc = b_re.shape
    rows, width = nd * g, p * hc
    bflat = lambda t: jnp.broadcast_to(t.astype(F32), (nd, g, p, hc)).reshape(rows, width)
    cflat = lambda t: jnp.broadcast_to(t.astype(F32), (nd, g, hc, p)).reshape(rows, width)
    args = (bflat(lam_re[..., None]), bflat(lam_im[..., None]), bflat(log_dt[..., None, None]),
            bflat(b_re), bflat(b_im),
            cflat(lam_re[:, :, None, :]), cflat(lam_im[:, :, None, :]), cflat(log_dt[..., None, None]),
            cflat(c_re), cflat(c_im))
    stack = lambda n: jax.ShapeDtypeStruct((n, rows, width), F32)
    one = jax.ShapeDtypeStruct((rows, width), F32)
    pbr, pbi, pcr, pci, akr, aki = pl.pallas_call(
        _s5_disc_kernel,
        out_shape=[stack(S5_CHUNK), stack(S5_CHUNK), stack(S5_CHUNK + 1), stack(S5_CHUNK + 1), one, one],
        name="s5_disc")(*args)
    pb = jnp.stack([pbr, pbi], axis=-1).reshape(S5_CHUNK, nd, g, p, hc, 2)
    pc = jnp.stack([pcr, pci], axis=-1).reshape(S5_CHUNK + 1, nd, g, hc, p, 2)
    ak = jnp.stack([akr, aki]).reshape(2, nd, g, p, hc)[..., 0]
    return pb, pc, ak


def _group_block_diag(t):
    *lead, g, k, n = t.shape
    nt = g // GROUPS_PER_TILE
    t = t.reshape(*lead, nt, GROUPS_PER_TILE, k, 1, n)
    eye = jnp.eye(GROUPS_PER_TILE, dtype=t.dtype).reshape(GROUPS_PER_TILE, 1, GROUPS_PER_TILE, 1)
    return (t * eye).reshape(*lead, nt, GROUPS_PER_TILE * k, GROUPS_PER_TILE * n)


def _s5_toeplitz_kernel(wall_ref, m0_ref, t_ref):
    nd, chunk, blk = wall_ref.shape[1], wall_ref.shape[2], wall_ref.shape[3]
    resp = [[jnp.dot(wall_ref[0, d, n], m0_ref[0, d], preferred_element_type=F32,
                     precision=lax.Precision.HIGHEST) for n in range(chunk)] for d in range(nd)]
    for j in range(chunk):
        for jp in range(chunk):
            tb = resp[0][jp - j] if jp > j else resp[1][j - jp] if jp < j else resp[0][0] + resp[1][0]
            t_ref[0, j * blk:(j + 1) * blk, jp * blk:(jp + 1) * blk] = tb.astype(t_ref.dtype)


def _s5_chunk_weights(pb, pc):
    chunk, nd = pb.shape[0], pb.shape[1]
    wall = jnp.concatenate([_group_block_diag(jnp.swapaxes(pb[..., ri], -1, -2)) for ri in range(2)], axis=-1)
    mall = jnp.concatenate([_group_block_diag(jnp.swapaxes(pc[..., ri], -1, -2)) for ri in range(2)], axis=-2)
    nt, blk, n_state = wall.shape[2], wall.shape[3], wall.shape[4]

    steps = jnp.arange(chunk)
    n_in = (chunk - 1 - steps, steps)
    n_out = (steps + 1, chunk - steps)
    win = jnp.stack([wall[n_in[d], d].transpose(1, 0, 2, 3).reshape(nt, chunk * blk, n_state) for d in range(nd)])
    mout = [mall[n_out[d], d].transpose(1, 2, 0, 3).reshape(nt, n_state, chunk * blk) for d in range(nd)]

    toep = pl.pallas_call(
        _s5_toeplitz_kernel,
        grid=(nt,),
        in_specs=[pl.BlockSpec((1, nd, chunk, blk, n_state), lambda o: (o, 0, 0, 0, 0)),
                  pl.BlockSpec((1, nd, n_state, blk), lambda o: (o, 0, 0, 0))],
        out_specs=pl.BlockSpec((1, chunk * blk, chunk * blk), lambda o: (o, 0, 0)),
        out_shape=jax.ShapeDtypeStruct((nt, chunk * blk, chunk * blk), BF16),
        compiler_params=pltpu.CompilerParams(dimension_semantics=("arbitrary",), vmem_limit_bytes=VMEM_LIMIT),
        name="s5_toeplitz",
    )(wall.transpose(2, 1, 0, 3, 4), mall[0].transpose(1, 0, 2, 3))
    wcat = jnp.concatenate([toep, mout[0].astype(BF16), mout[1].astype(BF16)], axis=1)
    return win.astype(BF16), wcat


def _s5_chunk_kernel(u_ref, win_ref, wcat_ref, ak_ref, y_ref, xf_ref, xb_ref, s0_ref, s1_ref, *, nb, rb):
    n_rows, half = u_ref.shape[1], ak_ref.shape[2]
    n_blk, pair = n_rows // rb, 2 * nb
    s_refs = (s0_ref, s1_ref)
    order = {0: list(range(n_blk)), 1: list(reversed(range(n_blk)))}

    def increments(d, i):
        blk = order[d][i]
        s_refs[i % 2][...] = jnp.dot(u_ref[0, blk * rb:(blk + 1) * rb, :], win_ref[d, 0], preferred_element_type=F32)

    def scan_block(d, i, x):
        blk, s_ref, x_ref = order[d][i], s_refs[i % 2], (xf_ref, xb_ref)[d]
        ar = jnp.broadcast_to(ak_ref[0, 2 * d:2 * d + 1, :], (nb, half))
        ai = jnp.broadcast_to(ak_ref[0, 2 * d + 1:2 * d + 2, :], (nb, half))

        def advance(x, rows):
            return (ar * x[0] - ai * x[1] + s_ref[rows, :half], ar * x[1] + ai * x[0] + s_ref[rows, half:])

        def body(it, x):
            r0 = pl.multiple_of((it if d == 0 else rb // pair - 1 - it) * pair, pair)
            early, late = pl.ds(r0, nb), pl.ds(r0 + nb, nb)
            x1 = advance(x, early if d == 0 else late)
            x2 = advance(x1, late if d == 0 else early)
            enter_early, enter_late = (x, x1) if d == 0 else (x1, x)
            dst = pl.ds(pl.multiple_of(blk * rb + r0, pair), pair)
            x_ref[dst, :half] = jnp.concatenate([enter_early[0], enter_late[0]], axis=0).astype(BF16)
            x_ref[dst, half:] = jnp.concatenate([enter_early[1], enter_late[1]], axis=0).astype(BF16)
            return x2

        return lax.fori_loop(0, rb // pair, body, x, unroll=True)

    zero = jnp.zeros((nb, half), F32)
    for d in range(2):
        x = (zero, zero)
        increments(d, 0)
        for i in range(n_blk):
            if i + 1 < n_blk:
                increments(d, i + 1)
            x = scan_block(d, i, x)

    for blk in range(n_blk):
        rows = slice(blk * rb, (blk + 1) * rb)
        lhs = jnp.concatenate([u_ref[0, rows, :], xf_ref[rows, :], xb_ref[rows, :]], axis=1)
        y_ref[0, rows, :] = jnp.dot(lhs, wcat_ref[0], preferred_element_type=F32).astype(y_ref.dtype)


def _s5_chunk_scan(u_tiles, win, wcat, ak, *, nb, rb=256):
    nt, n_rows, width = u_tiles.shape
    n_state = win.shape[3]
    kern = functools.partial(_s5_chunk_kernel, nb=nb, rb=rb)
    return pl.pallas_call(
        kern,
        grid=(nt,),
        in_specs=[
            pl.BlockSpec((1, n_rows, width), lambda o: (o, 0, 0)),
            pl.BlockSpec((win.shape[0], 1, width, n_state), lambda o: (0, o, 0, 0)),
            pl.BlockSpec((1,) + wcat.shape[1:], lambda o: (o, 0, 0)),
            pl.BlockSpec((1,) + ak.shape[1:], lambda o: (o, 0, 0)),
        ],
        out_specs=pl.BlockSpec((1, n_rows, width), lambda o: (o, 0, 0)),
        out_shape=jax.ShapeDtypeStruct(u_tiles.shape, BF16),
        scratch_shapes=[
            pltpu.VMEM((n_rows, n_state), BF16), pltpu.VMEM((n_rows, n_state), BF16),
            pltpu.VMEM((rb, n_state), F32), pltpu.VMEM((rb, n_state), F32),
        ],
        compiler_params=pltpu.CompilerParams(
            dimension_semantics=("arbitrary",), vmem_limit_bytes=VMEM_LIMIT),
        name="s5_scan",
    )(u_tiles, win, wcat, ak)


def _post_kernel(x_ref, att_ref, y_ref, u_ref, dsk_ref, wg_ref, bg_ref, wo_ref, g2_ref,
                 w1_ref, w2_ref, gf_ref, o_ref, *, ff_chunk, final_norm):
    aw = att_ref.shape[1]
    y = y_ref[...].astype(F32) + dsk_ref[...] * u_ref[...]
    z = 0.5 * y * (1.0 + lax.erf(y * (1.0 / math.sqrt(2.0))))
    gate = jax.nn.sigmoid(jnp.dot(z.astype(BF16), wg_ref[...], preferred_element_type=F32) + bg_ref[...])
    ssm = (z * gate).astype(BF16)
    h = (x_ref[...]
         + jnp.dot(att_ref[...], wo_ref[:aw, :], preferred_element_type=F32)
         + jnp.dot(ssm, wo_ref[aw:, :], preferred_element_type=F32))
    hn2 = _rms(h, g2_ref[...]).astype(BF16)
    mlp = jnp.zeros_like(h)
    for f0 in range(0, w1_ref.shape[1], ff_chunk):
        a = jnp.maximum(jnp.dot(hn2, w1_ref[:, f0:f0 + ff_chunk], preferred_element_type=F32), 0.0)
        mlp = mlp + jnp.dot((a * a).astype(BF16), w2_ref[f0:f0 + ff_chunk, :], preferred_element_type=F32)
    h = h + mlp
    o_ref[...] = _rms(h, gf_ref[...]) if final_norm else h


def _post(x2, att, y, u, d_skip, w_glu, b_glu, w_out, g2, w1, w2, gf, *, final_norm, tm=256):
    m, d = x2.shape
    aw = att.shape[1]
    sw = u.shape[1]
    row = lambda width: pl.BlockSpec((tm, width), lambda i: (i, 0))
    const = lambda arr: pl.BlockSpec(arr.shape, lambda i: (0, 0))
    kern = functools.partial(_post_kernel, ff_chunk=1024, final_norm=final_norm)
    return pl.pallas_call(
        kern,
        grid=(m // tm,),
        in_specs=[row(d), row(aw), row(sw), row(sw),
                  const(d_skip), const(w_glu), const(b_glu), const(w_out), const(g2),
                  const(w1), const(w2), const(gf)],
        out_specs=row(d),
        out_shape=jax.ShapeDtypeStruct((m, d), F32),
        compiler_params=pltpu.CompilerParams(
            dimension_semantics=("arbitrary",), vmem_limit_bytes=VMEM_LIMIT),
        name="post_mlp",
    )(x2, att, y, u, d_skip, w_glu, b_glu, w_out, g2, w1, w2, gf)


def kernel(x, norm1_g, w_in, lam_q1, lam_k1, lam_q2, lam_k2, subln_g, ssm_lam_re, ssm_lam_im, ssm_log_dt,
           ssm_b_re, ssm_b_im, ssm_c_re, ssm_c_im, ssm_d, w_glu, b_glu, w_out, norm2_g, w_mlp1, w_mlp2,
           final_g):
    bsz, s, d = x.shape
    depth = w_in.shape[0]
    ssm_w = ssm_d.shape[-1]
    attn_w = w_out.shape[1] - ssm_w
    qk_dim = attn_w // ATTN_HEADS // 2
    scale = float(qk_dim) ** -0.5 * LOG2_E
    m = bsz * s

    h2 = x.reshape(m, d).astype(F32)
    for l in range(depth):
        lambda_init = 0.8 - 0.6 * math.exp(-0.3 * l)
        qkv, u = _inproj(h2, norm1_g[l][None].astype(F32), w_in[l].astype(BF16),
                         q_width=attn_w, qkv_width=3 * attn_w, scale=scale)

        row = lambda t: t[l][None].astype(F32)
        att = _attention(qkv.reshape(bsz, s, 3 * attn_w), row(lam_q1), row(lam_k1), row(lam_q2), row(lam_k2),
                         row(subln_g), lambda_init=lambda_init)

        pb, pc, ak = _s5_discretise(ssm_lam_re[l], ssm_lam_im[l], ssm_log_dt[l], ssm_b_re[l], ssm_b_im[l],
                                    ssm_c_re[l], ssm_c_im[l])
        win, wcat = _s5_chunk_weights(pb, pc)
        nd, nt = win.shape[0], win.shape[1]
        n_chunks = s // S5_CHUNK
        ak_tiles = ak.reshape(2, nd, nt, -1).transpose(2, 1, 0, 3).reshape(nt, 2 * nd, -1)
        u_tiles = (u.astype(BF16).reshape(bsz, n_chunks, S5_CHUNK, nt, LANES).transpose(3, 1, 0, 2, 4)
                   .reshape(nt, n_chunks * bsz, S5_CHUNK * LANES))
        y_tiles = _s5_chunk_scan(u_tiles, win, wcat, ak_tiles, nb=bsz)
        y = (y_tiles.reshape(nt, n_chunks, bsz, S5_CHUNK, LANES).transpose(2, 1, 3, 0, 4).reshape(m, ssm_w))

        h2 = _post(h2, att.reshape(m, attn_w), y, u, row(ssm_d), w_glu[l].astype(BF16), row(b_glu),
                   w_out[l].astype(BF16), row(norm2_g), w_mlp1[l].astype(BF16), w_mlp2[l].astype(BF16),
                   final_g[None].astype(F32), final_norm=(l == depth - 1))
    return h2.reshape(bsz, s, d).astype(x.dtype)
```

```python
import functools
import math

import jax
import jax.numpy as jnp
from jax import lax
from jax.experimental import pallas as pl
from jax.experimental.pallas import tpu as pltpu

F32 = jnp.float32
BF16 = jnp.bfloat16

EPS = 1e-6
ATTN_HEADS = 4
SSM_GROUP = 16
SSM_STATE = 64
LANES = 128
BF16_SUBLANES = 16
GROUPS_PER_TILE = LANES // SSM_GROUP
VMEM_LIMIT = 56 * 1024 * 1024


def _rms(x, g):
    return x * lax.rsqrt(jnp.mean(x * x, axis=-1, keepdims=True) + EPS) * g


def _inproj_kernel(x_ref, g_ref, w_ref, qkv_ref, u_ref, *, q_width, qkv_width, scale, chunk):
    hn = _rms(x_ref[...], g_ref[...]).astype(BF16)
    n_out = w_ref.shape[1]
    for c0 in range(0, n_out, chunk):
        p = jnp.dot(hn, w_ref[:, c0:c0 + chunk], preferred_element_type=F32)
        if c0 < q_width:
            p = p * scale
        if c0 < qkv_width:
            qkv_ref[:, c0:c0 + chunk] = p.astype(BF16)
        else:
            u_ref[:, c0 - qkv_width:c0 - qkv_width + chunk] = p


def _inproj(x2, g1, w_in, *, q_width, qkv_width, scale, tm=512):
    m, d = x2.shape
    n_out = w_in.shape[1]
    kern = functools.partial(_inproj_kernel, q_width=q_width, qkv_width=qkv_width, scale=scale, chunk=512)
    return pl.pallas_call(
        kern,
        grid=(m // tm,),
        in_specs=[
            pl.BlockSpec((tm, d), lambda i: (i, 0)),
            pl.BlockSpec((1, d), lambda i: (0, 0)),
            pl.BlockSpec((d, n_out), lambda i: (0, 0)),
        ],
        out_specs=[
            pl.BlockSpec((tm, qkv_width), lambda i: (i, 0)),
            pl.BlockSpec((tm, n_out - qkv_width), lambda i: (i, 0)),
        ],
        out_shape=[
            jax.ShapeDtypeStruct((m, qkv_width), BF16),
            jax.ShapeDtypeStruct((m, n_out - qkv_width), F32),
        ],
        compiler_params=pltpu.CompilerParams(
            dimension_semantics=("arbitrary",), vmem_limit_bytes=VMEM_LIMIT),
        name="inproj",
    )(x2, g1, w_in)


POS_LOW_BITS = 3
LOG2_E = 1.4426950408889634


def _attn_kernel(lq1_ref, lk1_ref, lq2_ref, lk2_ref, sg_ref, q_ref, k_ref, v_ref, o_ref,
                 kx_ref, vt_ref, s0_ref, s1_ref, s2_ref, s3_ref, e0_ref, e1_ref, *, qb, lambda_init):
    h = pl.program_id(1)
    s_refs, e_refs = (s0_ref, s1_ref, s2_ref, s3_ref), (e0_ref, e1_ref)
    s_len, hv = k_ref.shape[1], k_ref.shape[2]
    dh = hv // 2
    low_mask = (1 << POS_LOW_BITS) - 1

    lam = (jnp.exp(jnp.sum(lq1_ref[...] * lk1_ref[...], axis=-1, keepdims=True))
           - jnp.exp(jnp.sum(lq2_ref[...] * lk2_ref[...], axis=-1, keepdims=True))
           + lambda_init)

    slope = jnp.float32(2.0 ** -8.0)
    for hh in range(ATTN_HEADS - 1):
        slope = jnp.where(h == hh, jnp.float32(2.0 ** (-8.0 * (hh + 1) / ATTN_HEADS)), slope)

    c2 = jnp.full((1, 1), slope, F32) * LOG2_E
    c2_parts = []
    rest = c2
    for _ in range(3):
        part = rest.astype(BF16).astype(F32)
        c2_parts.append(part)
        rest = rest - part

    lane = lax.broadcasted_iota(jnp.int32, (1, hv), 1)

    def extras(vals):
        out = 0.0
        for e, val in enumerate(vals):
            out = jnp.where(lane == dh + e, val, out)
        return out

    def split_pos(pos):
        lo = pos & low_mask
        return (pos - lo).astype(F32), lo.astype(F32)

    data_lanes = (lane < dh, lane >= dh)
    to_half = lambda ex, c: ex if c == 0 else pltpu.roll(ex, dh, 1)

    j_hi, j_lo = split_pos(lax.broadcasted_iota(jnp.int32, (s_len, 1), 0))
    k_extra = extras(c2_parts + c2_parts + [j_hi] * 3 + [j_lo] * 3)
    kf = k_ref[0].astype(F32)
    for c in range(2):
        kx_ref[c] = jnp.where(data_lanes[c], kf, to_half(k_extra, c)).astype(BF16)
    vt_ref[:hv, :] = v_ref[0].astype(F32).T.astype(BF16)
    pad_row = lax.broadcasted_iota(jnp.int32, (vt_ref.shape[0] - hv, s_len), 0)
    vt_ref[hv:, :] = jnp.where(pad_row == 0, 1.0, 0.0).astype(BF16)

    off = lax.broadcasted_iota(jnp.int32, (qb, qb), 0) - lax.broadcasted_iota(jnp.int32, (qb, qb), 1)
    diag_bias = jnp.abs(off).astype(F32) * (-c2)

    q_side = {}

    def scores(n):
        qi, c = divmod(n, 2)
        lo, hi = qi * qb, (qi + 1) * qb
        if qi not in q_side:
            i_hi, i_lo = split_pos(lo + lax.broadcasted_iota(jnp.int32, (qb, 1), 0))
            q_side[qi] = (extras([-i_hi] * 3 + [-i_lo] * 3 + c2_parts + c2_parts),
                          q_ref[0, lo:hi, :].astype(F32))
        q_extra, qf = q_side[qi]
        aug_t = lambda ex: jnp.where(data_lanes[c], qf, to_half(ex, c)).T.astype(BF16)
        dot = lambda r0, r1, qt: jnp.dot(kx_ref[c, r0:r1, :], qt, preferred_element_type=F32)
        s_ref = s_refs[n % 4]
        if lo > 0:
            s_ref[:lo, :] = dot(0, lo, aug_t(q_extra))
        s_ref[lo:hi, :] = dot(lo, hi, aug_t(jnp.zeros_like(q_extra))) + diag_bias
        if hi < s_len:
            s_ref[hi:, :] = dot(hi, s_len, aug_t(-q_extra))

    def weighted_values(n):
        sc = s_refs[n % 4][...]
        e_refs[n % 2][...] = jnp.exp2(sc - jnp.max(sc, axis=0, keepdims=True)).astype(BF16)
        acc = jnp.dot(vt_ref[...], e_refs[n % 2][...], preferred_element_type=F32)
        return acc[:hv, :] / acc[hv:hv + 1, :]

    n_blocks = s_len // qb
    scores(0), scores(1)
    for qi in range(n_blocks):
        if qi + 1 < n_blocks:
            scores(2 * qi + 2), scores(2 * qi + 3)
        out_t = weighted_values(2 * qi), weighted_values(2 * qi + 1)
        o = (out_t[0] - lam * out_t[1]).T
        o = _rms(o, sg_ref[...]) * (1.0 - lambda_init)
        o_ref[0, qi * qb:(qi + 1) * qb, :] = o.astype(o_ref.dtype)


def _attention(qkv3, lq1, lk1, lq2, lk2, subln_g, *, lambda_init, qb=512):
    b, s, w = qkv3.shape
    hv = w // (3 * ATTN_HEADS)
    assert s <= 256 << POS_LOW_BITS and s % qb == 0
    kern = functools.partial(_attn_kernel, qb=qb, lambda_init=lambda_init)
    small = lambda shape: pl.BlockSpec(shape, lambda bi, hi: (0, 0))
    head = lambda off: pl.BlockSpec((1, s, hv), lambda bi, hi: (bi, 0, off + hi))
    return pl.pallas_call(
        kern,
        grid=(b, ATTN_HEADS),
        in_specs=[
            small(lq1.shape), small(lk1.shape), small(lq2.shape), small(lk2.shape), small(subln_g.shape),
            head(0), head(ATTN_HEADS), head(2 * ATTN_HEADS),
        ],
        out_specs=head(0),
        out_shape=jax.ShapeDtypeStruct((b, s, ATTN_HEADS * hv), BF16),
        scratch_shapes=[
            pltpu.VMEM((2, s, hv), BF16),
            pltpu.VMEM((hv + BF16_SUBLANES, s), BF16),
            pltpu.VMEM((s, qb), F32), pltpu.VMEM((s, qb), F32), pltpu.VMEM((s, qb), F32), pltpu.VMEM((s, qb), F32),
            pltpu.VMEM((s, qb), BF16), pltpu.VMEM((s, qb), BF16),
        ],
        compiler_params=pltpu.CompilerParams(
            dimension_semantics=("arbitrary", "arbitrary"), vmem_limit_bytes=VMEM_LIMIT),
        name="diff_attn",
    )(lq1, lk1, lq2, lk2, subln_g, qkv3, qkv3, qkv3)


S5_CHUNK = 8


def _s5_disc_kernel(lr_ref, li_ref, ldt_ref, bre_ref, bim_ref, cre_ref, cim_ref,
                    pbr_ref, pbi_ref, pcr_ref, pci_ref, akr_ref, aki_ref):
    cmul = lambda ar, ai, xr, xi: (ar * xr - ai * xi, ar * xi + ai * xr)

    lr, li = lr_ref[...], li_ref[...]
    dt = jnp.exp(ldt_ref[...])
    mag = jnp.exp(lr * dt)
    ang = li * dt
    ar, ai = mag * jnp.cos(ang), mag * jnp.sin(ang)
    nr = ar - 1.0
    den = lr * lr + li * li
    cr = (nr * lr + ai * li) / den
    ci = (ai * lr - nr * li) / den
    b_re, b_im = bre_ref[...], bim_ref[...]
    pr, pi = cr * b_re - ci * b_im, cr * b_im + ci * b_re
    kr, ki = ar, ai
    for n in range(pbr_ref.shape[0]):
        pbr_ref[n] = pr
        pbi_ref[n] = pi
        pr, pi = cmul(ar, ai, pr, pi)
        if n > 0:
            kr, ki = cmul(ar, ai, kr, ki)
    akr_ref[...] = kr
    aki_ref[...] = ki

    qr, qi = cre_ref[...], cim_ref[...]
    for n in range(pcr_ref.shape[0]):
        pcr_ref[n] = qr
        pci_ref[n] = -qi
        qr, qi = cmul(ar, ai, qr, qi)


def _s5_discretise(lam_re, lam_im, log_dt, b_re, b_im, c_re, c_im):
    nd, g, p, hc = b_re.shape
    nt = g // GROUPS_PER_TILE
    rows, width = nd * nt * hc, GROUPS_PER_TILE * p
    arrange = lambda t: (jnp.broadcast_to(t.astype(F32), (nd, g, hc, p)).reshape(nd, nt, GROUPS_PER_TILE, hc, p)
                         .transpose(0, 1, 3, 2, 4).reshape(rows, width))
    args = (arrange(lam_re[:, :, None, :]), arrange(lam_im[:, :, None, :]), arrange(log_dt[..., None, None]),
            arrange(jnp.swapaxes(b_re, 2, 3)), arrange(jnp.swapaxes(b_im, 2, 3)), arrange(c_re), arrange(c_im))
    stack = lambda n: jax.ShapeDtypeStruct((n, rows, width), F32)
    one = jax.ShapeDtypeStruct((rows, width), F32)
    pbr, pbi, pcr, pci, akr, aki = pl.pallas_call(
        _s5_disc_kernel,
        out_shape=[stack(S5_CHUNK), stack(S5_CHUNK), stack(S5_CHUNK + 1), stack(S5_CHUNK + 1), one, one],
        name="s5_disc")(*args)
    tiles = lambda t: t.reshape(t.shape[0], nd, nt, hc, width)
    ak = jnp.stack([akr, aki]).reshape(2, nd, nt, hc, width)[:, :, :, 0]
    ak = ak.transpose(2, 1, 0, 3).reshape(nt, 2 * nd, width)
    return tiles(pbr), tiles(pbi), tiles(pcr), tiles(pci), ak


def _s5_weights_kernel(pbr_ref, pbi_ref, pcr_ref, pci_ref, win_ref, wcat_ref):
    chunk, nd, hc, half = pbr_ref.shape[0], pbr_ref.shape[1], pbr_ref.shape[3], pbr_ref.shape[4]
    blk = GROUPS_PER_TILE * hc
    p = half // GROUPS_PER_TILE
    same_group = (lax.broadcasted_iota(jnp.int32, (blk, half), 0) // hc
                  == lax.broadcasted_iota(jnp.int32, (blk, half), 1) // p)

    def expand(re, im):
        grid = lambda t: jnp.where(same_group, jnp.tile(t, (GROUPS_PER_TILE, 1)), 0.0)
        return jnp.concatenate([grid(re), grid(im)], axis=1)

    wall = [[expand(pbr_ref[n, d, 0], pbi_ref[n, d, 0]) for n in range(chunk)] for d in range(nd)]
    mall = [[expand(pcr_ref[n, d, 0], pci_ref[n, d, 0]).T for n in range(chunk + 1)] for d in range(nd)]

    n_state = 2 * half
    for d in range(nd):
        for j in range(chunk):
            n_in = chunk - 1 - j if d == 0 else j
            win_ref[d, 0, j * blk:(j + 1) * blk, :] = wall[d][n_in].astype(win_ref.dtype)
            n_out = j + 1 if d == 0 else chunk - j
            r0 = chunk * blk + d * n_state
            wcat_ref[0, r0:r0 + n_state, j * blk:(j + 1) * blk] = mall[d][n_out].astype(wcat_ref.dtype)

    resp = [[jnp.dot(wall[d][n], mall[d][0], preferred_element_type=F32, precision=lax.Precision.HIGHEST)
             for n in range(chunk)] for d in range(nd)]
    for j in range(chunk):
        for jp in range(chunk):
            tb = resp[0][jp - j] if jp > j else resp[1][j - jp] if jp < j else resp[0][0] + resp[1][0]
            wcat_ref[0, j * blk:(j + 1) * blk, jp * blk:(jp + 1) * blk] = tb.astype(wcat_ref.dtype)


def _s5_chunk_weights(pbr, pbi, pcr, pci):
    chunk, nd, nt, hc, half = pbr.shape
    width, n_state = chunk * GROUPS_PER_TILE * hc, 2 * half
    spec = lambda t: pl.BlockSpec((t.shape[0], nd, 1, hc, half), lambda o: (0, 0, o, 0, 0))
    return pl.pallas_call(
        _s5_weights_kernel,
        grid=(nt,),
        in_specs=[spec(pbr), spec(pbi), spec(pcr), spec(pci)],
        out_specs=[pl.BlockSpec((nd, 1, width, n_state), lambda o: (0, o, 0, 0)),
                   pl.BlockSpec((1, width + nd * n_state, width), lambda o: (o, 0, 0))],
        out_shape=[jax.ShapeDtypeStruct((nd, nt, width, n_state), BF16),
                   jax.ShapeDtypeStruct((nt, width + nd * n_state, width), BF16)],
        compiler_params=pltpu.CompilerParams(dimension_semantics=("arbitrary",), vmem_limit_bytes=VMEM_LIMIT),
        name="s5_weights",
    )(pbr, pbi, pcr, pci)


def _s5_chunk_kernel(u_ref, win_ref, wcat_ref, ak_ref, y_ref, xf_ref, xb_ref, s0_ref, s1_ref, *, nb, rb):
    n_rows, half = u_ref.shape[1], ak_ref.shape[2]
    n_blk, pair = n_rows // rb, 2 * nb
    s_refs = (s0_ref, s1_ref)
    order = {0: list(range(n_blk)), 1: list(reversed(range(n_blk)))}

    def increments(d, i):
        blk = order[d][i]
        s_refs[i % 2][...] = jnp.dot(u_ref[0, blk * rb:(blk + 1) * rb, :], win_ref[d, 0], preferred_element_type=F32)

    def scan_block(d, i, x):
        blk, s_ref, x_ref = order[d][i], s_refs[i % 2], (xf_ref, xb_ref)[d]
        ar = jnp.broadcast_to(ak_ref[0, 2 * d:2 * d + 1, :], (nb, half))
        ai = jnp.broadcast_to(ak_ref[0, 2 * d + 1:2 * d + 2, :], (nb, half))

        def advance(x, rows):
            return (ar * x[0] - ai * x[1] + s_ref[rows, :half], ar * x[1] + ai * x[0] + s_ref[rows, half:])

        def body(it, x):
            r0 = pl.multiple_of((it if d == 0 else rb // pair - 1 - it) * pair, pair)
            early, late = pl.ds(r0, nb), pl.ds(r0 + nb, nb)
            x1 = advance(x, early if d == 0 else late)
            x2 = advance(x1, late if d == 0 else early)
            enter_early, enter_late = (x, x1) if d == 0 else (x1, x)
            dst = pl.ds(pl.multiple_of(blk * rb + r0, pair), pair)
            x_ref[dst, :half] = jnp.concatenate([enter_early[0], enter_late[0]], axis=0).astype(BF16)
            x_ref[dst, half:] = jnp.concatenate([enter_early[1], enter_late[1]], axis=0).astype(BF16)
            return x2

        return lax.fori_loop(0, rb // pair, body, x, unroll=True)

    zero = jnp.zeros((nb, half), F32)
    for d in range(2):
        x = (zero, zero)
        increments(d, 0)
        for i in range(n_blk):
            if i + 1 < n_blk:
                increments(d, i + 1)
            x = scan_block(d, i, x)

    for blk in range(n_blk):
        rows = slice(blk * rb, (blk + 1) * rb)
        lhs = jnp.concatenate([u_ref[0, rows, :], xf_ref[rows, :], xb_ref[rows, :]], axis=1)
        y_ref[0, rows, :] = jnp.dot(lhs, wcat_ref[0], preferred_element_type=F32).astype(y_ref.dtype)


def _s5_chunk_scan(u_tiles, win, wcat, ak, *, nb, rb=256):
    nt, n_rows, width = u_tiles.shape
    n_state = win.shape[3]
    kern = functools.partial(_s5_chunk_kernel, nb=nb, rb=rb)
    return pl.pallas_call(
        kern,
        grid=(nt,),
        in_specs=[
            pl.BlockSpec((1, n_rows, width), lambda o: (o, 0, 0)),
            pl.BlockSpec((win.shape[0], 1, width, n_state), lambda o: (0, o, 0, 0)),
            pl.BlockSpec((1,) + wcat.shape[1:], lambda o: (o, 0, 0)),
            pl.BlockSpec((1,) + ak.shape[1:], lambda o: (o, 0, 0)),
        ],
        out_specs=pl.BlockSpec((1, n_rows, width), lambda o: (o, 0, 0)),
        out_shape=jax.ShapeDtypeStruct(u_tiles.shape, BF16),
        scratch_shapes=[
            pltpu.VMEM((n_rows, n_state), BF16), pltpu.VMEM((n_rows, n_state), BF16),
            pltpu.VMEM((rb, n_state), F32), pltpu.VMEM((rb, n_state), F32),
        ],
        compiler_params=pltpu.CompilerParams(
            dimension_semantics=("arbitrary",), vmem_limit_bytes=VMEM_LIMIT),
        name="s5_scan",
    )(u_tiles, win, wcat, ak)


def _post_kernel(x_ref, att_ref, y_ref, u_ref, dsk_ref, wg_ref, bg_ref, wo_ref, g2_ref,
                 w1_ref, w2_ref, gf_ref, o_ref, *, ff_chunk, final_norm):
    aw = att_ref.shape[1]
    y = y_ref[...].astype(F32) + dsk_ref[...] * u_ref[...]
    z = 0.5 * y * (1.0 + lax.erf(y * (1.0 / math.sqrt(2.0))))
    gate = jax.nn.sigmoid(jnp.dot(z.astype(BF16), wg_ref[...], preferred_element_type=F32) + bg_ref[...])
    ssm = (z * gate).astype(BF16)
    h = (x_ref[...]
         + jnp.dot(att_ref[...], wo_ref[:aw, :], preferred_element_type=F32)
         + jnp.dot(ssm, wo_ref[aw:, :], preferred_element_type=F32))
    hn2 = _rms(h, g2_ref[...]).astype(BF16)
    mlp = jnp.zeros_like(h)
    for f0 in range(0, w1_ref.shape[1], ff_chunk):
        a = jnp.maximum(jnp.dot(hn2, w1_ref[:, f0:f0 + ff_chunk], preferred_element_type=F32), 0.0)
        mlp = mlp + jnp.dot((a * a).astype(BF16), w2_ref[f0:f0 + ff_chunk, :], preferred_element_type=F32)
    h = h + mlp
    o_ref[...] = _rms(h, gf_ref[...]) if final_norm else h


def _post(x2, att, y, u, d_skip, w_glu, b_glu, w_out, g2, w1, w2, gf, *, final_norm, tm=256):
    m, d = x2.shape
    aw = att.shape[1]
    sw = u.shape[1]
    row = lambda width: pl.BlockSpec((tm, width), lambda i: (i, 0))
    const = lambda arr: pl.BlockSpec(arr.shape, lambda i: (0, 0))
    kern = functools.partial(_post_kernel, ff_chunk=1024, final_norm=final_norm)
    return pl.pallas_call(
        kern,
        grid=(m // tm,),
        in_specs=[row(d), row(aw), row(sw), row(sw),
                  const(d_skip), const(w_glu), const(b_glu), const(w_out), const(g2),
                  const(w1), const(w2), const(gf)],
        out_specs=row(d),
        out_shape=jax.ShapeDtypeStruct((m, d), F32),
        compiler_params=pltpu.CompilerParams(
            dimension_semantics=("arbitrary",), vmem_limit_bytes=VMEM_LIMIT),
        name="post_mlp",
    )(x2, att, y, u, d_skip, w_glu, b_glu, w_out, g2, w1, w2, gf)


def kernel(x, norm1_g, w_in, lam_q1, lam_k1, lam_q2, lam_k2, subln_g, ssm_lam_re, ssm_lam_im, ssm_log_dt,
           ssm_b_re, ssm_b_im, ssm_c_re, ssm_c_im, ssm_d, w_glu, b_glu, w_out, norm2_g, w_mlp1, w_mlp2,
           final_g):
    bsz, s, d = x.shape
    depth = w_in.shape[0]
    ssm_w = ssm_d.shape[-1]
    attn_w = w_out.shape[1] - ssm_w
    qk_dim = attn_w // ATTN_HEADS // 2
    scale = float(qk_dim) ** -0.5 * LOG2_E
    m = bsz * s

    h2 = x.reshape(m, d).astype(F32)
    for l in range(depth):
        lambda_init = 0.8 - 0.6 * math.exp(-0.3 * l)
        qkv, u = _inproj(h2, norm1_g[l][None].astype(F32), w_in[l].astype(BF16),
                         q_width=attn_w, qkv_width=3 * attn_w, scale=scale)

        row = lambda t: t[l][None].astype(F32)
        att = _attention(qkv.reshape(bsz, s, 3 * attn_w), row(lam_q1), row(lam_k1), row(lam_q2), row(lam_k2),
                         row(subln_g), lambda_init=lambda_init)

        pbr, pbi, pcr, pci, ak_tiles = _s5_discretise(ssm_lam_re[l], ssm_lam_im[l], ssm_log_dt[l],
                                                      ssm_b_re[l], ssm_b_im[l], ssm_c_re[l], ssm_c_im[l])
        win, wcat = _s5_chunk_weights(pbr, pbi, pcr, pci)
        nt = win.shape[1]
        n_chunks = s // S5_CHUNK
        u_tiles = (u.astype(BF16).reshape(bsz, n_chunks, S5_CHUNK, nt, LANES).transpose(3, 1, 0, 2, 4)
                   .reshape(nt, n_chunks * bsz, S5_CHUNK * LANES))
        y_tiles = _s5_chunk_scan(u_tiles, win, wcat, ak_tiles, nb=bsz)
        y = (y_tiles.reshape(nt, n_chunks, bsz, S5_CHUNK, LANES).transpose(2, 1, 3, 0, 4).reshape(m, ssm_w))

        h2 = _post(h2, att.reshape(m, attn_w), y, u, row(ssm_d), w_glu[l].astype(BF16), row(b_glu),
                   w_out[l].astype(BF16), row(norm2_g), w_mlp1[l].astype(BF16), w_mlp2[l].astype(BF16),
                   final_g[None].astype(F32), final_norm=(l == depth - 1))
    return h2.reshape(bsz, s, d).astype(x.dtype)
```

```python
import functools
import math

import jax
import jax.numpy as jnp
from jax import lax
from jax.experimental import pallas as pl
from jax.experimental.pallas import tpu as pltpu

F32 = jnp.float32
BF16 = jnp.bfloat16

EPS = 1e-6
ATTN_HEADS = 4
SSM_GROUP = 16
SSM_STATE = 64
LANES = 128
BF16_SUBLANES = 16
GROUPS_PER_TILE = LANES // SSM_GROUP
VMEM_LIMIT = 56 * 1024 * 1024


def _rms(x, g):
    return x * lax.rsqrt(jnp.mean(x * x, axis=-1, keepdims=True) + EPS) * g


def _inproj_kernel(x_ref, g_ref, w_ref, qkv_ref, u_ref, *, q_width, qkv_width, scale, chunk):
    hn = _rms(x_ref[...], g_ref[...]).astype(BF16)
    n_out = w_ref.shape[1]
    for c0 in range(0, n_out, chunk):
        p = jnp.dot(hn, w_ref[:, c0:c0 + chunk], preferred_element_type=F32)
        if c0 < q_width:
            p = p * scale
        if c0 < qkv_width:
            qkv_ref[:, c0:c0 + chunk] = p.astype(BF16)
        else:
            u_ref[:, c0 - qkv_width:c0 - qkv_width + chunk] = p


def _inproj(x2, g1, w_in, *, q_width, qkv_width, scale, tm=512):
    m, d = x2.shape
    n_out = w_in.shape[1]
    kern = functools.partial(_inproj_kernel, q_width=q_width, qkv_width=qkv_width, scale=scale, chunk=512)
    return pl.pallas_call(
        kern,
        grid=(m // tm,),
        in_specs=[
            pl.BlockSpec((tm, d), lambda i: (i, 0)),
            pl.BlockSpec((1, d), lambda i: (0, 0)),
            pl.BlockSpec((d, n_out), lambda i: (0, 0)),
        ],
        out_specs=[
            pl.BlockSpec((tm, qkv_width), lambda i: (i, 0)),
            pl.BlockSpec((tm, n_out - qkv_width), lambda i: (i, 0)),
        ],
        out_shape=[
            jax.ShapeDtypeStruct((m, qkv_width), BF16),
            jax.ShapeDtypeStruct((m, n_out - qkv_width), F32),
        ],
        compiler_params=pltpu.CompilerParams(
            dimension_semantics=("arbitrary",), vmem_limit_bytes=VMEM_LIMIT),
        name="inproj",
    )(x2, g1, w_in)


POS_LOW_BITS = 3
LOG2_E = 1.4426950408889634
SAFE_LOG2_SCORE = 64.0


def _attn_kernel(lq1_ref, lk1_ref, lq2_ref, lk2_ref, sg_ref, q_ref, k_ref, v_ref, o_ref,
                 kex_ref, qex_ref, diag_ref, kx_ref, vt_ref, s0_ref, s1_ref, s2_ref, s3_ref, e0_ref, e1_ref,
                 *, qb, lambda_init):
    h = pl.program_id(0)
    s_refs, e_refs = (s0_ref, s1_ref, s2_ref, s3_ref), (e0_ref, e1_ref)
    s_len, hv = k_ref.shape[1], k_ref.shape[2]
    dh = hv // 2

    lam = (jnp.exp(jnp.sum(lq1_ref[...] * lk1_ref[...], axis=-1, keepdims=True))
           - jnp.exp(jnp.sum(lq2_ref[...] * lk2_ref[...], axis=-1, keepdims=True))
           + lambda_init)

    lane = lax.broadcasted_iota(jnp.int32, (1, hv), 1)
    data_lanes = (lane < dh, lane >= dh)

    @pl.when(pl.program_id(1) == 0)
    def _():
        slope = jnp.float32(2.0 ** -8.0)
        for hh in range(ATTN_HEADS - 1):
            slope = jnp.where(h == hh, jnp.float32(2.0 ** (-8.0 * (hh + 1) / ATTN_HEADS)), slope)
        c2 = jnp.full((1, 1), slope, F32) * LOG2_E
        c2_parts = []
        rest = c2
        for _ in range(3):
            part = rest.astype(BF16).astype(F32)
            c2_parts.append(part)
            rest = rest - part

        def extras(vals):
            out = 0.0
            for e, val in enumerate(vals):
                out = jnp.where(lane == dh + e, val, out)
            return out

        pos = lax.broadcasted_iota(jnp.int32, (s_len, 1), 0)
        pos_lo = pos & ((1 << POS_LOW_BITS) - 1)
        p_hi, p_lo = (pos - pos_lo).astype(F32), pos_lo.astype(F32)
        kex = extras(c2_parts + c2_parts + [p_hi] * 3 + [p_lo] * 3)
        qex = extras([-p_hi] * 3 + [-p_lo] * 3 + c2_parts + c2_parts)
        kex_ref[0], kex_ref[1] = kex, pltpu.roll(kex, dh, 1)
        qex_ref[0], qex_ref[1] = qex, pltpu.roll(qex, dh, 1)
        off = lax.broadcasted_iota(jnp.int32, (qb, qb), 0) - lax.broadcasted_iota(jnp.int32, (qb, qb), 1)
        diag_ref[...] = jnp.abs(off).astype(F32) * (-c2)

    kf = k_ref[0].astype(F32)
    for c in range(2):
        kx_ref[c] = jnp.where(data_lanes[c], kf, kex_ref[c]).astype(BF16)
    vt_ref[:hv, :] = v_ref[0].astype(F32).T.astype(BF16)
    pad_row = lax.broadcasted_iota(jnp.int32, (vt_ref.shape[0] - hv, s_len), 0)
    vt_ref[hv:, :] = jnp.where(pad_row == 0, 1.0, 0.0).astype(BF16)

    norm2 = lambda t: jnp.max(jnp.sum(t * t, axis=1, keepdims=True), axis=0, keepdims=True)
    small_scores = (norm2(q_ref[0].astype(F32)) * norm2(kf))[0, 0] <= SAFE_LOG2_SCORE ** 2

    n_blocks = s_len // qb

    def score_pieces(n, q_side):
        qi, c = divmod(n, 2)
        lo, hi = qi * qb, (qi + 1) * qb
        if qi not in q_side:
            q_side[qi] = q_ref[0, lo:hi, :].astype(F32)
        qf = q_side[qi]
        aug_t = lambda ex: jnp.where(data_lanes[c], qf, ex).T.astype(BF16)
        dot = lambda r0, r1, qt: jnp.dot(kx_ref[c, r0:r1, :], qt, preferred_element_type=F32)
        q_extra = qex_ref[c, lo:hi, :]
        q_left, q_right = aug_t(q_extra), aug_t(-q_extra)
        for r0 in range(0, s_len, qb):
            if r0 == lo:
                yield lo, hi, dot(lo, hi, aug_t(jnp.zeros_like(q_extra))) + diag_ref[...]
            else:
                yield r0, r0 + qb, dot(r0, r0 + qb, q_left if r0 < lo else q_right)

    def weighted_values(n):
        acc = jnp.dot(vt_ref[...], e_refs[n % 2][...], preferred_element_type=F32)
        return acc[:hv, :] / acc[hv:hv + 1, :]

    def finish_block(qi, out_t):
        o = (out_t[0] - lam * out_t[1]).T
        o = _rms(o, sg_ref[...]) * (1.0 - lambda_init)
        o_ref[0, qi * qb:(qi + 1) * qb, :] = o.astype(o_ref.dtype)

    @pl.when(small_scores)
    def _():
        q_side = {}
        for qi in range(n_blocks):
            out_t = []
            for n in (2 * qi, 2 * qi + 1):
                for r0, r1, sc in score_pieces(n, q_side):
                    e_refs[n % 2][r0:r1, :] = jnp.exp2(sc).astype(BF16)
                out_t.append(weighted_values(n))
            finish_block(qi, out_t)

    @pl.when(jnp.logical_not(small_scores))
    def _():
        q_side = {}

        def scores(n):
            for r0, r1, sc in score_pieces(n, q_side):
                s_refs[n % 4][r0:r1, :] = sc

        def exponentials(n):
            sc = s_refs[n % 4][...]
            e_refs[n % 2][...] = jnp.exp2(sc - jnp.max(sc, axis=0, keepdims=True)).astype(BF16)

        scores(0), scores(1)
        for qi in range(n_blocks):
            if qi + 1 < n_blocks:
                scores(2 * qi + 2), scores(2 * qi + 3)
            out_t = []
            for n in (2 * qi, 2 * qi + 1):
                exponentials(n)
                out_t.append(weighted_values(n))
            finish_block(qi, out_t)


def _attention(qkv3, lq1, lk1, lq2, lk2, subln_g, *, lambda_init, qb=512):
    b, s, w = qkv3.shape
    hv = w // (3 * ATTN_HEADS)
    assert s <= 256 << POS_LOW_BITS and s % qb == 0
    kern = functools.partial(_attn_kernel, qb=qb, lambda_init=lambda_init)
    small = lambda shape: pl.BlockSpec(shape, lambda hi, bi: (0, 0))
    head = lambda off: pl.BlockSpec((1, s, hv), lambda hi, bi: (bi, 0, off + hi))
    return pl.pallas_call(
        kern,
        grid=(ATTN_HEADS, b),
        in_specs=[
            small(lq1.shape), small(lk1.shape), small(lq2.shape), small(lk2.shape), small(subln_g.shape),
            head(0), head(ATTN_HEADS), head(2 * ATTN_HEADS),
        ],
        out_specs=head(0),
        out_shape=jax.ShapeDtypeStruct((b, s, ATTN_HEADS * hv), BF16),
        scratch_shapes=[
            pltpu.VMEM((2, s, hv), F32), pltpu.VMEM((2, s, hv), F32), pltpu.VMEM((qb, qb), F32),
            pltpu.VMEM((2, s, hv), BF16),
            pltpu.VMEM((hv + BF16_SUBLANES, s), BF16),
            pltpu.VMEM((s, qb), F32), pltpu.VMEM((s, qb), F32), pltpu.VMEM((s, qb), F32), pltpu.VMEM((s, qb), F32),
            pltpu.VMEM((s, qb), BF16), pltpu.VMEM((s, qb), BF16),
        ],
        compiler_params=pltpu.CompilerParams(
            dimension_semantics=("arbitrary", "arbitrary"), vmem_limit_bytes=VMEM_LIMIT),
        name="diff_attn",
    )(lq1, lk1, lq2, lk2, subln_g, qkv3, qkv3, qkv3)


S5_CHUNK = 8


def _s5_disc_kernel(lr_ref, li_ref, ldt_ref, bre_ref, bim_ref, cre_ref, cim_ref,
                    pbr_ref, pbi_ref, pcr_ref, pci_ref, akr_ref, aki_ref):
    cmul = lambda ar, ai, xr, xi: (ar * xr - ai * xi, ar * xi + ai * xr)

    lr, li = lr_ref[...], li_ref[...]
    dt = jnp.exp(ldt_ref[...])
    mag = jnp.exp(lr * dt)
    ang = li * dt
    ar, ai = mag * jnp.cos(ang), mag * jnp.sin(ang)
    nr = ar - 1.0
    den = lr * lr + li * li
    cr = (nr * lr + ai * li) / den
    ci = (ai * lr - nr * li) / den
    b_re, b_im = bre_ref[...], bim_ref[...]
    pr, pi = cr * b_re - ci * b_im, cr * b_im + ci * b_re
    kr, ki = ar, ai
    for n in range(pbr_ref.shape[0]):
        pbr_ref[n] = pr
        pbi_ref[n] = pi
        pr, pi = cmul(ar, ai, pr, pi)
        if n > 0:
            kr, ki = cmul(ar, ai, kr, ki)
    akr_ref[...] = kr
    aki_ref[...] = ki

    qr, qi = cre_ref[...], cim_ref[...]
    for n in range(pcr_ref.shape[0]):
        pcr_ref[n] = qr
        pci_ref[n] = -qi
        qr, qi = cmul(ar, ai, qr, qi)


def _s5_discretise(lam_re, lam_im, log_dt, b_re, b_im, c_re, c_im):
    nd, g, p, hc = b_re.shape
    nt = g // GROUPS_PER_TILE
    rows, width = nd * nt * hc, GROUPS_PER_TILE * p
    arrange = lambda t: (jnp.broadcast_to(t.astype(F32), (nd, g, hc, p)).reshape(nd, nt, GROUPS_PER_TILE, hc, p)
                         .transpose(0, 1, 3, 2, 4).reshape(rows, width))
    args = (arrange(lam_re[:, :, None, :]), arrange(lam_im[:, :, None, :]), arrange(log_dt[..., None, None]),
            arrange(jnp.swapaxes(b_re, 2, 3)), arrange(jnp.swapaxes(b_im, 2, 3)), arrange(c_re), arrange(c_im))
    stack = lambda n: jax.ShapeDtypeStruct((n, rows, width), F32)
    one = jax.ShapeDtypeStruct((rows, width), F32)
    pbr, pbi, pcr, pci, akr, aki = pl.pallas_call(
        _s5_disc_kernel,
        out_shape=[stack(S5_CHUNK), stack(S5_CHUNK), stack(S5_CHUNK + 1), stack(S5_CHUNK + 1), one, one],
        name="s5_disc")(*args)
    tiles = lambda t: t.reshape(t.shape[0], nd, nt, hc, width)
    ak = jnp.stack([akr, aki]).reshape(2, nd, nt, hc, width)[:, :, :, 0]
    ak = ak.transpose(2, 1, 0, 3).reshape(nt, 2 * nd, width)
    return tiles(pbr), tiles(pbi), tiles(pcr), tiles(pci), ak


def _s5_weights_kernel(pbr_ref, pbi_ref, pcr_ref, pci_ref, win_ref, wcat_ref):
    chunk, nd, hc, half = pbr_ref.shape[0], pbr_ref.shape[1], pbr_ref.shape[3], pbr_ref.shape[4]
    blk = GROUPS_PER_TILE * hc
    p = half // GROUPS_PER_TILE
    same_group = (lax.broadcasted_iota(jnp.int32, (blk, half), 0) // hc
                  == lax.broadcasted_iota(jnp.int32, (blk, half), 1) // p)

    def expand(re, im):
        grid = lambda t: jnp.where(same_group, jnp.tile(t, (GROUPS_PER_TILE, 1)), 0.0)
        return jnp.concatenate([grid(re), grid(im)], axis=1)

    wall = [[expand(pbr_ref[n, d, 0], pbi_ref[n, d, 0]) for n in range(chunk)] for d in range(nd)]
    mall = [[expand(pcr_ref[n, d, 0], pci_ref[n, d, 0]).T for n in range(chunk + 1)] for d in range(nd)]

    n_state = 2 * half
    for d in range(nd):
        for j in range(chunk):
            n_in = chunk - 1 - j if d == 0 else j
            win_ref[d, 0, j * blk:(j + 1) * blk, :] = wall[d][n_in].astype(win_ref.dtype)
            n_out = j + 1 if d == 0 else chunk - j
            r0 = chunk * blk + d * n_state
            wcat_ref[0, r0:r0 + n_state, j * blk:(j + 1) * blk] = mall[d][n_out].astype(wcat_ref.dtype)

    resp = [[jnp.dot(wall[d][n], mall[d][0], preferred_element_type=F32, precision=lax.Precision.HIGHEST)
             for n in range(chunk)] for d in range(nd)]
    for j in range(chunk):
        for jp in range(chunk):
            tb = resp[0][jp - j] if jp > j else resp[1][j - jp] if jp < j else resp[0][0] + resp[1][0]
            wcat_ref[0, j * blk:(j + 1) * blk, jp * blk:(jp + 1) * blk] = tb.astype(wcat_ref.dtype)


def _s5_chunk_weights(pbr, pbi, pcr, pci):
    chunk, nd, nt, hc, half = pbr.shape
    width, n_state = chunk * GROUPS_PER_TILE * hc, 2 * half
    spec = lambda t: pl.BlockSpec((t.shape[0], nd, 1, hc, half), lambda o: (0, 0, o, 0, 0))
    return pl.pallas_call(
        _s5_weights_kernel,
        grid=(nt,),
        in_specs=[spec(pbr), spec(pbi), spec(pcr), spec(pci)],
        out_specs=[pl.BlockSpec((nd, 1, width, n_state), lambda o: (0, o, 0, 0)),
                   pl.BlockSpec((1, width + nd * n_state, width), lambda o: (o, 0, 0))],
        out_shape=[jax.ShapeDtypeStruct((nd, nt, width, n_state), BF16),
                   jax.ShapeDtypeStruct((nt, width + nd * n_state, width), BF16)],
        compiler_params=pltpu.CompilerParams(dimension_semantics=("arbitrary",), vmem_limit_bytes=VMEM_LIMIT),
        name="s5_weights",
    )(pbr, pbi, pcr, pci)


def _s5_chunk_kernel(u_ref, win_ref, wcat_ref, ak_ref, y_ref, xf_ref, xb_ref, s0_ref, s1_ref, *, nb, rb):
    n_rows, half = u_ref.shape[1], ak_ref.shape[2]
    n_blk, pair = n_rows // rb, 2 * nb
    s_refs = (s0_ref, s1_ref)
    order = {0: list(range(n_blk)), 1: list(reversed(range(n_blk)))}

    def increments(d, i):
        blk = order[d][i]
        s_refs[i % 2][...] = jnp.dot(u_ref[0, blk * rb:(blk + 1) * rb, :], win_ref[d, 0], preferred_element_type=F32)

    def scan_block(d, i, x):
        blk, s_ref, x_ref = order[d][i], s_refs[i % 2], (xf_ref, xb_ref)[d]
        ar = jnp.broadcast_to(ak_ref[0, 2 * d:2 * d + 1, :], (nb, half))
        ai = jnp.broadcast_to(ak_ref[0, 2 * d + 1:2 * d + 2, :], (nb, half))

        def advance(x, rows):
            return (ar * x[0] - ai * x[1] + s_ref[rows, :half], ar * x[1] + ai * x[0] + s_ref[rows, half:])

        def body(it, x):
            r0 = pl.multiple_of((it if d == 0 else rb // pair - 1 - it) * pair, pair)
            early, late = pl.ds(r0, nb), pl.ds(r0 + nb, nb)
            x1 = advance(x, early if d == 0 else late)
            x2 = advance(x1, late if d == 0 else early)
            enter_early, enter_late = (x, x1) if d == 0 else (x1, x)
            dst = pl.ds(pl.multiple_of(blk * rb + r0, pair), pair)
            x_ref[dst, :half] = jnp.concatenate([enter_early[0], enter_late[0]], axis=0).astype(BF16)
            x_ref[dst, half:] = jnp.concatenate([enter_early[1], enter_late[1]], axis=0).astype(BF16)
            return x2

        return lax.fori_loop(0, rb // pair, body, x, unroll=True)

    zero = jnp.zeros((nb, half), F32)
    for d in range(2):
        x = (zero, zero)
        increments(d, 0)
        for i in range(n_blk):
            if i + 1 < n_blk:
                increments(d, i + 1)
            x = scan_block(d, i, x)

    for blk in range(n_blk):
        rows = slice(blk * rb, (blk + 1) * rb)
        lhs = jnp.concatenate([u_ref[0, rows, :], xf_ref[rows, :], xb_ref[rows, :]], axis=1)
        y_ref[0, rows, :] = jnp.dot(lhs, wcat_ref[0], preferred_element_type=F32).astype(y_ref.dtype)


def _s5_chunk_scan(u_tiles, win, wcat, ak, *, nb, rb=256):
    nt, n_rows, width = u_tiles.shape
    n_state = win.shape[3]
    kern = functools.partial(_s5_chunk_kernel, nb=nb, rb=rb)
    return pl.pallas_call(
        kern,
        grid=(nt,),
        in_specs=[
            pl.BlockSpec((1, n_rows, width), lambda o: (o, 0, 0)),
            pl.BlockSpec((win.shape[0], 1, width, n_state), lambda o: (0, o, 0, 0)),
            pl.BlockSpec((1,) + wcat.shape[1:], lambda o: (o, 0, 0)),
            pl.BlockSpec((1,) + ak.shape[1:], lambda o: (o, 0, 0)),
        ],
        out_specs=pl.BlockSpec((1, n_rows, width), lambda o: (o, 0, 0)),
        out_shape=jax.ShapeDtypeStruct(u_tiles.shape, BF16),
        scratch_shapes=[
            pltpu.VMEM((n_rows, n_state), BF16), pltpu.VMEM((n_rows, n_state), BF16),
            pltpu.VMEM((rb, n_state), F32), pltpu.VMEM((rb, n_state), F32),
        ],
        compiler_params=pltpu.CompilerParams(
            dimension_semantics=("arbitrary",), vmem_limit_bytes=VMEM_LIMIT),
        name="s5_scan",
    )(u_tiles, win, wcat, ak)


def _post_kernel(x_ref, att_ref, y_ref, u_ref, dsk_ref, wg_ref, bg_ref, wo_ref, g2_ref,
                 w1_ref, w2_ref, gf_ref, o_ref, *, ff_chunk, final_norm):
    aw = att_ref.shape[1]
    y = y_ref[...].astype(F32) + dsk_ref[...] * u_ref[...]
    z = 0.5 * y * (1.0 + lax.erf(y * (1.0 / math.sqrt(2.0))))
    gate = jax.nn.sigmoid(jnp.dot(z.astype(BF16), wg_ref[...], preferred_element_type=F32) + bg_ref[...])
    ssm = (z * gate).astype(BF16)
    h = (x_ref[...]
         + jnp.dot(att_ref[...], wo_ref[:aw, :], preferred_element_type=F32)
         + jnp.dot(ssm, wo_ref[aw:, :], preferred_element_type=F32))
    hn2 = _rms(h, g2_ref[...]).astype(BF16)
    mlp = jnp.zeros_like(h)
    for f0 in range(0, w1_ref.shape[1], ff_chunk):
        a = jnp.maximum(jnp.dot(hn2, w1_ref[:, f0:f0 + ff_chunk], preferred_element_type=F32), 0.0)
        mlp = mlp + jnp.dot((a * a).astype(BF16), w2_ref[f0:f0 + ff_chunk, :], preferred_element_type=F32)
    h = h + mlp
    o_ref[...] = _rms(h, gf_ref[...]) if final_norm else h


def _post(x2, att, y, u, d_skip, w_glu, b_glu, w_out, g2, w1, w2, gf, *, final_norm, tm=512):
    m, d = x2.shape
    aw = att.shape[1]
    sw = u.shape[1]
    row = lambda width: pl.BlockSpec((tm, width), lambda i: (i, 0))
    const = lambda arr: pl.BlockSpec(arr.shape, lambda i: (0, 0))
    kern = functools.partial(_post_kernel, ff_chunk=1024, final_norm=final_norm)
    return pl.pallas_call(
        kern,
        grid=(m // tm,),
        in_specs=[row(d), row(aw), row(sw), row(sw),
                  const(d_skip), const(w_glu), const(b_glu), const(w_out), const(g2),
                  const(w1), const(w2), const(gf)],
        out_specs=row(d),
        out_shape=jax.ShapeDtypeStruct((m, d), F32),
        compiler_params=pltpu.CompilerParams(
            dimension_semantics=("arbitrary",), vmem_limit_bytes=VMEM_LIMIT),
        name="post_mlp",
    )(x2, att, y, u, d_skip, w_glu, b_glu, w_out, g2, w1, w2, gf)


def kernel(x, norm1_g, w_in, lam_q1, lam_k1, lam_q2, lam_k2, subln_g, ssm_lam_re, ssm_lam_im, ssm_log_dt,
           ssm_b_re, ssm_b_im, ssm_c_re, ssm_c_im, ssm_d, w_glu, b_glu, w_out, norm2_g, w_mlp1, w_mlp2,
           final_g):
    bsz, s, d = x.shape
    depth = w_in.shape[0]
    ssm_w = ssm_d.shape[-1]
    attn_w = w_out.shape[1] - ssm_w
    qk_dim = attn_w // ATTN_HEADS // 2
    scale = float(qk_dim) ** -0.5 * LOG2_E
    m = bsz * s

    h2 = x.reshape(m, d).astype(F32)
    for l in range(depth):
        lambda_init = 0.8 - 0.6 * math.exp(-0.3 * l)
        qkv, u = _inproj(h2, norm1_g[l][None].astype(F32), w_in[l].astype(BF16),
                         q_width=attn_w, qkv_width=3 * attn_w, scale=scale)

        row = lambda t: t[l][None].astype(F32)
        att = _attention(qkv.reshape(bsz, s, 3 * attn_w), row(lam_q1), row(lam_k1), row(lam_q2), row(lam_k2),
                         row(subln_g), lambda_init=lambda_init)

        pbr, pbi, pcr, pci, ak_tiles = _s5_discretise(ssm_lam_re[l], ssm_lam_im[l], ssm_log_dt[l],
                                                      ssm_b_re[l], ssm_b_im[l], ssm_c_re[l], ssm_c_im[l])
        win, wcat = _s5_chunk_weights(pbr, pbi, pcr, pci)
        nt = win.shape[1]
        n_chunks = s // S5_CHUNK
        u_tiles = (u.astype(BF16).reshape(bsz, n_chunks, S5_CHUNK, nt, LANES).transpose(3, 1, 0, 2, 4)
                   .reshape(nt, n_chunks * bsz, S5_CHUNK * LANES))
        y_tiles = _s5_chunk_scan(u_tiles, win, wcat, ak_tiles, nb=bsz)
        y = (y_tiles.reshape(nt, n_chunks, bsz, S5_CHUNK, LANES).transpose(2, 1, 3, 0, 4).reshape(m, ssm_w))

        h2 = _post(h2, att.reshape(m, attn_w), y, u, row(ssm_d), w_glu[l].astype(BF16), row(b_glu),
                   w_out[l].astype(BF16), row(norm2_g), w_mlp1[l].astype(BF16), w_mlp2[l].astype(BF16),
                   final_g[None].astype(F32), final_norm=(l == depth - 1))
    return h2.reshape(bsz, s, d).astype(x.dtype)
```

```python
import functools
import math

import jax
import jax.numpy as jnp
from jax import lax
from jax.experimental import pallas as pl
from jax.experimental.pallas import tpu as pltpu

F32 = jnp.float32
BF16 = jnp.bfloat16

EPS = 1e-6
ATTN_HEADS = 4
SSM_GROUP = 16
SSM_STATE = 64
LANES = 128
BF16_SUBLANES = 16
S5_SLAB_GROUPS = 2
S5_SLAB_LANES = S5_SLAB_GROUPS * SSM_GROUP
VMEM_LIMIT = 56 * 1024 * 1024


def _rms(x, g):
    return x * lax.rsqrt(jnp.mean(x * x, axis=-1, keepdims=True) + EPS) * g


def _inproj_kernel(x_ref, g_ref, w_ref, qkv_ref, u_ref, *, q_width, qkv_width, scale, chunk):
    hn = _rms(x_ref[...], g_ref[...]).astype(BF16)
    n_out = w_ref.shape[1]
    for c0 in range(0, n_out, chunk):
        p = jnp.dot(hn, w_ref[:, c0:c0 + chunk], preferred_element_type=F32)
        if c0 < q_width:
            p = p * scale
        if c0 < qkv_width:
            qkv_ref[:, c0:c0 + chunk] = p.astype(BF16)
        else:
            u_ref[:, c0 - qkv_width:c0 - qkv_width + chunk] = p


def _inproj(x2, g1, w_in, *, q_width, qkv_width, scale, tm=1024):
    m, d = x2.shape
    n_out = w_in.shape[1]
    kern = functools.partial(_inproj_kernel, q_width=q_width, qkv_width=qkv_width, scale=scale, chunk=512)
    return pl.pallas_call(
        kern,
        grid=(m // tm,),
        in_specs=[
            pl.BlockSpec((tm, d), lambda i: (i, 0)),
            pl.BlockSpec((1, d), lambda i: (0, 0)),
            pl.BlockSpec((d, n_out), lambda i: (0, 0)),
        ],
        out_specs=[
            pl.BlockSpec((tm, qkv_width), lambda i: (i, 0)),
            pl.BlockSpec((tm, n_out - qkv_width), lambda i: (i, 0)),
        ],
        out_shape=[
            jax.ShapeDtypeStruct((m, qkv_width), BF16),
            jax.ShapeDtypeStruct((m, n_out - qkv_width), F32),
        ],
        compiler_params=pltpu.CompilerParams(
            dimension_semantics=("arbitrary",), vmem_limit_bytes=VMEM_LIMIT),
        name="inproj",
    )(x2, g1, w_in)


POS_LOW_BITS = 3
LOG2_E = 1.4426950408889634
SAFE_LOG2_SCORE = 64.0


def _attn_kernel(lq1_ref, lk1_ref, lq2_ref, lk2_ref, sg_ref, q_ref, k_ref, v_ref, o_ref,
                 kex_ref, qex_ref, diag_ref, kx_ref, vt_ref, s0_ref, s1_ref, s2_ref, s3_ref, e0_ref, e1_ref,
                 *, qb, lambda_init):
    h = pl.program_id(0)
    s_refs, e_refs = (s0_ref, s1_ref, s2_ref, s3_ref), (e0_ref, e1_ref)
    s_len, hv = k_ref.shape[1], k_ref.shape[2]
    dh = hv // 2

    lam = (jnp.exp(jnp.sum(lq1_ref[...] * lk1_ref[...], axis=-1, keepdims=True))
           - jnp.exp(jnp.sum(lq2_ref[...] * lk2_ref[...], axis=-1, keepdims=True))
           + lambda_init)

    lane = lax.broadcasted_iota(jnp.int32, (1, hv), 1)
    data_lanes = (lane < dh, lane >= dh)

    @pl.when(pl.program_id(1) == 0)
    def _():
        slope = jnp.float32(2.0 ** -8.0)
        for hh in range(ATTN_HEADS - 1):
            slope = jnp.where(h == hh, jnp.float32(2.0 ** (-8.0 * (hh + 1) / ATTN_HEADS)), slope)
        c2 = jnp.full((1, 1), slope, F32) * LOG2_E
        c2_parts = []
        rest = c2
        for _ in range(3):
            part = rest.astype(BF16).astype(F32)
            c2_parts.append(part)
            rest = rest - part

        def extras(vals):
            out = 0.0
            for e, val in enumerate(vals):
                out = jnp.where(lane == dh + e, val, out)
            return out

        pos = lax.broadcasted_iota(jnp.int32, (s_len, 1), 0)
        pos_lo = pos & ((1 << POS_LOW_BITS) - 1)
        p_hi, p_lo = (pos - pos_lo).astype(F32), pos_lo.astype(F32)
        kex = extras(c2_parts + c2_parts + [p_hi] * 3 + [p_lo] * 3)
        qex = extras([-p_hi] * 3 + [-p_lo] * 3 + c2_parts + c2_parts)
        kex_ref[0], kex_ref[1] = kex, pltpu.roll(kex, dh, 1)
        qex_ref[0], qex_ref[1] = qex, pltpu.roll(qex, dh, 1)
        off = lax.broadcasted_iota(jnp.int32, (qb, qb), 0) - lax.broadcasted_iota(jnp.int32, (qb, qb), 1)
        diag_ref[...] = jnp.abs(off).astype(F32) * (-c2)

    kf = k_ref[0].astype(F32)
    for c in range(2):
        kx_ref[c] = jnp.where(data_lanes[c], kf, kex_ref[c]).astype(BF16)
    vt_ref[:hv, :] = v_ref[0].astype(F32).T.astype(BF16)
    pad_row = lax.broadcasted_iota(jnp.int32, (vt_ref.shape[0] - hv, s_len), 0)
    vt_ref[hv:, :] = jnp.where(pad_row == 0, 1.0, 0.0).astype(BF16)

    norm2 = lambda t: jnp.max(jnp.sum(t * t, axis=1, keepdims=True), axis=0, keepdims=True)
    small_scores = (norm2(q_ref[0].astype(F32)) * norm2(kf))[0, 0] <= SAFE_LOG2_SCORE ** 2

    n_blocks = s_len // qb

    def score_pieces(n, q_side):
        qi, c = divmod(n, 2)
        lo, hi = qi * qb, (qi + 1) * qb
        if qi not in q_side:
            q_side[qi] = q_ref[0, lo:hi, :].astype(F32)
        qf = q_side[qi]
        aug_t = lambda ex: jnp.where(data_lanes[c], qf, ex).T.astype(BF16)
        dot = lambda r0, r1, qt: jnp.dot(kx_ref[c, r0:r1, :], qt, preferred_element_type=F32)
        q_extra = qex_ref[c, lo:hi, :]
        q_left, q_right = aug_t(q_extra), aug_t(-q_extra)
        for r0 in range(0, s_len, qb):
            if r0 == lo:
                yield lo, hi, dot(lo, hi, aug_t(jnp.zeros_like(q_extra))) + diag_ref[...]
            else:
                yield r0, r0 + qb, dot(r0, r0 + qb, q_left if r0 < lo else q_right)

    def weighted_values(n):
        acc = jnp.dot(vt_ref[...], e_refs[n % 2][...], preferred_element_type=F32)
        return acc[:hv, :] / acc[hv:hv + 1, :]

    def finish_block(qi, out_t):
        o = (out_t[0] - lam * out_t[1]).T
        o = _rms(o, sg_ref[...]) * (1.0 - lambda_init)
        o_ref[0, qi * qb:(qi + 1) * qb, :] = o.astype(o_ref.dtype)

    @pl.when(small_scores)
    def _():
        q_side = {}
        for qi in range(n_blocks):
            out_t = []
            for n in (2 * qi, 2 * qi + 1):
                for r0, r1, sc in score_pieces(n, q_side):
                    e_refs[n % 2][r0:r1, :] = jnp.exp2(sc).astype(BF16)
                out_t.append(weighted_values(n))
            finish_block(qi, out_t)

    @pl.when(jnp.logical_not(small_scores))
    def _():
        q_side = {}

        def scores(n):
            for r0, r1, sc in score_pieces(n, q_side):
                s_refs[n % 4][r0:r1, :] = sc

        def exponentials(n):
            sc = s_refs[n % 4][...]
            e_refs[n % 2][...] = jnp.exp2(sc - jnp.max(sc, axis=0, keepdims=True)).astype(BF16)

        scores(0), scores(1)
        for qi in range(n_blocks):
            if qi + 1 < n_blocks:
                scores(2 * qi + 2), scores(2 * qi + 3)
            out_t = []
            for n in (2 * qi, 2 * qi + 1):
                exponentials(n)
                out_t.append(weighted_values(n))
            finish_block(qi, out_t)


def _attention(qkv3, lq1, lk1, lq2, lk2, subln_g, *, lambda_init, qb=512):
    b, s, w = qkv3.shape
    hv = w // (3 * ATTN_HEADS)
    assert s <= 256 << POS_LOW_BITS and s % qb == 0
    kern = functools.partial(_attn_kernel, qb=qb, lambda_init=lambda_init)
    small = lambda shape: pl.BlockSpec(shape, lambda hi, bi: (0, 0))
    head = lambda off: pl.BlockSpec((1, s, hv), lambda hi, bi: (bi, 0, off + hi))
    return pl.pallas_call(
        kern,
        grid=(ATTN_HEADS, b),
        in_specs=[
            small(lq1.shape), small(lk1.shape), small(lq2.shape), small(lk2.shape), small(subln_g.shape),
            head(0), head(ATTN_HEADS), head(2 * ATTN_HEADS),
        ],
        out_specs=head(0),
        out_shape=jax.ShapeDtypeStruct((b, s, ATTN_HEADS * hv), BF16),
        scratch_shapes=[
            pltpu.VMEM((2, s, hv), F32), pltpu.VMEM((2, s, hv), F32), pltpu.VMEM((qb, qb), F32),
            pltpu.VMEM((2, s, hv), BF16),
            pltpu.VMEM((hv + BF16_SUBLANES, s), BF16),
            pltpu.VMEM((s, qb), F32), pltpu.VMEM((s, qb), F32), pltpu.VMEM((s, qb), F32), pltpu.VMEM((s, qb), F32),
            pltpu.VMEM((s, qb), BF16), pltpu.VMEM((s, qb), BF16),
        ],
        compiler_params=pltpu.CompilerParams(
            dimension_semantics=("arbitrary", "arbitrary"), vmem_limit_bytes=VMEM_LIMIT),
        name="diff_attn",
    )(lq1, lk1, lq2, lk2, subln_g, qkv3, qkv3, qkv3)


S5_CHUNK = 8


def _s5_disc_kernel(lr_ref, li_ref, ldt_ref, bre_ref, bim_ref, cre_ref, cim_ref,
                    pbr_ref, pbi_ref, pcr_ref, pci_ref, akr_ref, aki_ref):
    cmul = lambda ar, ai, xr, xi: (ar * xr - ai * xi, ar * xi + ai * xr)

    lr, li = lr_ref[...], li_ref[...]
    dt = jnp.exp(ldt_ref[...])
    mag = jnp.exp(lr * dt)
    ang = li * dt
    ar, ai = mag * jnp.cos(ang), mag * jnp.sin(ang)
    nr = ar - 1.0
    den = lr * lr + li * li
    cr = (nr * lr + ai * li) / den
    ci = (ai * lr - nr * li) / den
    b_re, b_im = bre_ref[...], bim_ref[...]
    pr, pi = cr * b_re - ci * b_im, cr * b_im + ci * b_re
    kr, ki = ar, ai
    for n in range(pbr_ref.shape[0]):
        pbr_ref[n] = pr
        pbi_ref[n] = pi
        pr, pi = cmul(ar, ai, pr, pi)
        if n > 0:
            kr, ki = cmul(ar, ai, kr, ki)
    akr_ref[...] = kr
    aki_ref[...] = ki

    qr, qi = cre_ref[...], cim_ref[...]
    for n in range(pcr_ref.shape[0]):
        pcr_ref[n] = qr
        pci_ref[n] = -qi
        qr, qi = cmul(ar, ai, qr, qi)


def _s5_discretise(lam_re, lam_im, log_dt, b_re, b_im, c_re, c_im):
    nd, g, p, hc = b_re.shape
    nt = g // S5_SLAB_GROUPS
    rows, width = nd * nt * hc, S5_SLAB_GROUPS * p
    arrange = lambda t: (jnp.broadcast_to(t.astype(F32), (nd, g, hc, p)).reshape(nd, nt, S5_SLAB_GROUPS, hc, p)
                         .transpose(0, 1, 3, 2, 4).reshape(rows, width))
    args = (arrange(lam_re[:, :, None, :]), arrange(lam_im[:, :, None, :]), arrange(log_dt[..., None, None]),
            arrange(jnp.swapaxes(b_re, 2, 3)), arrange(jnp.swapaxes(b_im, 2, 3)), arrange(c_re), arrange(c_im))
    stack = lambda n: jax.ShapeDtypeStruct((n, rows, width), F32)
    one = jax.ShapeDtypeStruct((rows, width), F32)
    pbr, pbi, pcr, pci, akr, aki = pl.pallas_call(
        _s5_disc_kernel,
        out_shape=[stack(S5_CHUNK), stack(S5_CHUNK), stack(S5_CHUNK + 1), stack(S5_CHUNK + 1), one, one],
        name="s5_disc")(*args)
    tiles = lambda t: t.reshape(t.shape[0], nd, nt, hc, width)
    ak = jnp.stack([akr, aki]).reshape(2, nd, nt, hc, width)[:, :, :, 0]
    ak = ak.transpose(2, 1, 0, 3).reshape(nt, 2 * nd, width)
    return tiles(pbr), tiles(pbi), tiles(pcr), tiles(pci), ak


def _s5_weights_kernel(pbr_ref, pbi_ref, pcr_ref, pci_ref, win_ref, wcat_ref):
    chunk, nd, hc, half = pbr_ref.shape[0], pbr_ref.shape[1], pbr_ref.shape[3], pbr_ref.shape[4]
    blk = S5_SLAB_GROUPS * hc
    p = half // S5_SLAB_GROUPS
    same_group = (lax.broadcasted_iota(jnp.int32, (blk, half), 0) // hc
                  == lax.broadcasted_iota(jnp.int32, (blk, half), 1) // p)

    def expand(re, im):
        grid = lambda t: jnp.where(same_group, jnp.tile(t, (S5_SLAB_GROUPS, 1)), 0.0)
        return jnp.concatenate([grid(re), grid(im)], axis=1)

    wall = [[expand(pbr_ref[n, d, 0], pbi_ref[n, d, 0]) for n in range(chunk)] for d in range(nd)]
    mall = [[expand(pcr_ref[n, d, 0], pci_ref[n, d, 0]).T for n in range(chunk + 1)] for d in range(nd)]

    n_state = 2 * half
    for d in range(nd):
        for j in range(chunk):
            n_in = chunk - 1 - j if d == 0 else j
            win_ref[d, 0, j * blk:(j + 1) * blk, :] = wall[d][n_in].astype(win_ref.dtype)
        n_out = [j + 1 if d == 0 else chunk - j for j in range(chunk)]
        r0 = chunk * blk + d * n_state
        wcat_ref[0, r0:r0 + n_state, :] = jnp.concatenate([mall[d][n] for n in n_out], axis=1).astype(wcat_ref.dtype)

    def split_dot(a, b):
        a_hi, b_hi = a.astype(BF16), b.astype(BF16)
        a_lo, b_lo = (a - a_hi.astype(F32)).astype(BF16), (b - b_hi.astype(F32)).astype(BF16)
        return jnp.dot(jnp.concatenate([a_hi, a_hi, a_lo], axis=1), jnp.concatenate([b_hi, b_lo, b_hi], axis=0),
                       preferred_element_type=F32)

    resp = [[split_dot(wall[d][n], mall[d][0]) for n in range(chunk)] for d in range(nd)]
    both = resp[0][0] + resp[1][0]
    for j in range(chunk):
        row = [resp[0][jp - j] if jp > j else resp[1][j - jp] if jp < j else both for jp in range(chunk)]
        wcat_ref[0, j * blk:(j + 1) * blk, :] = jnp.concatenate(row, axis=1).astype(wcat_ref.dtype)


def _s5_chunk_weights(pbr, pbi, pcr, pci):
    chunk, nd, nt, hc, half = pbr.shape
    width, n_state = chunk * S5_SLAB_GROUPS * hc, 2 * half
    spec = lambda t: pl.BlockSpec((t.shape[0], nd, 1, hc, half), lambda o: (0, 0, o, 0, 0))
    return pl.pallas_call(
        _s5_weights_kernel,
        grid=(nt,),
        in_specs=[spec(pbr), spec(pbi), spec(pcr), spec(pci)],
        out_specs=[pl.BlockSpec((nd, 1, width, n_state), lambda o: (0, o, 0, 0)),
                   pl.BlockSpec((1, width + nd * n_state, width), lambda o: (o, 0, 0))],
        out_shape=[jax.ShapeDtypeStruct((nd, nt, width, n_state), BF16),
                   jax.ShapeDtypeStruct((nt, width + nd * n_state, width), BF16)],
        compiler_params=pltpu.CompilerParams(dimension_semantics=("arbitrary",), vmem_limit_bytes=VMEM_LIMIT),
        name="s5_weights",
    )(pbr, pbi, pcr, pci)


def _s5_chunk_kernel(u_ref, win_ref, wcat_ref, ak_ref, y_ref, xf_ref, xb_ref, s0_ref, s1_ref, *, nb, rb):
    n_rows, half = u_ref.shape[1], ak_ref.shape[2]
    n_blk, pair = n_rows // rb, 2 * nb
    s_refs = (s0_ref, s1_ref)
    order = {0: list(range(n_blk)), 1: list(reversed(range(n_blk)))}

    def increments(d, i):
        blk = order[d][i]
        s_refs[i % 2][...] = jnp.dot(u_ref[0, blk * rb:(blk + 1) * rb, :], win_ref[d, 0], preferred_element_type=F32)

    def scan_block(d, i, x):
        blk, s_ref, x_ref = order[d][i], s_refs[i % 2], (xf_ref, xb_ref)[d]
        ar = jnp.broadcast_to(ak_ref[0, 2 * d:2 * d + 1, :], (nb, half))
        ai = jnp.broadcast_to(ak_ref[0, 2 * d + 1:2 * d + 2, :], (nb, half))

        def advance(x, rows):
            return (ar * x[0] - ai * x[1] + s_ref[rows, :half], ar * x[1] + ai * x[0] + s_ref[rows, half:])

        def body(it, x):
            r0 = pl.multiple_of((it if d == 0 else rb // pair - 1 - it) * pair, pair)
            early, late = pl.ds(r0, nb), pl.ds(r0 + nb, nb)
            x1 = advance(x, early if d == 0 else late)
            x2 = advance(x1, late if d == 0 else early)
            enter_early, enter_late = (x, x1) if d == 0 else (x1, x)
            dst = pl.ds(pl.multiple_of(blk * rb + r0, pair), pair)
            x_ref[dst, :half] = jnp.concatenate([enter_early[0], enter_late[0]], axis=0).astype(BF16)
            x_ref[dst, half:] = jnp.concatenate([enter_early[1], enter_late[1]], axis=0).astype(BF16)
            return x2

        return lax.fori_loop(0, rb // pair, body, x, unroll=True)

    zero = jnp.zeros((nb, half), F32)
    for d in range(2):
        x = (zero, zero)
        increments(d, 0)
        for i in range(n_blk):
            if i + 1 < n_blk:
                increments(d, i + 1)
            x = scan_block(d, i, x)

    for blk in range(n_blk):
        rows = slice(blk * rb, (blk + 1) * rb)
        lhs = jnp.concatenate([u_ref[0, rows, :], xf_ref[rows, :], xb_ref[rows, :]], axis=1)
        y_ref[0, rows, :] = jnp.dot(lhs, wcat_ref[0], preferred_element_type=F32).astype(y_ref.dtype)


def _s5_chunk_scan(u_tiles, win, wcat, ak, *, nb, rb=256):
    nt, n_rows, width = u_tiles.shape
    n_state = win.shape[3]
    kern = functools.partial(_s5_chunk_kernel, nb=nb, rb=rb)
    return pl.pallas_call(
        kern,
        grid=(nt,),
        in_specs=[
            pl.BlockSpec((1, n_rows, width), lambda o: (o, 0, 0)),
            pl.BlockSpec((win.shape[0], 1, width, n_state), lambda o: (0, o, 0, 0)),
            pl.BlockSpec((1,) + wcat.shape[1:], lambda o: (o, 0, 0)),
            pl.BlockSpec((1,) + ak.shape[1:], lambda o: (o, 0, 0)),
        ],
        out_specs=pl.BlockSpec((1, n_rows, width), lambda o: (o, 0, 0)),
        out_shape=jax.ShapeDtypeStruct(u_tiles.shape, BF16),
        scratch_shapes=[
            pltpu.VMEM((n_rows, n_state), BF16), pltpu.VMEM((n_rows, n_state), BF16),
            pltpu.VMEM((rb, n_state), F32), pltpu.VMEM((rb, n_state), F32),
        ],
        compiler_params=pltpu.CompilerParams(
            dimension_semantics=("arbitrary",), vmem_limit_bytes=VMEM_LIMIT),
        name="s5_scan",
    )(u_tiles, win, wcat, ak)


def _post_kernel(x_ref, att_ref, y_ref, u_ref, dsk_ref, wg_ref, bg_ref, wo_ref, g2_ref,
                 w1_ref, w2_ref, gf_ref, o_ref, *, ff_chunk, final_norm):
    aw = att_ref.shape[1]
    y = y_ref[...].astype(F32) + dsk_ref[...] * u_ref[...]
    z = 0.5 * y * (1.0 + lax.erf(y * (1.0 / math.sqrt(2.0))))
    gate = jax.nn.sigmoid(jnp.dot(z.astype(BF16), wg_ref[...], preferred_element_type=F32) + bg_ref[...])
    ssm = (z * gate).astype(BF16)
    h = (x_ref[...]
         + jnp.dot(att_ref[...], wo_ref[:aw, :], preferred_element_type=F32)
         + jnp.dot(ssm, wo_ref[aw:, :], preferred_element_type=F32))
    hn2 = _rms(h, g2_ref[...]).astype(BF16)
    mlp = jnp.zeros_like(h)
    for f0 in range(0, w1_ref.shape[1], ff_chunk):
        a = jnp.maximum(jnp.dot(hn2, w1_ref[:, f0:f0 + ff_chunk], preferred_element_type=F32), 0.0)
        mlp = mlp + jnp.dot((a * a).astype(BF16), w2_ref[f0:f0 + ff_chunk, :], preferred_element_type=F32)
    h = h + mlp
    o_ref[...] = _rms(h, gf_ref[...]) if final_norm else h


def _post(x2, att, y, u, d_skip, w_glu, b_glu, w_out, g2, w1, w2, gf, *, final_norm, tm=512):
    m, d = x2.shape
    aw = att.shape[1]
    sw = u.shape[1]
    row = lambda width: pl.BlockSpec((tm, width), lambda i: (i, 0))
    const = lambda arr: pl.BlockSpec(arr.shape, lambda i: (0, 0))
    kern = functools.partial(_post_kernel, ff_chunk=1024, final_norm=final_norm)
    return pl.pallas_call(
        kern,
        grid=(m // tm,),
        in_specs=[row(d), row(aw), row(sw), row(sw),
                  const(d_skip), const(w_glu), const(b_glu), const(w_out), const(g2),
                  const(w1), const(w2), const(gf)],
        out_specs=row(d),
        out_shape=jax.ShapeDtypeStruct((m, d), F32),
        compiler_params=pltpu.CompilerParams(
            dimension_semantics=("arbitrary",), vmem_limit_bytes=VMEM_LIMIT),
        name="post_mlp",
    )(x2, att, y, u, d_skip, w_glu, b_glu, w_out, g2, w1, w2, gf)


def kernel(x, norm1_g, w_in, lam_q1, lam_k1, lam_q2, lam_k2, subln_g, ssm_lam_re, ssm_lam_im, ssm_log_dt,
           ssm_b_re, ssm_b_im, ssm_c_re, ssm_c_im, ssm_d, w_glu, b_glu, w_out, norm2_g, w_mlp1, w_mlp2,
           final_g):
    bsz, s, d = x.shape
    depth = w_in.shape[0]
    ssm_w = ssm_d.shape[-1]
    attn_w = w_out.shape[1] - ssm_w
    qk_dim = attn_w // ATTN_HEADS // 2
    scale = float(qk_dim) ** -0.5 * LOG2_E
    m = bsz * s

    h2 = x.reshape(m, d).astype(F32)
    for l in range(depth):
        lambda_init = 0.8 - 0.6 * math.exp(-0.3 * l)
        qkv, u = _inproj(h2, norm1_g[l][None].astype(F32), w_in[l].astype(BF16),
                         q_width=attn_w, qkv_width=3 * attn_w, scale=scale)

        row = lambda t: t[l][None].astype(F32)
        att = _attention(qkv.reshape(bsz, s, 3 * attn_w), row(lam_q1), row(lam_k1), row(lam_q2), row(lam_k2),
                         row(subln_g), lambda_init=lambda_init)

        pbr, pbi, pcr, pci, ak_tiles = _s5_discretise(ssm_lam_re[l], ssm_lam_im[l], ssm_log_dt[l],
                                                      ssm_b_re[l], ssm_b_im[l], ssm_c_re[l], ssm_c_im[l])
        win, wcat = _s5_chunk_weights(pbr, pbi, pcr, pci)
        nt = win.shape[1]
        n_chunks = s // S5_CHUNK
        u_tiles = (u.astype(BF16).reshape(bsz, n_chunks, S5_CHUNK, nt, S5_SLAB_LANES).transpose(3, 1, 0, 2, 4)
                   .reshape(nt, n_chunks * bsz, S5_CHUNK * S5_SLAB_LANES))
        y_tiles = _s5_chunk_scan(u_tiles, win, wcat, ak_tiles, nb=bsz)
        y = (y_tiles.reshape(nt, n_chunks, bsz, S5_CHUNK, S5_SLAB_LANES).transpose(2, 1, 3, 0, 4).reshape(m, ssm_w))

        h2 = _post(h2, att.reshape(m, attn_w), y, u, row(ssm_d), w_glu[l].astype(BF16), row(b_glu),
                   w_out[l].astype(BF16), row(norm2_g), w_mlp1[l].astype(BF16), w_mlp2[l].astype(BF16),
                   final_g[None].astype(F32), final_norm=(l == depth - 1))
    return h2.reshape(bsz, s, d).astype(x.dtype)
```

```python
import functools
import math

import jax
import jax.numpy as jnp
from jax import lax
from jax.experimental import pallas as pl
from jax.experimental.pallas import tpu as pltpu

F32 = jnp.float32
BF16 = jnp.bfloat16

EPS = 1e-6
ATTN_HEADS = 4
SSM_GROUP = 16
SSM_STATE = 64
LANES = 128
BF16_SUBLANES = 16
S5_SLAB_GROUPS = 2
S5_SLAB_LANES = S5_SLAB_GROUPS * SSM_GROUP
VMEM_LIMIT = 56 * 1024 * 1024


def _rms(x, g):
    return x * lax.rsqrt(jnp.mean(x * x, axis=-1, keepdims=True) + EPS) * g


def _to_slabs(u3):
    nb, tt, w = u3.shape
    n_ch = tt // S5_CHUNK
    t = jnp.transpose(u3.reshape(nb, n_ch, S5_CHUNK, w), (1, 2, 0, 3))
    slabs = []
    for q in range(w // S5_SLAB_LANES):
        lanes = slice(q * S5_SLAB_LANES, (q + 1) * S5_SLAB_LANES)
        rows = [jnp.concatenate([t[c, j][:, lanes] for j in range(S5_CHUNK)], axis=1) for c in range(n_ch)]
        slabs.append(jnp.concatenate(rows, axis=0))
    return slabs


def _from_slabs(slab_ref, nb):
    n_slabs, rows, _ = slab_ref.shape
    n_ch = rows // nb
    per_chunk = []
    for c in range(n_ch):
        steps = [jnp.concatenate([slab_ref[q, c * nb:(c + 1) * nb, j * S5_SLAB_LANES:(j + 1) * S5_SLAB_LANES]
                                  for q in range(n_slabs)], axis=1) for j in range(S5_CHUNK)]
        per_chunk.append(jnp.stack(steps, axis=0))
    t = jnp.stack(per_chunk, axis=0)
    return jnp.transpose(t, (2, 0, 1, 3)).reshape(nb, n_ch * S5_CHUNK, n_slabs * S5_SLAB_LANES)


def _inproj_kernel(x_ref, g_ref, w_ref, qkv_ref, us_ref, *, q_width, qkv_width, scale, chunk):
    nb, tt, d = x_ref.shape
    hn = _rms(x_ref[...].reshape(nb * tt, d), g_ref[...]).astype(BF16)
    u = jnp.dot(hn, w_ref[:, qkv_width:], preferred_element_type=F32)
    for q, slab in enumerate(_to_slabs(u.reshape(nb, tt, u.shape[1]))):
        us_ref[q] = slab
    for c0 in range(0, qkv_width, chunk):
        p = jnp.dot(hn, w_ref[:, c0:c0 + chunk], preferred_element_type=F32)
        if c0 < q_width:
            p = p * scale
        qkv_ref[:, :, c0:c0 + chunk] = p.astype(BF16).reshape(nb, tt, chunk)


def _inproj(x3, g1, w_in, *, q_width, qkv_width, scale, tt=128):
    nb, s, d = x3.shape
    n_out = w_in.shape[1]
    n_slabs = (n_out - qkv_width) // S5_SLAB_LANES
    kern = functools.partial(_inproj_kernel, q_width=q_width, qkv_width=qkv_width, scale=scale, chunk=512)
    return pl.pallas_call(
        kern,
        grid=(s // tt,),
        in_specs=[
            pl.BlockSpec((nb, tt, d), lambda i: (0, i, 0)),
            pl.BlockSpec((1, d), lambda i: (0, 0)),
            pl.BlockSpec((d, n_out), lambda i: (0, 0)),
        ],
        out_specs=[
            pl.BlockSpec((nb, tt, qkv_width), lambda i: (0, i, 0)),
            pl.BlockSpec((n_slabs, tt // S5_CHUNK * nb, S5_CHUNK * S5_SLAB_LANES), lambda i: (0, i, 0)),
        ],
        out_shape=[
            jax.ShapeDtypeStruct((nb, s, qkv_width), BF16),
            jax.ShapeDtypeStruct((n_slabs, s // S5_CHUNK * nb, S5_CHUNK * S5_SLAB_LANES), F32),
        ],
        compiler_params=pltpu.CompilerParams(
            dimension_semantics=("arbitrary",), vmem_limit_bytes=VMEM_LIMIT),
        name="inproj",
    )(x3, g1, w_in)


POS_LOW_BITS = 3
LOG2_E = 1.4426950408889634
SAFE_LOG2_SCORE = 64.0


def _attn_kernel(lq1_ref, lk1_ref, lq2_ref, lk2_ref, sg_ref, q_ref, k_ref, v_ref, o_ref,
                 kex_ref, qex_ref, diag_ref, kx_ref, vt_ref, s0_ref, s1_ref, s2_ref, s3_ref, e0_ref, e1_ref,
                 *, qb, lambda_init):
    h = pl.program_id(0)
    s_refs, e_refs = (s0_ref, s1_ref, s2_ref, s3_ref), (e0_ref, e1_ref)
    s_len, hv = k_ref.shape[1], k_ref.shape[2]
    dh = hv // 2

    lam = (jnp.exp(jnp.sum(lq1_ref[...] * lk1_ref[...], axis=-1, keepdims=True))
           - jnp.exp(jnp.sum(lq2_ref[...] * lk2_ref[...], axis=-1, keepdims=True))
           + lambda_init)

    lane = lax.broadcasted_iota(jnp.int32, (1, hv), 1)
    data_lanes = (lane < dh, lane >= dh)

    @pl.when(pl.program_id(1) == 0)
    def _():
        slope = jnp.float32(2.0 ** -8.0)
        for hh in range(ATTN_HEADS - 1):
            slope = jnp.where(h == hh, jnp.float32(2.0 ** (-8.0 * (hh + 1) / ATTN_HEADS)), slope)
        c2 = jnp.full((1, 1), slope, F32) * LOG2_E
        c2_parts = []
        rest = c2
        for _ in range(3):
            part = rest.astype(BF16).astype(F32)
            c2_parts.append(part)
            rest = rest - part

        def extras(vals):
            out = 0.0
            for e, val in enumerate(vals):
                out = jnp.where(lane == dh + e, val, out)
            return out

        pos = lax.broadcasted_iota(jnp.int32, (s_len, 1), 0)
        pos_lo = pos & ((1 << POS_LOW_BITS) - 1)
        p_hi, p_lo = (pos - pos_lo).astype(F32), pos_lo.astype(F32)
        kex = extras(c2_parts + c2_parts + [p_hi] * 3 + [p_lo] * 3)
        qex = extras([-p_hi] * 3 + [-p_lo] * 3 + c2_parts + c2_parts)
        kex_ref[0], kex_ref[1] = kex, pltpu.roll(kex, dh, 1)
        qex_ref[0], qex_ref[1] = qex, pltpu.roll(qex, dh, 1)
        off = lax.broadcasted_iota(jnp.int32, (qb, qb), 0) - lax.broadcasted_iota(jnp.int32, (qb, qb), 1)
        diag_ref[...] = jnp.abs(off).astype(F32) * (-c2)

    kf = k_ref[0].astype(F32)
    for c in range(2):
        kx_ref[c] = jnp.where(data_lanes[c], kf, kex_ref[c]).astype(BF16)
    vt_ref[:hv, :] = v_ref[0].astype(F32).T.astype(BF16)
    pad_row = lax.broadcasted_iota(jnp.int32, (vt_ref.shape[0] - hv, s_len), 0)
    vt_ref[hv:, :] = jnp.where(pad_row == 0, 1.0, 0.0).astype(BF16)

    norm2 = lambda t: jnp.max(jnp.sum(t * t, axis=1, keepdims=True), axis=0, keepdims=True)
    small_scores = (norm2(q_ref[0].astype(F32)) * norm2(kf))[0, 0] <= SAFE_LOG2_SCORE ** 2

    n_blocks = s_len // qb

    def score_pieces(n, q_side):
        qi, c = divmod(n, 2)
        lo, hi = qi * qb, (qi + 1) * qb
        if qi not in q_side:
            q_side[qi] = q_ref[0, lo:hi, :].astype(F32)
        qf = q_side[qi]
        aug_t = lambda ex: jnp.where(data_lanes[c], qf, ex).T.astype(BF16)
        dot = lambda r0, r1, qt: jnp.dot(kx_ref[c, r0:r1, :], qt, preferred_element_type=F32)
        q_extra = qex_ref[c, lo:hi, :]
        q_left, q_right = aug_t(q_extra), aug_t(-q_extra)
        for r0 in range(0, s_len, qb):
            if r0 == lo:
                yield lo, hi, dot(lo, hi, aug_t(jnp.zeros_like(q_extra))) + diag_ref[...]
            else:
                yield r0, r0 + qb, dot(r0, r0 + qb, q_left if r0 < lo else q_right)

    def weighted_values(n):
        acc = jnp.dot(vt_ref[...], e_refs[n % 2][...], preferred_element_type=F32)
        return acc[:hv, :] / acc[hv:hv + 1, :]

    def finish_block(qi, out_t):
        o = (out_t[0] - lam * out_t[1]).T
        o = _rms(o, sg_ref[...]) * (1.0 - lambda_init)
        o_ref[0, qi * qb:(qi + 1) * qb, :] = o.astype(o_ref.dtype)

    @pl.when(small_scores)
    def _():
        q_side = {}
        for qi in range(n_blocks):
            out_t = []
            for n in (2 * qi, 2 * qi + 1):
                for r0, r1, sc in score_pieces(n, q_side):
                    e_refs[n % 2][r0:r1, :] = jnp.exp2(sc).astype(BF16)
                out_t.append(weighted_values(n))
            finish_block(qi, out_t)

    @pl.when(jnp.logical_not(small_scores))
    def _():
        q_side = {}

        def scores(n):
            for r0, r1, sc in score_pieces(n, q_side):
                s_refs[n % 4][r0:r1, :] = sc

        def exponentials(n):
            sc = s_refs[n % 4][...]
            e_refs[n % 2][...] = jnp.exp2(sc - jnp.max(sc, axis=0, keepdims=True)).astype(BF16)

        scores(0), scores(1)
        for qi in range(n_blocks):
            if qi + 1 < n_blocks:
                scores(2 * qi + 2), scores(2 * qi + 3)
            out_t = []
            for n in (2 * qi, 2 * qi + 1):
                exponentials(n)
                out_t.append(weighted_values(n))
            finish_block(qi, out_t)


def _attention(qkv3, lq1, lk1, lq2, lk2, subln_g, *, lambda_init, qb=512):
    b, s, w = qkv3.shape
    hv = w // (3 * ATTN_HEADS)
    assert s <= 256 << POS_LOW_BITS and s % qb == 0
    kern = functools.partial(_attn_kernel, qb=qb, lambda_init=lambda_init)
    small = lambda shape: pl.BlockSpec(shape, lambda hi, bi: (0, 0))
    head = lambda off: pl.BlockSpec((1, s, hv), lambda hi, bi: (bi, 0, off + hi))
    return pl.pallas_call(
        kern,
        grid=(ATTN_HEADS, b),
        in_specs=[
            small(lq1.shape), small(lk1.shape), small(lq2.shape), small(lk2.shape), small(subln_g.shape),
            head(0), head(ATTN_HEADS), head(2 * ATTN_HEADS),
        ],
        out_specs=head(0),
        out_shape=jax.ShapeDtypeStruct((b, s, ATTN_HEADS * hv), BF16),
        scratch_shapes=[
            pltpu.VMEM((2, s, hv), F32), pltpu.VMEM((2, s, hv), F32), pltpu.VMEM((qb, qb), F32),
            pltpu.VMEM((2, s, hv), BF16),
            pltpu.VMEM((hv + BF16_SUBLANES, s), BF16),
            pltpu.VMEM((s, qb), F32), pltpu.VMEM((s, qb), F32), pltpu.VMEM((s, qb), F32), pltpu.VMEM((s, qb), F32),
            pltpu.VMEM((s, qb), BF16), pltpu.VMEM((s, qb), BF16),
        ],
        compiler_params=pltpu.CompilerParams(
            dimension_semantics=("arbitrary", "arbitrary"), vmem_limit_bytes=VMEM_LIMIT),
        name="diff_attn",
    )(lq1, lk1, lq2, lk2, subln_g, qkv3, qkv3, qkv3)


S5_CHUNK = 8


def _s5_disc_kernel(lr_ref, li_ref, ldt_ref, bre_ref, bim_ref, cre_ref, cim_ref,
                    pbr_ref, pbi_ref, pcr_ref, pci_ref, akr_ref, aki_ref):
    cmul = lambda ar, ai, xr, xi: (ar * xr - ai * xi, ar * xi + ai * xr)

    lr, li = lr_ref[...], li_ref[...]
    dt = jnp.exp(ldt_ref[...])
    mag = jnp.exp(lr * dt)
    ang = li * dt
    ar, ai = mag * jnp.cos(ang), mag * jnp.sin(ang)
    nr = ar - 1.0
    den = lr * lr + li * li
    cr = (nr * lr + ai * li) / den
    ci = (ai * lr - nr * li) / den
    b_re, b_im = bre_ref[...], bim_ref[...]
    pr, pi = cr * b_re - ci * b_im, cr * b_im + ci * b_re
    kr, ki = ar, ai
    for n in range(pbr_ref.shape[0]):
        pbr_ref[n] = pr
        pbi_ref[n] = pi
        pr, pi = cmul(ar, ai, pr, pi)
        if n > 0:
            kr, ki = cmul(ar, ai, kr, ki)
    akr_ref[...] = kr
    aki_ref[...] = ki

    qr, qi = cre_ref[...], cim_ref[...]
    for n in range(pcr_ref.shape[0]):
        pcr_ref[n] = qr
        pci_ref[n] = -qi
        qr, qi = cmul(ar, ai, qr, qi)


def _s5_discretise(lam_re, lam_im, log_dt, b_re, b_im, c_re, c_im):
    nd, g, p, hc = b_re.shape
    nt = g // S5_SLAB_GROUPS
    rows, width = nd * nt * hc, S5_SLAB_GROUPS * p
    arrange = lambda t: (jnp.broadcast_to(t.astype(F32), (nd, g, hc, p)).reshape(nd, nt, S5_SLAB_GROUPS, hc, p)
                         .transpose(0, 1, 3, 2, 4).reshape(rows, width))
    args = (arrange(lam_re[:, :, None, :]), arrange(lam_im[:, :, None, :]), arrange(log_dt[..., None, None]),
            arrange(jnp.swapaxes(b_re, 2, 3)), arrange(jnp.swapaxes(b_im, 2, 3)), arrange(c_re), arrange(c_im))
    stack = lambda n: jax.ShapeDtypeStruct((n, rows, width), F32)
    one = jax.ShapeDtypeStruct((rows, width), F32)
    pbr, pbi, pcr, pci, akr, aki = pl.pallas_call(
        _s5_disc_kernel,
        out_shape=[stack(S5_CHUNK), stack(S5_CHUNK), stack(S5_CHUNK + 1), stack(S5_CHUNK + 1), one, one],
        name="s5_disc")(*args)
    tiles = lambda t: t.reshape(t.shape[0], nd, nt, hc, width)
    ak = jnp.stack([akr, aki]).reshape(2, nd, nt, hc, width)[:, :, :, 0]
    ak = ak.transpose(2, 1, 0, 3).reshape(nt, 2 * nd, width)
    return tiles(pbr), tiles(pbi), tiles(pcr), tiles(pci), ak


def _s5_weights_kernel(pbr_ref, pbi_ref, pcr_ref, pci_ref, win_ref, wcat_ref):
    chunk, nd, hc, half = pbr_ref.shape[0], pbr_ref.shape[1], pbr_ref.shape[3], pbr_ref.shape[4]
    blk = S5_SLAB_GROUPS * hc
    p = half // S5_SLAB_GROUPS
    same_group = (lax.broadcasted_iota(jnp.int32, (blk, half), 0) // hc
                  == lax.broadcasted_iota(jnp.int32, (blk, half), 1) // p)

    def expand(re, im):
        grid = lambda t: jnp.where(same_group, jnp.tile(t, (S5_SLAB_GROUPS, 1)), 0.0)
        return jnp.concatenate([grid(re), grid(im)], axis=1)

    wall = [[expand(pbr_ref[n, d, 0], pbi_ref[n, d, 0]) for n in range(chunk)] for d in range(nd)]
    mall = [[expand(pcr_ref[n, d, 0], pci_ref[n, d, 0]).T for n in range(chunk + 1)] for d in range(nd)]

    n_state = 2 * half
    for d in range(nd):
        for j in range(chunk):
            n_in = chunk - 1 - j if d == 0 else j
            win_ref[d, 0, j * blk:(j + 1) * blk, :] = wall[d][n_in].astype(win_ref.dtype)
        n_out = [j + 1 if d == 0 else chunk - j for j in range(chunk)]
        r0 = chunk * blk + d * n_state
        wcat_ref[0, r0:r0 + n_state, :] = jnp.concatenate([mall[d][n] for n in n_out], axis=1).astype(wcat_ref.dtype)

    def split_dot(a, b):
        a_hi, b_hi = a.astype(BF16), b.astype(BF16)
        a_lo, b_lo = (a - a_hi.astype(F32)).astype(BF16), (b - b_hi.astype(F32)).astype(BF16)
        return jnp.dot(jnp.concatenate([a_hi, a_hi, a_lo], axis=1), jnp.concatenate([b_hi, b_lo, b_hi], axis=0),
                       preferred_element_type=F32)

    resp = [[split_dot(wall[d][n], mall[d][0]) for n in range(chunk)] for d in range(nd)]
    both = resp[0][0] + resp[1][0]
    for j in range(chunk):
        row = [resp[0][jp - j] if jp > j else resp[1][j - jp] if jp < j else both for jp in range(chunk)]
        wcat_ref[0, j * blk:(j + 1) * blk, :] = jnp.concatenate(row, axis=1).astype(wcat_ref.dtype)


def _s5_chunk_weights(pbr, pbi, pcr, pci):
    chunk, nd, nt, hc, half = pbr.shape
    width, n_state = chunk * S5_SLAB_GROUPS * hc, 2 * half
    spec = lambda t: pl.BlockSpec((t.shape[0], nd, 1, hc, half), lambda o: (0, 0, o, 0, 0))
    return pl.pallas_call(
        _s5_weights_kernel,
        grid=(nt,),
        in_specs=[spec(pbr), spec(pbi), spec(pcr), spec(pci)],
        out_specs=[pl.BlockSpec((nd, 1, width, n_state), lambda o: (0, o, 0, 0)),
                   pl.BlockSpec((1, width + nd * n_state, width), lambda o: (o, 0, 0))],
        out_shape=[jax.ShapeDtypeStruct((nd, nt, width, n_state), BF16),
                   jax.ShapeDtypeStruct((nt, width + nd * n_state, width), BF16)],
        compiler_params=pltpu.CompilerParams(dimension_semantics=("arbitrary",), vmem_limit_bytes=VMEM_LIMIT),
        name="s5_weights",
    )(pbr, pbi, pcr, pci)


def _s5_chunk_kernel(u_ref, win_ref, wcat_ref, ak_ref, dsk_ref, y_ref, xf_ref, xb_ref, s0_ref, s1_ref, *, nb, rb):
    n_rows, half = u_ref.shape[1], ak_ref.shape[2]
    n_blk, pair = n_rows // rb, 2 * nb
    s_refs = (s0_ref, s1_ref)
    order = {0: list(range(n_blk)), 1: list(reversed(range(n_blk)))}

    def increments(d, i):
        blk = order[d][i]
        s_refs[i % 2][...] = jnp.dot(u_ref[0, blk * rb:(blk + 1) * rb, :].astype(BF16), win_ref[d, 0],
                                     preferred_element_type=F32)

    def scan_block(d, i, x):
        blk, s_ref, x_ref = order[d][i], s_refs[i % 2], (xf_ref, xb_ref)[d]
        ar = jnp.broadcast_to(ak_ref[0, 2 * d:2 * d + 1, :], (nb, half))
        ai = jnp.broadcast_to(ak_ref[0, 2 * d + 1:2 * d + 2, :], (nb, half))

        def advance(x, rows):
            return (ar * x[0] - ai * x[1] + s_ref[rows, :half], ar * x[1] + ai * x[0] + s_ref[rows, half:])

        def body(it, x):
            r0 = pl.multiple_of((it if d == 0 else rb // pair - 1 - it) * pair, pair)
            early, late = pl.ds(r0, nb), pl.ds(r0 + nb, nb)
            x1 = advance(x, early if d == 0 else late)
            x2 = advance(x1, late if d == 0 else early)
            enter_early, enter_late = (x, x1) if d == 0 else (x1, x)
            dst = pl.ds(pl.multiple_of(blk * rb + r0, pair), pair)
            x_ref[dst, :half] = jnp.concatenate([enter_early[0], enter_late[0]], axis=0).astype(BF16)
            x_ref[dst, half:] = jnp.concatenate([enter_early[1], enter_late[1]], axis=0).astype(BF16)
            return x2

        return lax.fori_loop(0, rb // pair, body, x, unroll=True)

    zero = jnp.zeros((nb, half), F32)
    for d in range(2):
        x = (zero, zero)
        increments(d, 0)
        for i in range(n_blk):
            if i + 1 < n_blk:
                increments(d, i + 1)
            x = scan_block(d, i, x)

    for blk in range(n_blk):
        rows = slice(blk * rb, (blk + 1) * rb)
        u = u_ref[0, rows, :]
        lhs = jnp.concatenate([u.astype(BF16), xf_ref[rows, :], xb_ref[rows, :]], axis=1)
        y_ref[0, rows, :] = jnp.dot(lhs, wcat_ref[0], preferred_element_type=F32) + dsk_ref[0] * u


def _s5_chunk_scan(u_tiles, win, wcat, ak, dsk, *, nb, rb=256):
    nt, n_rows, width = u_tiles.shape
    n_state = win.shape[3]
    kern = functools.partial(_s5_chunk_kernel, nb=nb, rb=rb)
    return pl.pallas_call(
        kern,
        grid=(nt,),
        in_specs=[
            pl.BlockSpec((1, n_rows, width), lambda o: (o, 0, 0)),
            pl.BlockSpec((win.shape[0], 1, width, n_state), lambda o: (0, o, 0, 0)),
            pl.BlockSpec((1,) + wcat.shape[1:], lambda o: (o, 0, 0)),
            pl.BlockSpec((1,) + ak.shape[1:], lambda o: (o, 0, 0)),
            pl.BlockSpec((1,) + dsk.shape[1:], lambda o: (o, 0, 0)),
        ],
        out_specs=pl.BlockSpec((1, n_rows, width), lambda o: (o, 0, 0)),
        out_shape=jax.ShapeDtypeStruct(u_tiles.shape, F32),
        scratch_shapes=[
            pltpu.VMEM((n_rows, n_state), BF16), pltpu.VMEM((n_rows, n_state), BF16),
            pltpu.VMEM((rb, n_state), F32), pltpu.VMEM((rb, n_state), F32),
        ],
        compiler_params=pltpu.CompilerParams(
            dimension_semantics=("arbitrary",), vmem_limit_bytes=VMEM_LIMIT),
        name="s5_scan",
    )(u_tiles, win, wcat, ak, dsk)


def _post_kernel(x_ref, att_ref, ys_ref, wg_ref, bg_ref, wo_ref, g2_ref,
                 w1_ref, w2_ref, gf_ref, o_ref, *, ff_chunk, final_norm):
    nb, tt, d = x_ref.shape
    aw = att_ref.shape[2]
    y = _from_slabs(ys_ref, nb).reshape(nb * tt, -1)
    z = 0.5 * y * (1.0 + lax.erf(y * (1.0 / math.sqrt(2.0))))
    gate = jax.nn.sigmoid(jnp.dot(z.astype(BF16), wg_ref[...], preferred_element_type=F32) + bg_ref[...])
    ssm = (z * gate).astype(BF16)
    h = (x_ref[...].reshape(nb * tt, d)
         + jnp.dot(att_ref[...].reshape(nb * tt, aw), wo_ref[:aw, :], preferred_element_type=F32)
         + jnp.dot(ssm, wo_ref[aw:, :], preferred_element_type=F32))
    hn2 = _rms(h, g2_ref[...]).astype(BF16)
    mlp = jnp.zeros_like(h)
    for f0 in range(0, w1_ref.shape[1], ff_chunk):
        a = jnp.maximum(jnp.dot(hn2, w1_ref[:, f0:f0 + ff_chunk], preferred_element_type=F32), 0.0)
        mlp = mlp + jnp.dot((a * a).astype(BF16), w2_ref[f0:f0 + ff_chunk, :], preferred_element_type=F32)
    h = h + mlp
    o_ref[...] = (_rms(h, gf_ref[...]) if final_norm else h).reshape(nb, tt, d)


def _post(x3, att, y_slabs, w_glu, b_glu, w_out, g2, w1, w2, gf, *, final_norm, tt=64):
    nb, s, d = x3.shape
    aw = att.shape[2]
    n_slabs, _, slab_w = y_slabs.shape
    rows = lambda width: pl.BlockSpec((nb, tt, width), lambda i: (0, i, 0))
    const = lambda arr: pl.BlockSpec(arr.shape, lambda i: (0, 0))
    kern = functools.partial(_post_kernel, ff_chunk=1024, final_norm=final_norm)
    return pl.pallas_call(
        kern,
        grid=(s // tt,),
        in_specs=[rows(d), rows(aw),
                  pl.BlockSpec((n_slabs, tt // S5_CHUNK * nb, slab_w), lambda i: (0, i, 0)),
                  const(w_glu), const(b_glu), const(w_out), const(g2), const(w1), const(w2), const(gf)],
        out_specs=rows(d),
        out_shape=jax.ShapeDtypeStruct((nb, s, d), F32),
        compiler_params=pltpu.CompilerParams(
            dimension_semantics=("arbitrary",), vmem_limit_bytes=VMEM_LIMIT),
        name="post_mlp",
    )(x3, att, y_slabs, w_glu, b_glu, w_out, g2, w1, w2, gf)


def kernel(x, norm1_g, w_in, lam_q1, lam_k1, lam_q2, lam_k2, subln_g, ssm_lam_re, ssm_lam_im, ssm_log_dt,
           ssm_b_re, ssm_b_im, ssm_c_re, ssm_c_im, ssm_d, w_glu, b_glu, w_out, norm2_g, w_mlp1, w_mlp2,
           final_g):
    bsz, s, d = x.shape
    depth = w_in.shape[0]
    ssm_w = ssm_d.shape[-1]
    attn_w = w_out.shape[1] - ssm_w
    qk_dim = attn_w // ATTN_HEADS // 2
    scale = float(qk_dim) ** -0.5 * LOG2_E

    h3 = x.astype(F32)
    for l in range(depth):
        lambda_init = 0.8 - 0.6 * math.exp(-0.3 * l)
        qkv, u_slabs = _inproj(h3, norm1_g[l][None].astype(F32), w_in[l].astype(BF16),
                               q_width=attn_w, qkv_width=3 * attn_w, scale=scale)

        row = lambda t: t[l][None].astype(F32)
        att = _attention(qkv, row(lam_q1), row(lam_k1), row(lam_q2), row(lam_k2), row(subln_g),
                         lambda_init=lambda_init)

        pbr, pbi, pcr, pci, ak_slabs = _s5_discretise(ssm_lam_re[l], ssm_lam_im[l], ssm_log_dt[l],
                                                      ssm_b_re[l], ssm_b_im[l], ssm_c_re[l], ssm_c_im[l])
        win, wcat = _s5_chunk_weights(pbr, pbi, pcr, pci)
        dsk_slabs = jnp.tile(ssm_d[l].astype(F32).reshape(-1, 1, S5_SLAB_LANES), (1, 1, S5_CHUNK))
        y_slabs = _s5_chunk_scan(u_slabs, win, wcat, ak_slabs, dsk_slabs, nb=bsz)

        h3 = _post(h3, att, y_slabs, w_glu[l].astype(BF16), row(b_glu), w_out[l].astype(BF16), row(norm2_g),
                   w_mlp1[l].astype(BF16), w_mlp2[l].astype(BF16), final_g[None].astype(F32),
                   final_norm=(l == depth - 1))
    return h3.astype(x.dtype)
```

```python
import functools
import math

import jax
import jax.numpy as jnp
from jax import lax
from jax.experimental import pallas as pl
from jax.experimental.pallas import tpu as pltpu

F32 = jnp.float32
BF16 = jnp.bfloat16

EPS = 1e-6
ATTN_HEADS = 4
SSM_GROUP = 16
SSM_STATE = 64
LANES = 128
BF16_SUBLANES = 16
S5_SLAB_GROUPS = 2
S5_SLAB_LANES = S5_SLAB_GROUPS * SSM_GROUP
VMEM_LIMIT = 56 * 1024 * 1024


def _rms(x, g):
    return x * lax.rsqrt(jnp.mean(x * x, axis=-1, keepdims=True) + EPS) * g


def _to_slabs(u3):
    nb, tt, w = u3.shape
    n_ch = tt // S5_CHUNK
    t = jnp.transpose(u3.reshape(nb, n_ch, S5_CHUNK, w), (1, 2, 0, 3))
    slabs = []
    for q in range(w // S5_SLAB_LANES):
        lanes = slice(q * S5_SLAB_LANES, (q + 1) * S5_SLAB_LANES)
        rows = [jnp.concatenate([t[c, j][:, lanes] for j in range(S5_CHUNK)], axis=1) for c in range(n_ch)]
        slabs.append(jnp.concatenate(rows, axis=0))
    return slabs


def _from_slabs(slab_ref, nb):
    n_slabs, rows, _ = slab_ref.shape
    n_ch = rows // nb
    per_chunk = []
    for c in range(n_ch):
        steps = [jnp.concatenate([slab_ref[q, c * nb:(c + 1) * nb, j * S5_SLAB_LANES:(j + 1) * S5_SLAB_LANES]
                                  for q in range(n_slabs)], axis=1) for j in range(S5_CHUNK)]
        per_chunk.append(jnp.stack(steps, axis=0))
    t = jnp.stack(per_chunk, axis=0)
    return jnp.transpose(t, (2, 0, 1, 3)).reshape(nb, n_ch * S5_CHUNK, n_slabs * S5_SLAB_LANES)


def _inproj_kernel(x_ref, g_ref, w_ref, qkv_ref, us_ref, *, q_width, qkv_width, scale, chunk):
    nb, tt, d = x_ref.shape
    hn = _rms(x_ref[...].reshape(nb * tt, d), g_ref[...]).astype(BF16)
    u = jnp.dot(hn, w_ref[:, qkv_width:], preferred_element_type=F32)
    for q, slab in enumerate(_to_slabs(u.reshape(nb, tt, u.shape[1]))):
        us_ref[q] = slab
    for c0 in range(0, qkv_width, chunk):
        p = jnp.dot(hn, w_ref[:, c0:c0 + chunk], preferred_element_type=F32)
        if c0 < q_width:
            p = p * scale
        qkv_ref[:, :, c0:c0 + chunk] = p.astype(BF16).reshape(nb, tt, chunk)


def _inproj(x3, g1, w_in, *, q_width, qkv_width, scale, tt=128):
    nb, s, d = x3.shape
    n_out = w_in.shape[1]
    n_slabs = (n_out - qkv_width) // S5_SLAB_LANES
    kern = functools.partial(_inproj_kernel, q_width=q_width, qkv_width=qkv_width, scale=scale, chunk=512)
    return pl.pallas_call(
        kern,
        grid=(s // tt,),
        in_specs=[
            pl.BlockSpec((nb, tt, d), lambda i: (0, i, 0)),
            pl.BlockSpec((1, d), lambda i: (0, 0)),
            pl.BlockSpec((d, n_out), lambda i: (0, 0)),
        ],
        out_specs=[
            pl.BlockSpec((nb, tt, qkv_width), lambda i: (0, i, 0)),
            pl.BlockSpec((n_slabs, tt // S5_CHUNK * nb, S5_CHUNK * S5_SLAB_LANES), lambda i: (0, i, 0)),
        ],
        out_shape=[
            jax.ShapeDtypeStruct((nb, s, qkv_width), BF16),
            jax.ShapeDtypeStruct((n_slabs, s // S5_CHUNK * nb, S5_CHUNK * S5_SLAB_LANES), F32),
        ],
        compiler_params=pltpu.CompilerParams(
            dimension_semantics=("arbitrary",), vmem_limit_bytes=VMEM_LIMIT),
        name="inproj",
    )(x3, g1, w_in)


POS_LOW_BITS = 3
LOG2_E = 1.4426950408889634
SAFE_LOG2_SCORE = 64.0


def _attn_kernel(lq1_ref, lk1_ref, lq2_ref, lk2_ref, sg_ref, q_ref, k_ref, v_ref, o_ref,
                 kex_ref, qex_ref, diag_ref, kx_ref, vt_ref, s0_ref, s1_ref, s2_ref, s3_ref, e0_ref, e1_ref,
                 *, qb, lambda_init):
    h = pl.program_id(0)
    s_refs, e_refs = (s0_ref, s1_ref, s2_ref, s3_ref), (e0_ref, e1_ref)
    s_len, hv = k_ref.shape[1], k_ref.shape[2]
    dh = hv // 2

    lam = (jnp.exp(jnp.sum(lq1_ref[...] * lk1_ref[...], axis=-1, keepdims=True))
           - jnp.exp(jnp.sum(lq2_ref[...] * lk2_ref[...], axis=-1, keepdims=True))
           + lambda_init)

    lane = lax.broadcasted_iota(jnp.int32, (1, hv), 1)
    data_lanes = (lane < dh, lane >= dh)

    @pl.when(pl.program_id(1) == 0)
    def _():
        slope = jnp.float32(2.0 ** -8.0)
        for hh in range(ATTN_HEADS - 1):
            slope = jnp.where(h == hh, jnp.float32(2.0 ** (-8.0 * (hh + 1) / ATTN_HEADS)), slope)
        c2 = jnp.full((1, 1), slope, F32) * LOG2_E
        c2_parts = []
        rest = c2
        for _ in range(3):
            part = rest.astype(BF16).astype(F32)
            c2_parts.append(part)
            rest = rest - part

        def extras(vals):
            out = 0.0
            for e, val in enumerate(vals):
                out = jnp.where(lane == dh + e, val, out)
            return out

        pos = lax.broadcasted_iota(jnp.int32, (s_len, 1), 0)
        pos_lo = pos & ((1 << POS_LOW_BITS) - 1)
        p_hi, p_lo = (pos - pos_lo).astype(F32), pos_lo.astype(F32)
        kex = extras(c2_parts + c2_parts + [p_hi] * 3 + [p_lo] * 3)
        qex = extras([-p_hi] * 3 + [-p_lo] * 3 + c2_parts + c2_parts)
        kex_ref[0], kex_ref[1] = kex, pltpu.roll(kex, dh, 1)
        qex_ref[0], qex_ref[1] = qex, pltpu.roll(qex, dh, 1)
        off = lax.broadcasted_iota(jnp.int32, (qb, qb), 0) - lax.broadcasted_iota(jnp.int32, (qb, qb), 1)
        diag_ref[...] = jnp.abs(off).astype(F32) * (-c2)

    kf = k_ref[0].astype(F32)
    for c in range(2):
        kx_ref[c] = jnp.where(data_lanes[c], kf, kex_ref[c]).astype(BF16)
    vt_ref[:hv, :] = v_ref[0].astype(F32).T.astype(BF16)
    pad_row = lax.broadcasted_iota(jnp.int32, (vt_ref.shape[0] - hv, s_len), 0)
    vt_ref[hv:, :] = jnp.where(pad_row == 0, 1.0, 0.0).astype(BF16)

    norm2 = lambda t: jnp.max(jnp.sum(t * t, axis=1, keepdims=True), axis=0, keepdims=True)
    small_scores = (norm2(q_ref[0].astype(F32)) * norm2(kf))[0, 0] <= SAFE_LOG2_SCORE ** 2

    n_blocks = s_len // qb

    def score_pieces(n, q_side):
        qi, c = divmod(n, 2)
        lo, hi = qi * qb, (qi + 1) * qb
        if qi not in q_side:
            q_side[qi] = q_ref[0, lo:hi, :].astype(F32)
        qf = q_side[qi]
        aug_t = lambda ex: jnp.where(data_lanes[c], qf, ex).T.astype(BF16)
        dot = lambda r0, r1, qt: jnp.dot(kx_ref[c, r0:r1, :], qt, preferred_element_type=F32)
        q_extra = qex_ref[c, lo:hi, :]
        q_left, q_right = aug_t(q_extra), aug_t(-q_extra)
        for r0 in range(0, s_len, qb):
            if r0 == lo:
                yield lo, hi, dot(lo, hi, aug_t(jnp.zeros_like(q_extra))) + diag_ref[...]
            else:
                yield r0, r0 + qb, dot(r0, r0 + qb, q_left if r0 < lo else q_right)

    def weighted_values(n):
        acc = jnp.dot(vt_ref[...], e_refs[n % 2][...], preferred_element_type=F32)
        return acc[:hv, :] / acc[hv:hv + 1, :]

    def finish_block(qi, out_t):
        o = (out_t[0] - lam * out_t[1]).T
        o = _rms(o, sg_ref[...]) * (1.0 - lambda_init)
        o_ref[0, qi * qb:(qi + 1) * qb, :] = o.astype(o_ref.dtype)

    @pl.when(small_scores)
    def _():
        q_side = {}
        for qi in range(n_blocks):
            out_t = []
            for n in (2 * qi, 2 * qi + 1):
                for r0, r1, sc in score_pieces(n, q_side):
                    e_refs[n % 2][r0:r1, :] = jnp.exp2(sc).astype(BF16)
                out_t.append(weighted_values(n))
            finish_block(qi, out_t)

    @pl.when(jnp.logical_not(small_scores))
    def _():
        q_side = {}

        def scores(n):
            for r0, r1, sc in score_pieces(n, q_side):
                s_refs[n % 4][r0:r1, :] = sc

        def exponentials(n):
            sc = s_refs[n % 4][...]
            e_refs[n % 2][...] = jnp.exp2(sc - jnp.max(sc, axis=0, keepdims=True)).astype(BF16)

        scores(0), scores(1)
        for qi in range(n_blocks):
            if qi + 1 < n_blocks:
                scores(2 * qi + 2), scores(2 * qi + 3)
            out_t = []
            for n in (2 * qi, 2 * qi + 1):
                exponentials(n)
                out_t.append(weighted_values(n))
            finish_block(qi, out_t)


def _attention(qkv3, lq1, lk1, lq2, lk2, subln_g, *, lambda_init, qb=512):
    b, s, w = qkv3.shape
    hv = w // (3 * ATTN_HEADS)
    assert s <= 256 << POS_LOW_BITS and s % qb == 0
    kern = functools.partial(_attn_kernel, qb=qb, lambda_init=lambda_init)
    small = lambda shape: pl.BlockSpec(shape, lambda hi, bi: (0, 0))
    head = lambda off: pl.BlockSpec((1, s, hv), lambda hi, bi: (bi, 0, off + hi))
    return pl.pallas_call(
        kern,
        grid=(ATTN_HEADS, b),
        in_specs=[
            small(lq1.shape), small(lk1.shape), small(lq2.shape), small(lk2.shape), small(subln_g.shape),
            head(0), head(ATTN_HEADS), head(2 * ATTN_HEADS),
        ],
        out_specs=head(0),
        out_shape=jax.ShapeDtypeStruct((b, s, ATTN_HEADS * hv), BF16),
        scratch_shapes=[
            pltpu.VMEM((2, s, hv), F32), pltpu.VMEM((2, s, hv), F32), pltpu.VMEM((qb, qb), F32),
            pltpu.VMEM((2, s, hv), BF16),
            pltpu.VMEM((hv + BF16_SUBLANES, s), BF16),
            pltpu.VMEM((s, qb), F32), pltpu.VMEM((s, qb), F32), pltpu.VMEM((s, qb), F32), pltpu.VMEM((s, qb), F32),
            pltpu.VMEM((s, qb), BF16), pltpu.VMEM((s, qb), BF16),
        ],
        compiler_params=pltpu.CompilerParams(
            dimension_semantics=("arbitrary", "arbitrary"), vmem_limit_bytes=VMEM_LIMIT),
        name="diff_attn",
    )(lq1, lk1, lq2, lk2, subln_g, qkv3, qkv3, qkv3)


S5_CHUNK = 8


def _s5_disc_kernel(lr_ref, li_ref, ldt_ref, bre_ref, bim_ref, cre_ref, cim_ref,
                    pbr_ref, pbi_ref, pcr_ref, pci_ref, akr_ref, aki_ref):
    cmul = lambda ar, ai, xr, xi: (ar * xr - ai * xi, ar * xi + ai * xr)

    lr, li = lr_ref[...], li_ref[...]
    dt = jnp.exp(ldt_ref[...])
    mag = jnp.exp(lr * dt)
    ang = li * dt
    ar, ai = mag * jnp.cos(ang), mag * jnp.sin(ang)
    nr = ar - 1.0
    den = lr * lr + li * li
    cr = (nr * lr + ai * li) / den
    ci = (ai * lr - nr * li) / den
    b_re, b_im = bre_ref[...], bim_ref[...]
    pr, pi = cr * b_re - ci * b_im, cr * b_im + ci * b_re
    kr, ki = ar, ai
    for n in range(pbr_ref.shape[0]):
        pbr_ref[n] = pr
        pbi_ref[n] = pi
        pr, pi = cmul(ar, ai, pr, pi)
        if n > 0:
            kr, ki = cmul(ar, ai, kr, ki)
    akr_ref[...] = kr
    aki_ref[...] = ki

    qr, qi = cre_ref[...], cim_ref[...]
    for n in range(pcr_ref.shape[0]):
        pcr_ref[n] = qr
        pci_ref[n] = -qi
        qr, qi = cmul(ar, ai, qr, qi)


def _s5_discretise(lam_re, lam_im, log_dt, b_re, b_im, c_re, c_im):
    nd, g, p, hc = b_re.shape
    nt = g // S5_SLAB_GROUPS
    rows, width = nd * nt * hc, S5_SLAB_GROUPS * p
    arrange = lambda t: (jnp.broadcast_to(t.astype(F32), (nd, g, hc, p)).reshape(nd, nt, S5_SLAB_GROUPS, hc, p)
                         .transpose(0, 1, 3, 2, 4).reshape(rows, width))
    args = (arrange(lam_re[:, :, None, :]), arrange(lam_im[:, :, None, :]), arrange(log_dt[..., None, None]),
            arrange(jnp.swapaxes(b_re, 2, 3)), arrange(jnp.swapaxes(b_im, 2, 3)), arrange(c_re), arrange(c_im))
    stack = lambda n: jax.ShapeDtypeStruct((n, rows, width), F32)
    one = jax.ShapeDtypeStruct((rows, width), F32)
    pbr, pbi, pcr, pci, akr, aki = pl.pallas_call(
        _s5_disc_kernel,
        out_shape=[stack(S5_CHUNK), stack(S5_CHUNK), stack(S5_CHUNK + 1), stack(S5_CHUNK + 1), one, one],
        name="s5_disc")(*args)
    tiles = lambda t: t.reshape(t.shape[0], nd, nt, hc, width)
    ak = jnp.stack([akr, aki]).reshape(2, nd, nt, hc, width)[:, :, :, 0]
    ak = ak.transpose(2, 1, 0, 3).reshape(nt, 2 * nd, width)
    return tiles(pbr), tiles(pbi), tiles(pcr), tiles(pci), ak


def _s5_weights_kernel(pbr_ref, pbi_ref, pcr_ref, pci_ref, win_ref, wcat_ref):
    chunk, nd, hc, half = pbr_ref.shape[0], pbr_ref.shape[1], pbr_ref.shape[3], pbr_ref.shape[4]
    blk = S5_SLAB_GROUPS * hc
    p = half // S5_SLAB_GROUPS
    same_group = (lax.broadcasted_iota(jnp.int32, (blk, half), 0) // hc
                  == lax.broadcasted_iota(jnp.int32, (blk, half), 1) // p)

    def expand(re, im):
        grid = lambda t: jnp.where(same_group, jnp.tile(t, (S5_SLAB_GROUPS, 1)), 0.0)
        return jnp.concatenate([grid(re), grid(im)], axis=1)

    def split_dot(a, b):
        a_hi, b_hi = a.astype(BF16), b.astype(BF16)
        a_lo, b_lo = (a - a_hi.astype(F32)).astype(BF16), (b - b_hi.astype(F32)).astype(BF16)
        return jnp.dot(jnp.concatenate([a_hi, a_hi, a_lo], axis=1), jnp.concatenate([b_hi, b_lo, b_hi], axis=0),
                       preferred_element_type=F32)

    n_state = 2 * half
    for o in range(pbr_ref.shape[2]):
        wall = [[expand(pbr_ref[n, d, o], pbi_ref[n, d, o]) for n in range(chunk)] for d in range(nd)]
        mall = [[expand(pcr_ref[n, d, o], pci_ref[n, d, o]).T for n in range(chunk + 1)] for d in range(nd)]

        for d in range(nd):
            for j in range(chunk):
                n_in = chunk - 1 - j if d == 0 else j
                win_ref[d, o, j * blk:(j + 1) * blk, :] = wall[d][n_in].astype(win_ref.dtype)
            n_out = [j + 1 if d == 0 else chunk - j for j in range(chunk)]
            r0 = chunk * blk + d * n_state
            wcat_ref[o, r0:r0 + n_state, :] = jnp.concatenate([mall[d][n] for n in n_out],
                                                              axis=1).astype(wcat_ref.dtype)

        resp = [[split_dot(wall[d][n], mall[d][0]) for n in range(chunk)] for d in range(nd)]
        both = resp[0][0] + resp[1][0]
        for j in range(chunk):
            row = [resp[0][jp - j] if jp > j else resp[1][j - jp] if jp < j else both for jp in range(chunk)]
            wcat_ref[o, j * blk:(j + 1) * blk, :] = jnp.concatenate(row, axis=1).astype(wcat_ref.dtype)


def _s5_chunk_weights(pbr, pbi, pcr, pci, *, slabs_per_step=4):
    chunk, nd, nt, hc, half = pbr.shape
    width, n_state = chunk * S5_SLAB_GROUPS * hc, 2 * half
    sps = slabs_per_step
    spec = lambda t: pl.BlockSpec((t.shape[0], nd, sps, hc, half), lambda o: (0, 0, o, 0, 0))
    return pl.pallas_call(
        _s5_weights_kernel,
        grid=(nt // sps,),
        in_specs=[spec(pbr), spec(pbi), spec(pcr), spec(pci)],
        out_specs=[pl.BlockSpec((nd, sps, width, n_state), lambda o: (0, o, 0, 0)),
                   pl.BlockSpec((sps, width + nd * n_state, width), lambda o: (o, 0, 0))],
        out_shape=[jax.ShapeDtypeStruct((nd, nt, width, n_state), BF16),
                   jax.ShapeDtypeStruct((nt, width + nd * n_state, width), BF16)],
        compiler_params=pltpu.CompilerParams(dimension_semantics=("arbitrary",), vmem_limit_bytes=VMEM_LIMIT),
        name="s5_weights",
    )(pbr, pbi, pcr, pci)


def _s5_chunk_kernel(u_ref, win_ref, wcat_ref, ak_ref, dsk_ref, y_ref, xf_ref, xb_ref, s0_ref, s1_ref, *, nb, rb):
    n_rows, half = u_ref.shape[1], ak_ref.shape[2]
    n_blk, pair = n_rows // rb, 2 * nb
    s_refs = (s0_ref, s1_ref)
    order = {0: list(range(n_blk)), 1: list(reversed(range(n_blk)))}

    def increments(d, i):
        blk = order[d][i]
        s_refs[i % 2][...] = jnp.dot(u_ref[0, blk * rb:(blk + 1) * rb, :].astype(BF16), win_ref[d, 0],
                                     preferred_element_type=F32)

    def scan_block(d, i, x):
        blk, s_ref, x_ref = order[d][i], s_refs[i % 2], (xf_ref, xb_ref)[d]
        ar = jnp.broadcast_to(ak_ref[0, 2 * d:2 * d + 1, :], (nb, half))
        ai = jnp.broadcast_to(ak_ref[0, 2 * d + 1:2 * d + 2, :], (nb, half))

        def advance(x, rows):
            return (ar * x[0] - ai * x[1] + s_ref[rows, :half], ar * x[1] + ai * x[0] + s_ref[rows, half:])

        def body(it, x):
            r0 = pl.multiple_of((it if d == 0 else rb // pair - 1 - it) * pair, pair)
            early, late = pl.ds(r0, nb), pl.ds(r0 + nb, nb)
            x1 = advance(x, early if d == 0 else late)
            x2 = advance(x1, late if d == 0 else early)
            enter_early, enter_late = (x, x1) if d == 0 else (x1, x)
            dst = pl.ds(pl.multiple_of(blk * rb + r0, pair), pair)
            x_ref[dst, :half] = jnp.concatenate([enter_early[0], enter_late[0]], axis=0).astype(BF16)
            x_ref[dst, half:] = jnp.concatenate([enter_early[1], enter_late[1]], axis=0).astype(BF16)
            return x2

        return lax.fori_loop(0, rb // pair, body, x, unroll=True)

    zero = jnp.zeros((nb, half), F32)
    for d in range(2):
        x = (zero, zero)
        increments(d, 0)
        for i in range(n_blk):
            if i + 1 < n_blk:
                increments(d, i + 1)
            x = scan_block(d, i, x)

    for blk in range(n_blk):
        rows = slice(blk * rb, (blk + 1) * rb)
        u = u_ref[0, rows, :]
        lhs = jnp.concatenate([u.astype(BF16), xf_ref[rows, :], xb_ref[rows, :]], axis=1)
        y_ref[0, rows, :] = jnp.dot(lhs, wcat_ref[0], preferred_element_type=F32) + dsk_ref[0] * u


def _s5_chunk_scan(u_tiles, win, wcat, ak, dsk, *, nb, rb=256):
    nt, n_rows, width = u_tiles.shape
    n_state = win.shape[3]
    kern = functools.partial(_s5_chunk_kernel, nb=nb, rb=rb)
    return pl.pallas_call(
        kern,
        grid=(nt,),
        in_specs=[
            pl.BlockSpec((1, n_rows, width), lambda o: (o, 0, 0)),
            pl.BlockSpec((win.shape[0], 1, width, n_state), lambda o: (0, o, 0, 0)),
            pl.BlockSpec((1,) + wcat.shape[1:], lambda o: (o, 0, 0)),
            pl.BlockSpec((1,) + ak.shape[1:], lambda o: (o, 0, 0)),
            pl.BlockSpec((1,) + dsk.shape[1:], lambda o: (o, 0, 0)),
        ],
        out_specs=pl.BlockSpec((1, n_rows, width), lambda o: (o, 0, 0)),
        out_shape=jax.ShapeDtypeStruct(u_tiles.shape, F32),
        scratch_shapes=[
            pltpu.VMEM((n_rows, n_state), BF16), pltpu.VMEM((n_rows, n_state), BF16),
            pltpu.VMEM((rb, n_state), F32), pltpu.VMEM((rb, n_state), F32),
        ],
        compiler_params=pltpu.CompilerParams(
            dimension_semantics=("arbitrary",), vmem_limit_bytes=VMEM_LIMIT),
        name="s5_scan",
    )(u_tiles, win, wcat, ak, dsk)


def _post_kernel(x_ref, att_ref, ys_first_ref, ys_next_ref, wg_ref, bg_ref, wo_ref, g2_ref,
                 w1_ref, w2_ref, gf_ref, o_ref, y_ref, *, ff_chunk, final_norm):
    nb, tt, d = x_ref.shape
    aw = att_ref.shape[2]
    unslab = lambda ref: _from_slabs(ref, nb).reshape(nb * tt, -1)

    @pl.when(pl.program_id(0) == 0)
    def _():
        y_ref[...] = unslab(ys_first_ref)

    y = y_ref[...]
    z = 0.5 * y * (1.0 + lax.erf(y * (1.0 / math.sqrt(2.0))))
    gate = jax.nn.sigmoid(jnp.dot(z.astype(BF16), wg_ref[...], preferred_element_type=F32) + bg_ref[...])
    ssm = (z * gate).astype(BF16)
    h = (x_ref[...].reshape(nb * tt, d)
         + jnp.dot(att_ref[...].reshape(nb * tt, aw), wo_ref[:aw, :], preferred_element_type=F32)
         + jnp.dot(ssm, wo_ref[aw:, :], preferred_element_type=F32))
    hn2 = _rms(h, g2_ref[...]).astype(BF16)
    y_ref[...] = unslab(ys_next_ref)
    mlp = jnp.zeros_like(h)
    for f0 in range(0, w1_ref.shape[1], ff_chunk):
        a = jnp.maximum(jnp.dot(hn2, w1_ref[:, f0:f0 + ff_chunk], preferred_element_type=F32), 0.0)
        mlp = mlp + jnp.dot((a * a).astype(BF16), w2_ref[f0:f0 + ff_chunk, :], preferred_element_type=F32)
    h = h + mlp
    o_ref[...] = (_rms(h, gf_ref[...]) if final_norm else h).reshape(nb, tt, d)


def _post(x3, att, y_slabs, w_glu, b_glu, w_out, g2, w1, w2, gf, *, final_norm, tt=64):
    nb, s, d = x3.shape
    aw = att.shape[2]
    n_slabs, _, slab_w = y_slabs.shape
    rows = lambda width: pl.BlockSpec((nb, tt, width), lambda i: (0, i, 0))
    const = lambda arr: pl.BlockSpec(arr.shape, lambda i: (0, 0))
    kern = functools.partial(_post_kernel, ff_chunk=1024, final_norm=final_norm)
    n_steps = s // tt
    slab_block = (n_slabs, tt // S5_CHUNK * nb, slab_w)
    return pl.pallas_call(
        kern,
        grid=(n_steps,),
        in_specs=[rows(d), rows(aw),
                  pl.BlockSpec(slab_block, lambda i: (0, 0, 0)),
                  pl.BlockSpec(slab_block, lambda i: (0, jnp.minimum(i + 1, n_steps - 1), 0)),
                  const(w_glu), const(b_glu), const(w_out), const(g2), const(w1), const(w2), const(gf)],
        out_specs=rows(d),
        out_shape=jax.ShapeDtypeStruct((nb, s, d), F32),
        scratch_shapes=[pltpu.VMEM((nb * tt, n_slabs * S5_SLAB_LANES), F32)],
        compiler_params=pltpu.CompilerParams(
            dimension_semantics=("arbitrary",), vmem_limit_bytes=VMEM_LIMIT),
        name="post_mlp",
    )(x3, att, y_slabs, y_slabs, w_glu, b_glu, w_out, g2, w1, w2, gf)


def kernel(x, norm1_g, w_in, lam_q1, lam_k1, lam_q2, lam_k2, subln_g, ssm_lam_re, ssm_lam_im, ssm_log_dt,
           ssm_b_re, ssm_b_im, ssm_c_re, ssm_c_im, ssm_d, w_glu, b_glu, w_out, norm2_g, w_mlp1, w_mlp2,
           final_g):
    bsz, s, d = x.shape
    depth = w_in.shape[0]
    ssm_w = ssm_d.shape[-1]
    attn_w = w_out.shape[1] - ssm_w
    qk_dim = attn_w // ATTN_HEADS // 2
    scale = float(qk_dim) ** -0.5 * LOG2_E

    h3 = x.astype(F32)
    for l in range(depth):
        lambda_init = 0.8 - 0.6 * math.exp(-0.3 * l)
        qkv, u_slabs = _inproj(h3, norm1_g[l][None].astype(F32), w_in[l].astype(BF16),
                               q_width=attn_w, qkv_width=3 * attn_w, scale=scale)

        row = lambda t: t[l][None].astype(F32)
        att = _attention(qkv, row(lam_q1), row(lam_k1), row(lam_q2), row(lam_k2), row(subln_g),
                         lambda_init=lambda_init)

        pbr, pbi, pcr, pci, ak_slabs = _s5_discretise(ssm_lam_re[l], ssm_lam_im[l], ssm_log_dt[l],
                                                      ssm_b_re[l], ssm_b_im[l], ssm_c_re[l], ssm_c_im[l])
        win, wcat = _s5_chunk_weights(pbr, pbi, pcr, pci)
        dsk_slabs = jnp.tile(ssm_d[l].astype(F32).reshape(-1, 1, S5_SLAB_LANES), (1, 1, S5_CHUNK))
        y_slabs = _s5_chunk_scan(u_slabs, win, wcat, ak_slabs, dsk_slabs, nb=bsz)

        h3 = _post(h3, att, y_slabs, w_glu[l].astype(BF16), row(b_glu), w_out[l].astype(BF16), row(norm2_g),
                   w_mlp1[l].astype(BF16), w_mlp2[l].astype(BF16), final_g[None].astype(F32),
                   final_norm=(l == depth - 1))
    return h3.astype(x.dtype)
```

```python
import functools
import math

import jax
import jax.numpy as jnp
from jax import lax
from jax.experimental import pallas as pl
from jax.experimental.pallas import tpu as pltpu

F32 = jnp.float32
BF16 = jnp.bfloat16

EPS = 1e-6
ATTN_HEADS = 4
SSM_GROUP = 16
SSM_STATE = 64
LANES = 128
BF16_SUBLANES = 16
S5_SLAB_GROUPS = 2
S5_SLAB_LANES = S5_SLAB_GROUPS * SSM_GROUP
VMEM_LIMIT = 56 * 1024 * 1024


def _rms(x, g):
    return x * lax.rsqrt(jnp.mean(x * x, axis=-1, keepdims=True) + EPS) * g


def _to_slabs(u3):
    nb, tt, w = u3.shape
    n_ch = tt // S5_CHUNK
    t = jnp.transpose(u3.reshape(nb, n_ch, S5_CHUNK, w), (1, 2, 0, 3))
    slabs = []
    for q in range(w // S5_SLAB_LANES):
        lanes = slice(q * S5_SLAB_LANES, (q + 1) * S5_SLAB_LANES)
        rows = [jnp.concatenate([t[c, j][:, lanes] for j in range(S5_CHUNK)], axis=1) for c in range(n_ch)]
        slabs.append(jnp.concatenate(rows, axis=0))
    return slabs


def _from_slabs(slab_ref, nb):
    n_slabs, rows, _ = slab_ref.shape
    n_ch = rows // nb
    per_chunk = []
    for c in range(n_ch):
        steps = [jnp.concatenate([slab_ref[q, c * nb:(c + 1) * nb, j * S5_SLAB_LANES:(j + 1) * S5_SLAB_LANES]
                                  for q in range(n_slabs)], axis=1) for j in range(S5_CHUNK)]
        per_chunk.append(jnp.stack(steps, axis=0))
    t = jnp.stack(per_chunk, axis=0)
    return jnp.transpose(t, (2, 0, 1, 3)).reshape(nb, n_ch * S5_CHUNK, n_slabs * S5_SLAB_LANES)


def _inproj_kernel(x_ref, g_ref, w_ref, qk_ref, vt_ref, us_ref, *, q_width, scale, chunk):
    nb, tt, d = x_ref.shape
    qk_width, n_heads, hv = qk_ref.shape[2], vt_ref.shape[1], vt_ref.shape[2]
    v_width = n_heads * hv
    hn = _rms(x_ref[...].reshape(nb * tt, d), g_ref[...]).astype(BF16)
    u = jnp.dot(hn, w_ref[:, qk_width + v_width:], preferred_element_type=F32)
    for q, slab in enumerate(_to_slabs(u.reshape(nb, tt, u.shape[1]))):
        us_ref[q] = slab
    v = jnp.dot(hn, w_ref[:, qk_width:qk_width + v_width], preferred_element_type=F32)
    for b in range(nb):
        for hh in range(n_heads):
            vt_ref[b, hh] = v[b * tt:(b + 1) * tt, hh * hv:(hh + 1) * hv].T.astype(BF16)
    for c0 in range(0, qk_width, chunk):
        p = jnp.dot(hn, w_ref[:, c0:c0 + chunk], preferred_element_type=F32)
        if c0 < q_width:
            p = p * scale
        qk_ref[:, :, c0:c0 + chunk] = p.astype(BF16).reshape(nb, tt, chunk)


def _inproj(x3, g1, w_in, *, q_width, n_heads, ssm_width, scale, tt=128):
    nb, s, d = x3.shape
    n_out = w_in.shape[1]
    v_width = n_out - 2 * q_width - ssm_width
    n_slabs = ssm_width // S5_SLAB_LANES
    kern = functools.partial(_inproj_kernel, q_width=q_width, scale=scale, chunk=512)
    return pl.pallas_call(
        kern,
        grid=(s // tt,),
        in_specs=[
            pl.BlockSpec((nb, tt, d), lambda i: (0, i, 0)),
            pl.BlockSpec((1, d), lambda i: (0, 0)),
            pl.BlockSpec((d, n_out), lambda i: (0, 0)),
        ],
        out_specs=[
            pl.BlockSpec((nb, tt, 2 * q_width), lambda i: (0, i, 0)),
            pl.BlockSpec((nb, n_heads, v_width // n_heads, tt), lambda i: (0, 0, 0, i)),
            pl.BlockSpec((n_slabs, tt // S5_CHUNK * nb, S5_CHUNK * S5_SLAB_LANES), lambda i: (0, i, 0)),
        ],
        out_shape=[
            jax.ShapeDtypeStruct((nb, s, 2 * q_width), BF16),
            jax.ShapeDtypeStruct((nb, n_heads, v_width // n_heads, s), BF16),
            jax.ShapeDtypeStruct((n_slabs, s // S5_CHUNK * nb, S5_CHUNK * S5_SLAB_LANES), F32),
        ],
        compiler_params=pltpu.CompilerParams(
            dimension_semantics=("arbitrary",), vmem_limit_bytes=VMEM_LIMIT),
        name="inproj",
    )(x3, g1, w_in)


POS_LOW_BITS = 3
LOG2_E = 1.4426950408889634
SAFE_LOG2_SCORE = 96.0
SAFE_VALUE = 2.0 ** 16
NORM_FOLD = 4


def _attn_kernel(lq1_ref, lk1_ref, lq2_ref, lk2_ref, sg_ref, q_ref, k_ref, vin_ref, o_ref,
                 kex_ref, qex_ref, diag_ref, kx_ref, vt_ref, s0_ref, s1_ref, s2_ref, s3_ref, e0_ref, e1_ref,
                 *, qb, lambda_init):
    h = pl.program_id(0)
    s_refs, e_refs = (s0_ref, s1_ref, s2_ref, s3_ref), (e0_ref, e1_ref)
    s_len, hv = k_ref.shape[1], k_ref.shape[2]
    dh = hv // 2

    lam = (jnp.exp(jnp.sum(lq1_ref[...] * lk1_ref[...], axis=-1, keepdims=True))
           - jnp.exp(jnp.sum(lq2_ref[...] * lk2_ref[...], axis=-1, keepdims=True))
           + lambda_init)

    lane = lax.broadcasted_iota(jnp.int32, (1, hv), 1)
    data_lanes = (lane < dh, lane >= dh)

    @pl.when(pl.program_id(1) == 0)
    def _():
        slope = jnp.float32(2.0 ** -8.0)
        for hh in range(ATTN_HEADS - 1):
            slope = jnp.where(h == hh, jnp.float32(2.0 ** (-8.0 * (hh + 1) / ATTN_HEADS)), slope)
        c2 = jnp.full((1, 1), slope, F32) * LOG2_E
        c2_parts = []
        rest = c2
        for _ in range(3):
            part = rest.astype(BF16).astype(F32)
            c2_parts.append(part)
            rest = rest - part

        def extras(vals):
            out = 0.0
            for e, val in enumerate(vals):
                out = jnp.where(lane == dh + e, val, out)
            return out

        pos = lax.broadcasted_iota(jnp.int32, (s_len, 1), 0)
        pos_lo = pos & ((1 << POS_LOW_BITS) - 1)
        p_hi, p_lo = (pos - pos_lo).astype(F32), pos_lo.astype(F32)
        kex = extras(c2_parts + c2_parts + [p_hi] * 3 + [p_lo] * 3)
        qex = extras([-p_hi] * 3 + [-p_lo] * 3 + c2_parts + c2_parts)
        kex_ref[0], kex_ref[1] = kex, pltpu.roll(kex, dh, 1)
        qex_ref[0], qex_ref[1] = qex, pltpu.roll(qex, dh, 1)
        off = lax.broadcasted_iota(jnp.int32, (qb, qb), 0) - lax.broadcasted_iota(jnp.int32, (qb, qb), 1)
        diag_ref[...] = jnp.abs(off).astype(F32) * (-c2)

    kf = k_ref[0].astype(F32)
    for c in range(2):
        kx_ref[c] = jnp.where(data_lanes[c], kf, kex_ref[c]).astype(BF16)
    v_t = vin_ref[0, 0]
    vt_ref[:hv, :] = v_t
    pad_row = lax.broadcasted_iota(jnp.int32, (vt_ref.shape[0] - hv, s_len), 0)
    vt_ref[hv:, :] = jnp.where(pad_row == 0, 1.0, 0.0).astype(BF16)

    def norm2_bound(t):
        sq = (t * t).reshape(NORM_FOLD, s_len // NORM_FOLD, hv)
        return jnp.max(jnp.sum(jnp.max(sq, axis=0), axis=1, keepdims=True), axis=0, keepdims=True)

    v_max = jnp.max(jnp.max(jnp.abs(v_t.astype(F32)), axis=1, keepdims=True), axis=0, keepdims=True)
    small_scores = jnp.logical_and(
        (norm2_bound(q_ref[0].astype(F32)) * norm2_bound(kf))[0, 0] <= SAFE_LOG2_SCORE ** 2,
        v_max[0, 0] <= SAFE_VALUE)

    n_blocks = s_len // qb

    def score_pieces(n, q_side):
        qi, c = divmod(n, 2)
        lo, hi = qi * qb, (qi + 1) * qb
        if qi not in q_side:
            q_side[qi] = q_ref[0, lo:hi, :].astype(F32)
        qf = q_side[qi]
        aug_t = lambda ex: jnp.where(data_lanes[c], qf, ex).T.astype(BF16)
        dot = lambda r0, r1, qt: jnp.dot(kx_ref[c, r0:r1, :], qt, preferred_element_type=F32)
        q_extra = qex_ref[c, lo:hi, :]
        q_left, q_right = aug_t(q_extra), aug_t(-q_extra)
        for r0 in range(0, s_len, qb):
            if r0 == lo:
                yield lo, hi, dot(lo, hi, aug_t(jnp.zeros_like(q_extra))) + diag_ref[...]
            else:
                yield r0, r0 + qb, dot(r0, r0 + qb, q_left if r0 < lo else q_right)

    def weighted_values(n):
        acc = jnp.dot(vt_ref[...], e_refs[n % 2][...], preferred_element_type=F32)
        return acc[:hv, :] / acc[hv:hv + 1, :]

    def finish_block(qi, out_t):
        o = (out_t[0] - lam * out_t[1]).T
        o = _rms(o, sg_ref[...]) * (1.0 - lambda_init)
        o_ref[0, qi * qb:(qi + 1) * qb, :] = o.astype(o_ref.dtype)

    @pl.when(small_scores)
    def _():
        q_side = {}
        for qi in range(n_blocks):
            out_t = []
            for n in (2 * qi, 2 * qi + 1):
                for r0, r1, sc in score_pieces(n, q_side):
                    e_refs[n % 2][r0:r1, :] = jnp.exp2(sc).astype(BF16)
                out_t.append(weighted_values(n))
            finish_block(qi, out_t)

    @pl.when(jnp.logical_not(small_scores))
    def _():
        q_side = {}

        def scores(n):
            for r0, r1, sc in score_pieces(n, q_side):
                s_refs[n % 4][r0:r1, :] = sc

        def exponentials(n):
            sc = s_refs[n % 4][...]
            e_refs[n % 2][...] = jnp.exp2(sc - jnp.max(sc, axis=0, keepdims=True)).astype(BF16)

        scores(0), scores(1)
        for qi in range(n_blocks):
            if qi + 1 < n_blocks:
                scores(2 * qi + 2), scores(2 * qi + 3)
            out_t = []
            for n in (2 * qi, 2 * qi + 1):
                exponentials(n)
                out_t.append(weighted_values(n))
            finish_block(qi, out_t)


def _attention(qk3, v_t, lq1, lk1, lq2, lk2, subln_g, *, lambda_init, qb=512):
    b, s, w = qk3.shape
    hv = w // (2 * ATTN_HEADS)
    assert s <= 256 << POS_LOW_BITS and s % qb == 0
    kern = functools.partial(_attn_kernel, qb=qb, lambda_init=lambda_init)
    small = lambda shape: pl.BlockSpec(shape, lambda hi, bi: (0, 0))
    head = lambda off: pl.BlockSpec((1, s, hv), lambda hi, bi: (bi, 0, off + hi))
    return pl.pallas_call(
        kern,
        grid=(ATTN_HEADS, b),
        in_specs=[
            small(lq1.shape), small(lk1.shape), small(lq2.shape), small(lk2.shape), small(subln_g.shape),
            head(0), head(ATTN_HEADS), pl.BlockSpec((1, 1, hv, s), lambda hi, bi: (bi, hi, 0, 0)),
        ],
        out_specs=head(0),
        out_shape=jax.ShapeDtypeStruct((b, s, ATTN_HEADS * hv), BF16),
        scratch_shapes=[
            pltpu.VMEM((2, s, hv), F32), pltpu.VMEM((2, s, hv), F32), pltpu.VMEM((qb, qb), F32),
            pltpu.VMEM((2, s, hv), BF16),
            pltpu.VMEM((hv + BF16_SUBLANES, s), BF16),
            pltpu.VMEM((s, qb), F32), pltpu.VMEM((s, qb), F32), pltpu.VMEM((s, qb), F32), pltpu.VMEM((s, qb), F32),
            pltpu.VMEM((s, qb), BF16), pltpu.VMEM((s, qb), BF16),
        ],
        compiler_params=pltpu.CompilerParams(
            dimension_semantics=("arbitrary", "arbitrary"), vmem_limit_bytes=VMEM_LIMIT),
        name="diff_attn",
    )(lq1, lk1, lq2, lk2, subln_g, qk3, qk3, v_t)


S5_CHUNK = 8


def _s5_disc_kernel(lr_ref, li_ref, ldt_ref, bre_ref, bim_ref, cre_ref, cim_ref,
                    pbr_ref, pbi_ref, pcr_ref, pci_ref, akr_ref, aki_ref):
    cmul = lambda ar, ai, xr, xi: (ar * xr - ai * xi, ar * xi + ai * xr)

    lr, li = lr_ref[...], li_ref[...]
    dt = jnp.exp(ldt_ref[...])
    mag = jnp.exp(lr * dt)
    ang = li * dt
    ar, ai = mag * jnp.cos(ang), mag * jnp.sin(ang)
    nr = ar - 1.0
    den = lr * lr + li * li
    cr = (nr * lr + ai * li) / den
    ci = (ai * lr - nr * li) / den
    b_re, b_im = bre_ref[...], bim_ref[...]
    pr, pi = cr * b_re - ci * b_im, cr * b_im + ci * b_re
    kr, ki = ar, ai
    for n in range(pbr_ref.shape[0]):
        pbr_ref[n] = pr
        pbi_ref[n] = pi
        pr, pi = cmul(ar, ai, pr, pi)
        if n > 0:
            kr, ki = cmul(ar, ai, kr, ki)
    akr_ref[...] = kr
    aki_ref[...] = ki

    qr, qi = cre_ref[...], cim_ref[...]
    for n in range(pcr_ref.shape[0]):
        pcr_ref[n] = qr
        pci_ref[n] = -qi
        qr, qi = cmul(ar, ai, qr, qi)


def _s5_discretise(lam_re, lam_im, log_dt, b_re, b_im, c_re, c_im):
    nd, g, p, hc = b_re.shape
    nt = g // S5_SLAB_GROUPS
    rows, width = nd * nt * hc, S5_SLAB_GROUPS * p
    arrange = lambda t: (jnp.broadcast_to(t.astype(F32), (nd, g, hc, p)).reshape(nd, nt, S5_SLAB_GROUPS, hc, p)
                         .transpose(0, 1, 3, 2, 4).reshape(rows, width))
    args = (arrange(lam_re[:, :, None, :]), arrange(lam_im[:, :, None, :]), arrange(log_dt[..., None, None]),
            arrange(jnp.swapaxes(b_re, 2, 3)), arrange(jnp.swapaxes(b_im, 2, 3)), arrange(c_re), arrange(c_im))
    stack = lambda n: jax.ShapeDtypeStruct((n, rows, width), F32)
    one = jax.ShapeDtypeStruct((rows, width), F32)
    pbr, pbi, pcr, pci, akr, aki = pl.pallas_call(
        _s5_disc_kernel,
        out_shape=[stack(S5_CHUNK), stack(S5_CHUNK), stack(S5_CHUNK + 1), stack(S5_CHUNK + 1), one, one],
        name="s5_disc")(*args)
    tiles = lambda t: t.reshape(t.shape[0], nd, nt, hc, width)
    ak = jnp.stack([akr, aki]).reshape(2, nd, nt, hc, width)[:, :, :, 0]
    ak = ak.transpose(2, 1, 0, 3).reshape(nt, 2 * nd, width)
    return tiles(pbr), tiles(pbi), tiles(pcr), tiles(pci), ak


def _s5_weights_kernel(pbr_ref, pbi_ref, pcr_ref, pci_ref, win_ref, wcat_ref):
    chunk, nd, hc, half = pbr_ref.shape[0], pbr_ref.shape[1], pbr_ref.shape[3], pbr_ref.shape[4]
    blk = S5_SLAB_GROUPS * hc
    p = half // S5_SLAB_GROUPS
    same_group = (lax.broadcasted_iota(jnp.int32, (blk, half), 0) // hc
                  == lax.broadcasted_iota(jnp.int32, (blk, half), 1) // p)

    def expand(re, im):
        grid = lambda t: jnp.where(same_group, jnp.tile(t, (S5_SLAB_GROUPS, 1)), 0.0)
        return jnp.concatenate([grid(re), grid(im)], axis=1)

    def split_dot(a, b):
        a_hi, b_hi = a.astype(BF16), b.astype(BF16)
        a_lo, b_lo = (a - a_hi.astype(F32)).astype(BF16), (b - b_hi.astype(F32)).astype(BF16)
        return jnp.dot(jnp.concatenate([a_hi, a_hi, a_lo], axis=1), jnp.concatenate([b_hi, b_lo, b_hi], axis=0),
                       preferred_element_type=F32)

    n_state = 2 * half
    for o in range(pbr_ref.shape[2]):
        wall = [[expand(pbr_ref[n, d, o], pbi_ref[n, d, o]) for n in range(chunk)] for d in range(nd)]
        mall = [[expand(pcr_ref[n, d, o], pci_ref[n, d, o]).T for n in range(chunk + 1)] for d in range(nd)]

        for d in range(nd):
            for j in range(chunk):
                n_in = chunk - 1 - j if d == 0 else j
                win_ref[d, o, j * blk:(j + 1) * blk, :] = wall[d][n_in].astype(win_ref.dtype)
            n_out = [j + 1 if d == 0 else chunk - j for j in range(chunk)]
            r0 = chunk * blk + d * n_state
            wcat_ref[o, r0:r0 + n_state, :] = jnp.concatenate([mall[d][n] for n in n_out],
                                                              axis=1).astype(wcat_ref.dtype)

        resp = [[split_dot(wall[d][n], mall[d][0]) for n in range(chunk)] for d in range(nd)]
        both = resp[0][0] + resp[1][0]
        for j in range(chunk):
            row = [resp[0][jp - j] if jp > j else resp[1][j - jp] if jp < j else both for jp in range(chunk)]
            wcat_ref[o, j * blk:(j + 1) * blk, :] = jnp.concatenate(row, axis=1).astype(wcat_ref.dtype)


def _s5_chunk_weights(pbr, pbi, pcr, pci, *, slabs_per_step=4):
    chunk, nd, nt, hc, half = pbr.shape
    width, n_state = chunk * S5_SLAB_GROUPS * hc, 2 * half
    sps = slabs_per_step
    spec = lambda t: pl.BlockSpec((t.shape[0], nd, sps, hc, half), lambda o: (0, 0, o, 0, 0))
    return pl.pallas_call(
        _s5_weights_kernel,
        grid=(nt // sps,),
        in_specs=[spec(pbr), spec(pbi), spec(pcr), spec(pci)],
        out_specs=[pl.BlockSpec((nd, sps, width, n_state), lambda o: (0, o, 0, 0)),
                   pl.BlockSpec((sps, width + nd * n_state, width), lambda o: (o, 0, 0))],
        out_shape=[jax.ShapeDtypeStruct((nd, nt, width, n_state), BF16),
                   jax.ShapeDtypeStruct((nt, width + nd * n_state, width), BF16)],
        compiler_params=pltpu.CompilerParams(dimension_semantics=("arbitrary",), vmem_limit_bytes=VMEM_LIMIT),
        name="s5_weights",
    )(pbr, pbi, pcr, pci)


def _s5_chunk_kernel(u_ref, win_ref, wcat_ref, ak_ref, dsk_ref, y_ref, xf_ref, xb_ref, s0_ref, s1_ref, *, nb, rb):
    n_sl, n_rows, half = u_ref.shape[0], u_ref.shape[1], ak_ref.shape[2]
    n_blk, pair = n_rows // rb, 2 * nb
    s_refs = (s0_ref, s1_ref)
    order = {0: list(range(n_blk)), 1: list(reversed(range(n_blk)))}

    def increments(d, i):
        blk = order[d][i]
        for sl in range(n_sl):
            s_refs[i % 2][sl] = jnp.dot(u_ref[sl, blk * rb:(blk + 1) * rb, :].astype(BF16), win_ref[d, sl],
                                        preferred_element_type=F32)

    def scan_block(d, i, xs):
        blk, s_ref, x_ref = order[d][i], s_refs[i % 2], (xf_ref, xb_ref)[d]
        ak = [(jnp.broadcast_to(ak_ref[sl, 2 * d:2 * d + 1, :], (nb, half)),
               jnp.broadcast_to(ak_ref[sl, 2 * d + 1:2 * d + 2, :], (nb, half))) for sl in range(n_sl)]

        def advance(sl, x, rows):
            ar, ai = ak[sl]
            return (ar * x[0] - ai * x[1] + s_ref[sl, rows, :half], ar * x[1] + ai * x[0] + s_ref[sl, rows, half:])

        def body(it, xs):
            r0 = pl.multiple_of((it if d == 0 else rb // pair - 1 - it) * pair, pair)
            early, late = pl.ds(r0, nb), pl.ds(r0 + nb, nb)
            dst = pl.ds(pl.multiple_of(blk * rb + r0, pair), pair)
            out = []
            for sl, x in enumerate(xs):
                x1 = advance(sl, x, early if d == 0 else late)
                x2 = advance(sl, x1, late if d == 0 else early)
                enter_early, enter_late = (x, x1) if d == 0 else (x1, x)
                x_ref[sl, dst, :half] = jnp.concatenate([enter_early[0], enter_late[0]], axis=0).astype(BF16)
                x_ref[sl, dst, half:] = jnp.concatenate([enter_early[1], enter_late[1]], axis=0).astype(BF16)
                out.append(x2)
            return tuple(out)

        return lax.fori_loop(0, rb // pair, body, xs, unroll=True)

    zero = jnp.zeros((nb, half), F32)
    for d in range(2):
        xs = ((zero, zero),) * n_sl
        increments(d, 0)
        for i in range(n_blk):
            if i + 1 < n_blk:
                increments(d, i + 1)
            xs = scan_block(d, i, xs)

    for blk in range(n_blk):
        rows = slice(blk * rb, (blk + 1) * rb)
        for sl in range(n_sl):
            u = u_ref[sl, rows, :]
            lhs = jnp.concatenate([u.astype(BF16), xf_ref[sl, rows, :], xb_ref[sl, rows, :]], axis=1)
            y_ref[sl, rows, :] = jnp.dot(lhs, wcat_ref[sl], preferred_element_type=F32) + dsk_ref[sl] * u


def _s5_chunk_scan(u_tiles, win, wcat, ak, dsk, *, nb, rb=256, slabs_per_step=4):
    nt, n_rows, width = u_tiles.shape
    n_state = win.shape[3]
    sps = slabs_per_step
    kern = functools.partial(_s5_chunk_kernel, nb=nb, rb=rb)
    return pl.pallas_call(
        kern,
        grid=(nt // sps,),
        in_specs=[
            pl.BlockSpec((sps, n_rows, width), lambda o: (o, 0, 0)),
            pl.BlockSpec((win.shape[0], sps, width, n_state), lambda o: (0, o, 0, 0)),
            pl.BlockSpec((sps,) + wcat.shape[1:], lambda o: (o, 0, 0)),
            pl.BlockSpec((sps,) + ak.shape[1:], lambda o: (o, 0, 0)),
            pl.BlockSpec((sps,) + dsk.shape[1:], lambda o: (o, 0, 0)),
        ],
        out_specs=pl.BlockSpec((sps, n_rows, width), lambda o: (o, 0, 0)),
        out_shape=jax.ShapeDtypeStruct(u_tiles.shape, F32),
        scratch_shapes=[
            pltpu.VMEM((sps, n_rows, n_state), BF16), pltpu.VMEM((sps, n_rows, n_state), BF16),
            pltpu.VMEM((sps, rb, n_state), F32), pltpu.VMEM((sps, rb, n_state), F32),
        ],
        compiler_params=pltpu.CompilerParams(
            dimension_semantics=("arbitrary",), vmem_limit_bytes=VMEM_LIMIT),
        name="s5_scan",
    )(u_tiles, win, wcat, ak, dsk)


def _post_kernel(x_ref, att_ref, ys_first_ref, ys_next_ref, wg_ref, bg_ref, wo_ref, g2_ref,
                 w1_ref, w2_ref, gf_ref, o_ref, y_ref, *, ff_chunk, final_norm):
    nb, tt, d = x_ref.shape
    aw = att_ref.shape[2]
    unslab = lambda ref: _from_slabs(ref, nb).reshape(nb * tt, -1)

    @pl.when(pl.program_id(0) == 0)
    def _():
        y_ref[...] = unslab(ys_first_ref)

    y = y_ref[...]
    z = 0.5 * y * (1.0 + lax.erf(y * (1.0 / math.sqrt(2.0))))
    gate = jax.nn.sigmoid(jnp.dot(z.astype(BF16), wg_ref[...], preferred_element_type=F32) + bg_ref[...])
    ssm = (z * gate).astype(BF16)
    h = (x_ref[...].reshape(nb * tt, d)
         + jnp.dot(att_ref[...].reshape(nb * tt, aw), wo_ref[:aw, :], preferred_element_type=F32)
         + jnp.dot(ssm, wo_ref[aw:, :], preferred_element_type=F32))
    hn2 = _rms(h, g2_ref[...]).astype(BF16)
    y_ref[...] = unslab(ys_next_ref)
    mlp = jnp.zeros_like(h)
    for f0 in range(0, w1_ref.shape[1], ff_chunk):
        a = jnp.maximum(jnp.dot(hn2, w1_ref[:, f0:f0 + ff_chunk], preferred_element_type=F32), 0.0)
        mlp = mlp + jnp.dot((a * a).astype(BF16), w2_ref[f0:f0 + ff_chunk, :], preferred_element_type=F32)
    h = h + mlp
    o_ref[...] = (_rms(h, gf_ref[...]) if final_norm else h).reshape(nb, tt, d)


def _post(x3, att, y_slabs, w_glu, b_glu, w_out, g2, w1, w2, gf, *, final_norm, tt=64):
    nb, s, d = x3.shape
    aw = att.shape[2]
    n_slabs, _, slab_w = y_slabs.shape
    rows = lambda width: pl.BlockSpec((nb, tt, width), lambda i: (0, i, 0))
    const = lambda arr: pl.BlockSpec(arr.shape, lambda i: (0, 0))
    kern = functools.partial(_post_kernel, ff_chunk=1024, final_norm=final_norm)
    n_steps = s // tt
    slab_block = (n_slabs, tt // S5_CHUNK * nb, slab_w)
    return pl.pallas_call(
        kern,
        grid=(n_steps,),
        in_specs=[rows(d), rows(aw),
                  pl.BlockSpec(slab_block, lambda i: (0, 0, 0)),
                  pl.BlockSpec(slab_block, lambda i: (0, jnp.minimum(i + 1, n_steps - 1), 0)),
                  const(w_glu), const(b_glu), const(w_out), const(g2), const(w1), const(w2), const(gf)],
        out_specs=rows(d),
        out_shape=jax.ShapeDtypeStruct((nb, s, d), F32),
        scratch_shapes=[pltpu.VMEM((nb * tt, n_slabs * S5_SLAB_LANES), F32)],
        compiler_params=pltpu.CompilerParams(
            dimension_semantics=("arbitrary",), vmem_limit_bytes=VMEM_LIMIT),
        name="post_mlp",
    )(x3, att, y_slabs, y_slabs, w_glu, b_glu, w_out, g2, w1, w2, gf)


def kernel(x, norm1_g, w_in, lam_q1, lam_k1, lam_q2, lam_k2, subln_g, ssm_lam_re, ssm_lam_im, ssm_log_dt,
           ssm_b_re, ssm_b_im, ssm_c_re, ssm_c_im, ssm_d, w_glu, b_glu, w_out, norm2_g, w_mlp1, w_mlp2,
           final_g):
    bsz, s, d = x.shape
    depth = w_in.shape[0]
    ssm_w = ssm_d.shape[-1]
    attn_w = w_out.shape[1] - ssm_w
    qk_dim = attn_w // ATTN_HEADS // 2
    scale = float(qk_dim) ** -0.5 * LOG2_E

    h3 = x.astype(F32)
    for l in range(depth):
        lambda_init = 0.8 - 0.6 * math.exp(-0.3 * l)
        qk, v_t, u_slabs = _inproj(h3, norm1_g[l][None].astype(F32), w_in[l].astype(BF16),
                                   q_width=attn_w, n_heads=ATTN_HEADS, ssm_width=ssm_w, scale=scale)

        row = lambda t: t[l][None].astype(F32)
        att = _attention(qk, v_t, row(lam_q1), row(lam_k1), row(lam_q2), row(lam_k2), row(subln_g),
                         lambda_init=lambda_init)

        pbr, pbi, pcr, pci, ak_slabs = _s5_discretise(ssm_lam_re[l], ssm_lam_im[l], ssm_log_dt[l],
                                                      ssm_b_re[l], ssm_b_im[l], ssm_c_re[l], ssm_c_im[l])
        win, wcat = _s5_chunk_weights(pbr, pbi, pcr, pci)
        dsk_slabs = jnp.tile(ssm_d[l].astype(F32).reshape(-1, 1, S5_SLAB_LANES), (1, 1, S5_CHUNK))
        y_slabs = _s5_chunk_scan(u_slabs, win, wcat, ak_slabs, dsk_slabs, nb=bsz)

        h3 = _post(h3, att, y_slabs, w_glu[l].astype(BF16), row(b_glu), w_out[l].astype(BF16), row(norm2_g),
                   w_mlp1[l].astype(BF16), w_mlp2[l].astype(BF16), final_g[None].astype(F32),
                   final_norm=(l == depth - 1))
    return h3.astype(x.dtype)
```

```python
import functools
import math

import jax
import jax.numpy as jnp
from jax import lax
from jax.experimental import pallas as pl
from jax.experimental.pallas import tpu as pltpu

F32 = jnp.float32
BF16 = jnp.bfloat16

EPS = 1e-6
ATTN_HEADS = 4
SSM_GROUP = 16
SSM_STATE = 64
S5_SLAB_GROUPS = 2
S5_SLAB_LANES = S5_SLAB_GROUPS * SSM_GROUP
VMEM_LIMIT = 56 * 1024 * 1024


def _rms(x, g):
    return x * lax.rsqrt(jnp.mean(x * x, axis=-1, keepdims=True) + EPS) * g


def _to_slabs(u3):
    nb, tt, w = u3.shape
    n_ch = tt // S5_CHUNK
    t = jnp.transpose(u3.reshape(nb, n_ch, S5_CHUNK, w), (1, 2, 0, 3))
    slabs = []
    for q in range(w // S5_SLAB_LANES):
        lanes = slice(q * S5_SLAB_LANES, (q + 1) * S5_SLAB_LANES)
        rows = [jnp.concatenate([t[c, j][:, lanes] for j in range(S5_CHUNK)], axis=1) for c in range(n_ch)]
        slabs.append(jnp.concatenate(rows, axis=0))
    return slabs


def _from_slabs(slab_ref, nb):
    n_slabs, rows, _ = slab_ref.shape
    n_ch = rows // nb
    per_chunk = []
    for c in range(n_ch):
        steps = [jnp.concatenate([slab_ref[q, c * nb:(c + 1) * nb, j * S5_SLAB_LANES:(j + 1) * S5_SLAB_LANES]
                                  for q in range(n_slabs)], axis=1) for j in range(S5_CHUNK)]
        per_chunk.append(jnp.stack(steps, axis=0))
    t = jnp.stack(per_chunk, axis=0)
    return jnp.transpose(t, (2, 0, 1, 3)).reshape(nb, n_ch * S5_CHUNK, n_slabs * S5_SLAB_LANES)


def _inproj_kernel(x_ref, g_ref, w_ref, qk_ref, vt_ref, us_ref, *, q_width, scale, chunk):
    nb, tt, d = x_ref.shape
    qk_width, n_heads, hv = qk_ref.shape[2], vt_ref.shape[1], vt_ref.shape[2]
    v_width = n_heads * hv
    hn = _rms(x_ref[...].reshape(nb * tt, d), g_ref[...]).astype(BF16)
    u = jnp.dot(hn, w_ref[:, qk_width + v_width:], preferred_element_type=F32)
    for q, slab in enumerate(_to_slabs(u.reshape(nb, tt, u.shape[1]))):
        us_ref[q] = slab
    v = jnp.dot(hn, w_ref[:, qk_width:qk_width + v_width], preferred_element_type=F32)
    for b in range(nb):
        for hh in range(n_heads):
            vt_ref[b, hh] = v[b * tt:(b + 1) * tt, hh * hv:(hh + 1) * hv].T.astype(BF16)
    for c0 in range(0, qk_width, chunk):
        p = jnp.dot(hn, w_ref[:, c0:c0 + chunk], preferred_element_type=F32)
        if c0 < q_width:
            p = p * scale
        qk_ref[:, :, c0:c0 + chunk] = p.astype(BF16).reshape(nb, tt, chunk)


def _inproj(x3, g1, w_in, *, q_width, n_heads, ssm_width, scale, tt=128):
    nb, s, d = x3.shape
    n_out = w_in.shape[1]
    v_width = n_out - 2 * q_width - ssm_width
    n_slabs = ssm_width // S5_SLAB_LANES
    kern = functools.partial(_inproj_kernel, q_width=q_width, scale=scale, chunk=512)
    return pl.pallas_call(
        kern,
        grid=(s // tt,),
        in_specs=[
            pl.BlockSpec((nb, tt, d), lambda i: (0, i, 0)),
            pl.BlockSpec((1, d), lambda i: (0, 0)),
            pl.BlockSpec((d, n_out), lambda i: (0, 0)),
        ],
        out_specs=[
            pl.BlockSpec((nb, tt, 2 * q_width), lambda i: (0, i, 0)),
            pl.BlockSpec((nb, n_heads, v_width // n_heads, tt), lambda i: (0, 0, 0, i)),
            pl.BlockSpec((n_slabs, tt // S5_CHUNK * nb, S5_CHUNK * S5_SLAB_LANES), lambda i: (0, i, 0)),
        ],
        out_shape=[
            jax.ShapeDtypeStruct((nb, s, 2 * q_width), BF16),
            jax.ShapeDtypeStruct((nb, n_heads, v_width // n_heads, s), BF16),
            jax.ShapeDtypeStruct((n_slabs, s // S5_CHUNK * nb, S5_CHUNK * S5_SLAB_LANES), F32),
        ],
        compiler_params=pltpu.CompilerParams(
            dimension_semantics=("arbitrary",), vmem_limit_bytes=VMEM_LIMIT),
        name="inproj",
    )(x3, g1, w_in)


POS_LOW_BITS = 3
LOG2_E = 1.4426950408889634
SAFE_LOG2_SCORE = 96.0
SAFE_VALUE = 2.0 ** 16
NORM_FOLD = 4


def _attn_kernel(lq1_ref, lk1_ref, lq2_ref, lk2_ref, sg_ref, q_ref, k_ref, vin_ref, o_ref,
                 kex_ref, qex_ref, diag_ref, kx_ref, s0_ref, s1_ref, s2_ref, s3_ref, e0_ref, e1_ref,
                 *, qb, lambda_init):
    h = pl.program_id(0)
    s_refs, e_refs = (s0_ref, s1_ref, s2_ref, s3_ref), (e0_ref, e1_ref)
    s_len, hv = k_ref.shape[1], k_ref.shape[2]
    dh = hv // 2

    lam = (jnp.exp(jnp.sum(lq1_ref[...] * lk1_ref[...], axis=-1, keepdims=True))
           - jnp.exp(jnp.sum(lq2_ref[...] * lk2_ref[...], axis=-1, keepdims=True))
           + lambda_init)

    lane = lax.broadcasted_iota(jnp.int32, (1, hv), 1)
    data_lanes = (lane < dh, lane >= dh)

    @pl.when(pl.program_id(1) == 0)
    def _():
        slope = jnp.float32(2.0 ** -8.0)
        for hh in range(ATTN_HEADS - 1):
            slope = jnp.where(h == hh, jnp.float32(2.0 ** (-8.0 * (hh + 1) / ATTN_HEADS)), slope)
        c2 = jnp.full((1, 1), slope, F32) * LOG2_E
        c2_parts = []
        rest = c2
        for _ in range(3):
            part = rest.astype(BF16).astype(F32)
            c2_parts.append(part)
            rest = rest - part

        def extras(vals):
            out = 0.0
            for e, val in enumerate(vals):
                out = jnp.where(lane == dh + e, val, out)
            return out

        pos = lax.broadcasted_iota(jnp.int32, (s_len, 1), 0)
        pos_lo = pos & ((1 << POS_LOW_BITS) - 1)
        p_hi, p_lo = (pos - pos_lo).astype(F32), pos_lo.astype(F32)
        kex = extras(c2_parts + c2_parts + [p_hi] * 3 + [p_lo] * 3)
        qex = extras([-p_hi] * 3 + [-p_lo] * 3 + c2_parts + c2_parts)
        kex_ref[0], kex_ref[1] = kex, pltpu.roll(kex, dh, 1)
        qex_ref[0], qex_ref[1] = qex, pltpu.roll(qex, dh, 1)
        off = lax.broadcasted_iota(jnp.int32, (qb, qb), 0) - lax.broadcasted_iota(jnp.int32, (qb, qb), 1)
        diag_ref[...] = jnp.abs(off).astype(F32) * (-c2)

    kf = k_ref[0].astype(F32)
    for c in range(2):
        kx_ref[c] = jnp.where(data_lanes[c], kf, kex_ref[c]).astype(BF16)

    def norm2_bound(t):
        sq = (t * t).reshape(NORM_FOLD, s_len // NORM_FOLD, hv)
        return jnp.max(jnp.sum(jnp.max(sq, axis=0), axis=1, keepdims=True), axis=0, keepdims=True)

    v_max = jnp.max(jnp.max(jnp.abs(vin_ref[0, 0].astype(F32)), axis=1, keepdims=True), axis=0, keepdims=True)
    small_scores = jnp.logical_and(
        (norm2_bound(q_ref[0].astype(F32)) * norm2_bound(kf))[0, 0] <= SAFE_LOG2_SCORE ** 2,
        v_max[0, 0] <= SAFE_VALUE)

    n_blocks = s_len // qb

    def score_pieces(n, q_side):
        qi, c = divmod(n, 2)
        lo, hi = qi * qb, (qi + 1) * qb
        if qi not in q_side:
            q_side[qi] = q_ref[0, lo:hi, :].astype(F32)
        qf = q_side[qi]
        aug_t = lambda ex: jnp.where(data_lanes[c], qf, ex).T.astype(BF16)
        dot = lambda r0, r1, qt: jnp.dot(kx_ref[c, r0:r1, :], qt, preferred_element_type=F32)
        q_extra = qex_ref[c, lo:hi, :]
        q_left, q_right = aug_t(q_extra), aug_t(-q_extra)
        for r0 in range(0, s_len, qb):
            if r0 == lo:
                yield lo, hi, dot(lo, hi, aug_t(jnp.zeros_like(q_extra))) + diag_ref[...]
            else:
                yield r0, r0 + qb, dot(r0, r0 + qb, q_left if r0 < lo else q_right)

    def weighted_values(n, denom):
        return jnp.dot(vin_ref[0, 0], e_refs[n % 2][...], preferred_element_type=F32) / denom

    def finish_block(qi, out_t):
        o = (out_t[0] - lam * out_t[1]).T
        o = _rms(o, sg_ref[...]) * (1.0 - lambda_init)
        o_ref[0, qi * qb:(qi + 1) * qb, :] = o.astype(o_ref.dtype)

    @pl.when(small_scores)
    def _():
        q_side = {}
        for qi in range(n_blocks):
            out_t = []
            for n in (2 * qi, 2 * qi + 1):
                denom = 0.0
                for r0, r1, sc in score_pieces(n, q_side):
                    e = jnp.exp2(sc)
                    denom = denom + jnp.sum(e, axis=0, keepdims=True)
                    e_refs[n % 2][r0:r1, :] = e.astype(BF16)
                out_t.append(weighted_values(n, denom))
            finish_block(qi, out_t)

    @pl.when(jnp.logical_not(small_scores))
    def _():
        q_side = {}

        def scores(n):
            for r0, r1, sc in score_pieces(n, q_side):
                s_refs[n % 4][r0:r1, :] = sc

        def exponentials(n):
            sc = s_refs[n % 4][...]
            e = jnp.exp2(sc - jnp.max(sc, axis=0, keepdims=True))
            e_refs[n % 2][...] = e.astype(BF16)
            return jnp.sum(e, axis=0, keepdims=True)

        scores(0), scores(1)
        for qi in range(n_blocks):
            if qi + 1 < n_blocks:
                scores(2 * qi + 2), scores(2 * qi + 3)
            out_t = []
            for n in (2 * qi, 2 * qi + 1):
                out_t.append(weighted_values(n, exponentials(n)))
            finish_block(qi, out_t)


def _attention(qk3, v_t, lq1, lk1, lq2, lk2, subln_g, *, lambda_init, qb=512):
    b, s, w = qk3.shape
    hv = w // (2 * ATTN_HEADS)
    assert s <= 256 << POS_LOW_BITS and s % qb == 0
    kern = functools.partial(_attn_kernel, qb=qb, lambda_init=lambda_init)
    small = lambda shape: pl.BlockSpec(shape, lambda hi, bi: (0, 0))
    head = lambda off: pl.BlockSpec((1, s, hv), lambda hi, bi: (bi, 0, off + hi))
    return pl.pallas_call(
        kern,
        grid=(ATTN_HEADS, b),
        in_specs=[
            small(lq1.shape), small(lk1.shape), small(lq2.shape), small(lk2.shape), small(subln_g.shape),
            head(0), head(ATTN_HEADS), pl.BlockSpec((1, 1, hv, s), lambda hi, bi: (bi, hi, 0, 0)),
        ],
        out_specs=head(0),
        out_shape=jax.ShapeDtypeStruct((b, s, ATTN_HEADS * hv), BF16),
        scratch_shapes=[
            pltpu.VMEM((2, s, hv), F32), pltpu.VMEM((2, s, hv), F32), pltpu.VMEM((qb, qb), F32),
            pltpu.VMEM((2, s, hv), BF16),
            pltpu.VMEM((s, qb), F32), pltpu.VMEM((s, qb), F32), pltpu.VMEM((s, qb), F32), pltpu.VMEM((s, qb), F32),
            pltpu.VMEM((s, qb), BF16), pltpu.VMEM((s, qb), BF16),
        ],
        compiler_params=pltpu.CompilerParams(
            dimension_semantics=("arbitrary", "arbitrary"), vmem_limit_bytes=VMEM_LIMIT),
        name="diff_attn",
    )(lq1, lk1, lq2, lk2, subln_g, qk3, qk3, v_t)


S5_CHUNK = 8


def _s5_disc_kernel(lr_ref, li_ref, ldt_ref, bre_ref, bim_ref, cre_ref, cim_ref,
                    pbr_ref, pbi_ref, pcr_ref, pci_ref, akr_ref, aki_ref):
    cmul = lambda ar, ai, xr, xi: (ar * xr - ai * xi, ar * xi + ai * xr)

    lr, li = lr_ref[...], li_ref[...]
    dt = jnp.exp(ldt_ref[...])
    mag = jnp.exp(lr * dt)
    ang = li * dt
    ar, ai = mag * jnp.cos(ang), mag * jnp.sin(ang)
    nr = ar - 1.0
    den = lr * lr + li * li
    cr = (nr * lr + ai * li) / den
    ci = (ai * lr - nr * li) / den
    b_re, b_im = bre_ref[...], bim_ref[...]
    pr, pi = cr * b_re - ci * b_im, cr * b_im + ci * b_re
    kr, ki = ar, ai
    for n in range(pbr_ref.shape[0]):
        pbr_ref[n] = pr
        pbi_ref[n] = pi
        pr, pi = cmul(ar, ai, pr, pi)
        if n > 0:
            kr, ki = cmul(ar, ai, kr, ki)
    akr_ref[...] = kr
    aki_ref[...] = ki

    qr, qi = cre_ref[...], cim_ref[...]
    for n in range(pcr_ref.shape[0]):
        pcr_ref[n] = qr
        pci_ref[n] = -qi
        qr, qi = cmul(ar, ai, qr, qi)


def _s5_discretise(lam_re, lam_im, log_dt, b_re, b_im, c_re, c_im):
    nd, g, p, hc = b_re.shape
    nt = g // S5_SLAB_GROUPS
    rows, width = nd * nt * hc, S5_SLAB_GROUPS * p
    arrange = lambda t: (jnp.broadcast_to(t.astype(F32), (nd, g, hc, p)).reshape(nd, nt, S5_SLAB_GROUPS, hc, p)
                         .transpose(0, 1, 3, 2, 4).reshape(rows, width))
    args = (arrange(lam_re[:, :, None, :]), arrange(lam_im[:, :, None, :]), arrange(log_dt[..., None, None]),
            arrange(jnp.swapaxes(b_re, 2, 3)), arrange(jnp.swapaxes(b_im, 2, 3)), arrange(c_re), arrange(c_im))
    stack = lambda n: jax.ShapeDtypeStruct((n, rows, width), F32)
    one = jax.ShapeDtypeStruct((rows, width), F32)
    pbr, pbi, pcr, pci, akr, aki = pl.pallas_call(
        _s5_disc_kernel,
        out_shape=[stack(S5_CHUNK), stack(S5_CHUNK), stack(S5_CHUNK + 1), stack(S5_CHUNK + 1), one, one],
        name="s5_disc")(*args)
    tiles = lambda t: t.reshape(t.shape[0], nd, nt, hc, width)
    ak = jnp.stack([akr, aki]).reshape(2, nd, nt, hc, width)[:, :, :, 0]
    ak = ak.transpose(2, 1, 0, 3).reshape(nt, 2 * nd, width)
    return tiles(pbr), tiles(pbi), tiles(pcr), tiles(pci), ak


def _s5_weights_kernel(pbr_ref, pbi_ref, pcr_ref, pci_ref, win_ref, wcat_ref):
    chunk, nd, hc, half = pbr_ref.shape[0], pbr_ref.shape[1], pbr_ref.shape[3], pbr_ref.shape[4]
    blk = S5_SLAB_GROUPS * hc
    p = half // S5_SLAB_GROUPS
    same_group = (lax.broadcasted_iota(jnp.int32, (blk, half), 0) // hc
                  == lax.broadcasted_iota(jnp.int32, (blk, half), 1) // p)

    def expand(re, im):
        grid = lambda t: jnp.where(same_group, jnp.tile(t, (S5_SLAB_GROUPS, 1)), 0.0)
        return jnp.concatenate([grid(re), grid(im)], axis=1)

    def split_dot(a, b):
        a_hi, b_hi = a.astype(BF16), b.astype(BF16)
        a_lo, b_lo = (a - a_hi.astype(F32)).astype(BF16), (b - b_hi.astype(F32)).astype(BF16)
        return jnp.dot(jnp.concatenate([a_hi, a_hi, a_lo], axis=1), jnp.concatenate([b_hi, b_lo, b_hi], axis=0),
                       preferred_element_type=F32)

    n_state = 2 * half
    for o in range(pbr_ref.shape[2]):
        wall = [[expand(pbr_ref[n, d, o], pbi_ref[n, d, o]) for n in range(chunk)] for d in range(nd)]
        mall = [[expand(pcr_ref[n, d, o], pci_ref[n, d, o]).T for n in range(chunk + 1)] for d in range(nd)]

        for d in range(nd):
            for j in range(chunk):
                n_in = chunk - 1 - j if d == 0 else j
                win_ref[d, o, j * blk:(j + 1) * blk, :] = wall[d][n_in].astype(win_ref.dtype)
            n_out = [j + 1 if d == 0 else chunk - j for j in range(chunk)]
            r0 = chunk * blk + d * n_state
            wcat_ref[o, r0:r0 + n_state, :] = jnp.concatenate([mall[d][n] for n in n_out],
                                                              axis=1).astype(wcat_ref.dtype)

        resp = [[split_dot(wall[d][n], mall[d][0]) for n in range(chunk)] for d in range(nd)]
        both = resp[0][0] + resp[1][0]
        for j in range(chunk):
            row = [resp[0][jp - j] if jp > j else resp[1][j - jp] if jp < j else both for jp in range(chunk)]
            wcat_ref[o, j * blk:(j + 1) * blk, :] = jnp.concatenate(row, axis=1).astype(wcat_ref.dtype)


def _s5_chunk_weights(pbr, pbi, pcr, pci, *, slabs_per_step=4):
    chunk, nd, nt, hc, half = pbr.shape
    width, n_state = chunk * S5_SLAB_GROUPS * hc, 2 * half
    sps = slabs_per_step
    spec = lambda t: pl.BlockSpec((t.shape[0], nd, sps, hc, half), lambda o: (0, 0, o, 0, 0))
    return pl.pallas_call(
        _s5_weights_kernel,
        grid=(nt // sps,),
        in_specs=[spec(pbr), spec(pbi), spec(pcr), spec(pci)],
        out_specs=[pl.BlockSpec((nd, sps, width, n_state), lambda o: (0, o, 0, 0)),
                   pl.BlockSpec((sps, width + nd * n_state, width), lambda o: (o, 0, 0))],
        out_shape=[jax.ShapeDtypeStruct((nd, nt, width, n_state), BF16),
                   jax.ShapeDtypeStruct((nt, width + nd * n_state, width), BF16)],
        compiler_params=pltpu.CompilerParams(dimension_semantics=("arbitrary",), vmem_limit_bytes=VMEM_LIMIT),
        name="s5_weights",
    )(pbr, pbi, pcr, pci)


def _s5_chunk_kernel(u_ref, win_ref, wcat_ref, ak_ref, dsk_ref, y_ref, xf_ref, xb_ref, s0_ref, s1_ref, *, nb, rb):
    n_sl, n_rows, half = u_ref.shape[0], u_ref.shape[1], ak_ref.shape[2]
    n_blk, pair = n_rows // rb, 2 * nb
    s_refs = (s0_ref, s1_ref)
    order = {0: list(range(n_blk)), 1: list(reversed(range(n_blk)))}

    def increments(d, i):
        blk = order[d][i]
        for sl in range(n_sl):
            s_refs[i % 2][sl] = jnp.dot(u_ref[sl, blk * rb:(blk + 1) * rb, :].astype(BF16), win_ref[d, sl],
                                        preferred_element_type=F32)

    def scan_block(d, i, xs):
        blk, s_ref, x_ref = order[d][i], s_refs[i % 2], (xf_ref, xb_ref)[d]
        ak = [(jnp.broadcast_to(ak_ref[sl, 2 * d:2 * d + 1, :], (nb, half)),
               jnp.broadcast_to(ak_ref[sl, 2 * d + 1:2 * d + 2, :], (nb, half))) for sl in range(n_sl)]

        def advance(sl, x, rows):
            ar, ai = ak[sl]
            return (ar * x[0] - ai * x[1] + s_ref[sl, rows, :half], ar * x[1] + ai * x[0] + s_ref[sl, rows, half:])

        def body(it, xs):
            r0 = pl.multiple_of((it if d == 0 else rb // pair - 1 - it) * pair, pair)
            early, late = pl.ds(r0, nb), pl.ds(r0 + nb, nb)
            dst = pl.ds(pl.multiple_of(blk * rb + r0, pair), pair)
            out = []
            for sl, x in enumerate(xs):
                x1 = advance(sl, x, early if d == 0 else late)
                x2 = advance(sl, x1, late if d == 0 else early)
                enter_early, enter_late = (x, x1) if d == 0 else (x1, x)
                x_ref[sl, dst, :half] = jnp.concatenate([enter_early[0], enter_late[0]], axis=0).astype(BF16)
                x_ref[sl, dst, half:] = jnp.concatenate([enter_early[1], enter_late[1]], axis=0).astype(BF16)
                out.append(x2)
            return tuple(out)

        return lax.fori_loop(0, rb // pair, body, xs, unroll=True)

    zero = jnp.zeros((nb, half), F32)
    for d in range(2):
        xs = ((zero, zero),) * n_sl
        increments(d, 0)
        for i in range(n_blk):
            if i + 1 < n_blk:
                increments(d, i + 1)
            xs = scan_block(d, i, xs)

    for blk in range(n_blk):
        rows = slice(blk * rb, (blk + 1) * rb)
        for sl in range(n_sl):
            u = u_ref[sl, rows, :]
            lhs = jnp.concatenate([u.astype(BF16), xf_ref[sl, rows, :], xb_ref[sl, rows, :]], axis=1)
            y_ref[sl, rows, :] = jnp.dot(lhs, wcat_ref[sl], preferred_element_type=F32) + dsk_ref[sl] * u


def _s5_chunk_scan(u_tiles, win, wcat, ak, dsk, *, nb, rb=256, slabs_per_step=4):
    nt, n_rows, width = u_tiles.shape
    n_state = win.shape[3]
    sps = slabs_per_step
    kern = functools.partial(_s5_chunk_kernel, nb=nb, rb=rb)
    return pl.pallas_call(
        kern,
        grid=(nt // sps,),
        in_specs=[
            pl.BlockSpec((sps, n_rows, width), lambda o: (o, 0, 0)),
            pl.BlockSpec((win.shape[0], sps, width, n_state), lambda o: (0, o, 0, 0)),
            pl.BlockSpec((sps,) + wcat.shape[1:], lambda o: (o, 0, 0)),
            pl.BlockSpec((sps,) + ak.shape[1:], lambda o: (o, 0, 0)),
            pl.BlockSpec((sps,) + dsk.shape[1:], lambda o: (o, 0, 0)),
        ],
        out_specs=pl.BlockSpec((sps, n_rows, width), lambda o: (o, 0, 0)),
        out_shape=jax.ShapeDtypeStruct(u_tiles.shape, F32),
        scratch_shapes=[
            pltpu.VMEM((sps, n_rows, n_state), BF16), pltpu.VMEM((sps, n_rows, n_state), BF16),
            pltpu.VMEM((sps, rb, n_state), F32), pltpu.VMEM((sps, rb, n_state), F32),
        ],
        compiler_params=pltpu.CompilerParams(
            dimension_semantics=("arbitrary",), vmem_limit_bytes=VMEM_LIMIT),
        name="s5_scan",
    )(u_tiles, win, wcat, ak, dsk)


def _post_kernel(x_ref, att_ref, ys_first_ref, ys_next_ref, wg_ref, bg_ref, wo_ref, g2_ref,
                 w1_ref, w2_ref, gf_ref, o_ref, y_ref, *, ff_chunk, final_norm):
    nb, tt, d = x_ref.shape
    aw = att_ref.shape[2]
    unslab = lambda ref: _from_slabs(ref, nb).reshape(nb * tt, -1)

    @pl.when(pl.program_id(0) == 0)
    def _():
        y_ref[...] = unslab(ys_first_ref)

    y = y_ref[...]
    z = 0.5 * y * (1.0 + lax.erf(y * (1.0 / math.sqrt(2.0))))
    gate = jax.nn.sigmoid(jnp.dot(z.astype(BF16), wg_ref[...], preferred_element_type=F32) + bg_ref[...])
    ssm = (z * gate).astype(BF16)
    h = (x_ref[...].reshape(nb * tt, d)
         + jnp.dot(att_ref[...].reshape(nb * tt, aw), wo_ref[:aw, :], preferred_element_type=F32)
         + jnp.dot(ssm, wo_ref[aw:, :], preferred_element_type=F32))
    hn2 = _rms(h, g2_ref[...]).astype(BF16)
    y_ref[...] = unslab(ys_next_ref)
    mlp = jnp.zeros_like(h)
    for f0 in range(0, w1_ref.shape[1], ff_chunk):
        a = jnp.maximum(jnp.dot(hn2, w1_ref[:, f0:f0 + ff_chunk], preferred_element_type=F32), 0.0)
        mlp = mlp + jnp.dot((a * a).astype(BF16), w2_ref[f0:f0 + ff_chunk, :], preferred_element_type=F32)
    h = h + mlp
    o_ref[...] = (_rms(h, gf_ref[...]) if final_norm else h).reshape(nb, tt, d)


def _post(x3, att, y_slabs, w_glu, b_glu, w_out, g2, w1, w2, gf, *, final_norm, tt=64):
    nb, s, d = x3.shape
    aw = att.shape[2]
    n_slabs, _, slab_w = y_slabs.shape
    rows = lambda width: pl.BlockSpec((nb, tt, width), lambda i: (0, i, 0))
    const = lambda arr: pl.BlockSpec(arr.shape, lambda i: (0, 0))
    kern = functools.partial(_post_kernel, ff_chunk=1024, final_norm=final_norm)
    n_steps = s // tt
    slab_block = (n_slabs, tt // S5_CHUNK * nb, slab_w)
    return pl.pallas_call(
        kern,
        grid=(n_steps,),
        in_specs=[rows(d), rows(aw),
                  pl.BlockSpec(slab_block, lambda i: (0, 0, 0)),
                  pl.BlockSpec(slab_block, lambda i: (0, jnp.minimum(i + 1, n_steps - 1), 0)),
                  const(w_glu), const(b_glu), const(w_out), const(g2), const(w1), const(w2), const(gf)],
        out_specs=rows(d),
        out_shape=jax.ShapeDtypeStruct((nb, s, d), F32),
        scratch_shapes=[pltpu.VMEM((nb * tt, n_slabs * S5_SLAB_LANES), F32)],
        compiler_params=pltpu.CompilerParams(
            dimension_semantics=("arbitrary",), vmem_limit_bytes=VMEM_LIMIT),
        name="post_mlp",
    )(x3, att, y_slabs, y_slabs, w_glu, b_glu, w_out, g2, w1, w2, gf)


def kernel(x, norm1_g, w_in, lam_q1, lam_k1, lam_q2, lam_k2, subln_g, ssm_lam_re, ssm_lam_im, ssm_log_dt,
           ssm_b_re, ssm_b_im, ssm_c_re, ssm_c_im, ssm_d, w_glu, b_glu, w_out, norm2_g, w_mlp1, w_mlp2,
           final_g):
    bsz, s, d = x.shape
    depth = w_in.shape[0]
    ssm_w = ssm_d.shape[-1]
    attn_w = w_out.shape[1] - ssm_w
    qk_dim = attn_w // ATTN_HEADS // 2
    scale = float(qk_dim) ** -0.5 * LOG2_E

    h3 = x.astype(F32)
    for l in range(depth):
        lambda_init = 0.8 - 0.6 * math.exp(-0.3 * l)
        qk, v_t, u_slabs = _inproj(h3, norm1_g[l][None].astype(F32), w_in[l].astype(BF16),
                                   q_width=attn_w, n_heads=ATTN_HEADS, ssm_width=ssm_w, scale=scale)

        row = lambda t: t[l][None].astype(F32)
        att = _attention(qk, v_t, row(lam_q1), row(lam_k1), row(lam_q2), row(lam_k2), row(subln_g),
                         lambda_init=lambda_init)

        pbr, pbi, pcr, pci, ak_slabs = _s5_discretise(ssm_lam_re[l], ssm_lam_im[l], ssm_log_dt[l],
                                                      ssm_b_re[l], ssm_b_im[l], ssm_c_re[l], ssm_c_im[l])
        win, wcat = _s5_chunk_weights(pbr, pbi, pcr, pci)
        dsk_slabs = jnp.tile(ssm_d[l].astype(F32).reshape(-1, 1, S5_SLAB_LANES), (1, 1, S5_CHUNK))
        y_slabs = _s5_chunk_scan(u_slabs, win, wcat, ak_slabs, dsk_slabs, nb=bsz)

        h3 = _post(h3, att, y_slabs, w_glu[l].astype(BF16), row(b_glu), w_out[l].astype(BF16), row(norm2_g),
                   w_mlp1[l].astype(BF16), w_mlp2[l].astype(BF16), final_g[None].astype(F32),
                   final_norm=(l == depth - 1))
    return h3.astype(x.dtype)
```

```python
import functools
import math

import jax
import jax.numpy as jnp
from jax import lax
from jax.experimental import pallas as pl
from jax.experimental.pallas import tpu as pltpu

F32 = jnp.float32
BF16 = jnp.bfloat16

EPS = 1e-6
ATTN_HEADS = 4
SSM_GROUP = 16
SSM_STATE = 64
S5_SLAB_GROUPS = 2
S5_SLAB_LANES = S5_SLAB_GROUPS * SSM_GROUP
VMEM_LIMIT = 56 * 1024 * 1024


def _rms(x, g):
    return x * lax.rsqrt(jnp.mean(x * x, axis=-1, keepdims=True) + EPS) * g


def _to_slabs(u3):
    nb, tt, w = u3.shape
    n_ch = tt // S5_CHUNK
    t = jnp.transpose(u3.reshape(nb, n_ch, S5_CHUNK, w), (1, 2, 0, 3))
    slabs = []
    for q in range(w // S5_SLAB_LANES):
        lanes = slice(q * S5_SLAB_LANES, (q + 1) * S5_SLAB_LANES)
        rows = [jnp.concatenate([t[c, j][:, lanes] for j in range(S5_CHUNK)], axis=1) for c in range(n_ch)]
        slabs.append(jnp.concatenate(rows, axis=0))
    return slabs


def _from_slabs(slab_ref, nb):
    n_slabs, rows, _ = slab_ref.shape
    n_ch = rows // nb
    per_chunk = []
    for c in range(n_ch):
        steps = [jnp.concatenate([slab_ref[q, c * nb:(c + 1) * nb, j * S5_SLAB_LANES:(j + 1) * S5_SLAB_LANES]
                                  for q in range(n_slabs)], axis=1) for j in range(S5_CHUNK)]
        per_chunk.append(jnp.stack(steps, axis=0))
    t = jnp.stack(per_chunk, axis=0)
    return jnp.transpose(t, (2, 0, 1, 3)).reshape(nb, n_ch * S5_CHUNK, n_slabs * S5_SLAB_LANES)


def _inproj_kernel(x_ref, g_ref, w_ref, qk_ref, vt_ref, us_ref, *, q_width, scale, chunk):
    nb, tt, d = x_ref.shape
    qk_width, n_heads, hv = qk_ref.shape[2], vt_ref.shape[1], vt_ref.shape[2]
    v_width = n_heads * hv
    hn = _rms(x_ref[...].reshape(nb * tt, d), g_ref[...]).astype(BF16)
    w = lambda c0, c1: w_ref[:, c0:c1].astype(BF16)
    u = jnp.dot(hn, w(qk_width + v_width, w_ref.shape[1]), preferred_element_type=F32)
    for q, slab in enumerate(_to_slabs(u.reshape(nb, tt, u.shape[1]))):
        us_ref[q] = slab
    v = jnp.dot(hn, w(qk_width, qk_width + v_width), preferred_element_type=F32)
    for b in range(nb):
        for hh in range(n_heads):
            vt_ref[b, hh] = v[b * tt:(b + 1) * tt, hh * hv:(hh + 1) * hv].T.astype(BF16)
    for c0 in range(0, qk_width, chunk):
        p = jnp.dot(hn, w(c0, c0 + chunk), preferred_element_type=F32)
        if c0 < q_width:
            p = p * scale
        qk_ref[:, :, c0:c0 + chunk] = p.astype(BF16).reshape(nb, tt, chunk)


def _inproj(x3, g1, w_in, *, q_width, n_heads, ssm_width, scale, tt=128):
    nb, s, d = x3.shape
    n_out = w_in.shape[1]
    v_width = n_out - 2 * q_width - ssm_width
    n_slabs = ssm_width // S5_SLAB_LANES
    kern = functools.partial(_inproj_kernel, q_width=q_width, scale=scale, chunk=512)
    return pl.pallas_call(
        kern,
        grid=(s // tt,),
        in_specs=[
            pl.BlockSpec((nb, tt, d), lambda i: (0, i, 0)),
            pl.BlockSpec((1, d), lambda i: (0, 0)),
            pl.BlockSpec((d, n_out), lambda i: (0, 0)),
        ],
        out_specs=[
            pl.BlockSpec((nb, tt, 2 * q_width), lambda i: (0, i, 0)),
            pl.BlockSpec((nb, n_heads, v_width // n_heads, tt), lambda i: (0, 0, 0, i)),
            pl.BlockSpec((n_slabs, tt // S5_CHUNK * nb, S5_CHUNK * S5_SLAB_LANES), lambda i: (0, i, 0)),
        ],
        out_shape=[
            jax.ShapeDtypeStruct((nb, s, 2 * q_width), BF16),
            jax.ShapeDtypeStruct((nb, n_heads, v_width // n_heads, s), BF16),
            jax.ShapeDtypeStruct((n_slabs, s // S5_CHUNK * nb, S5_CHUNK * S5_SLAB_LANES), F32),
        ],
        compiler_params=pltpu.CompilerParams(
            dimension_semantics=("arbitrary",), vmem_limit_bytes=VMEM_LIMIT),
        name="inproj",
    )(x3, g1, w_in)


POS_LOW_BITS = 3
LOG2_E = 1.4426950408889634
SAFE_LOG2_SCORE = 96.0
SAFE_VALUE = 2.0 ** 16
NORM_FOLD = 4


def _attn_kernel(lq1_ref, lk1_ref, lq2_ref, lk2_ref, sg_ref, q_ref, k_ref, vin_ref, o_ref,
                 kex_ref, qex_ref, diag_ref, kx_ref, s0_ref, s1_ref, s2_ref, s3_ref, e0_ref, e1_ref,
                 *, qb, lambda_init):
    h = pl.program_id(0)
    s_refs, e_refs = (s0_ref, s1_ref, s2_ref, s3_ref), (e0_ref, e1_ref)
    s_len, hv = k_ref.shape[1], k_ref.shape[2]
    dh = hv // 2

    lam = (jnp.exp(jnp.sum(lq1_ref[...] * lk1_ref[...], axis=-1, keepdims=True))
           - jnp.exp(jnp.sum(lq2_ref[...] * lk2_ref[...], axis=-1, keepdims=True))
           + lambda_init)

    lane = lax.broadcasted_iota(jnp.int32, (1, hv), 1)
    data_lanes = (lane < dh, lane >= dh)

    @pl.when(pl.program_id(1) == 0)
    def _():
        slope = jnp.float32(2.0 ** -8.0)
        for hh in range(ATTN_HEADS - 1):
            slope = jnp.where(h == hh, jnp.float32(2.0 ** (-8.0 * (hh + 1) / ATTN_HEADS)), slope)
        c2 = jnp.full((1, 1), slope, F32) * LOG2_E
        c2_parts = []
        rest = c2
        for _ in range(3):
            part = rest.astype(BF16).astype(F32)
            c2_parts.append(part)
            rest = rest - part

        def extras(vals):
            out = 0.0
            for e, val in enumerate(vals):
                out = jnp.where(lane == dh + e, val, out)
            return out

        pos = lax.broadcasted_iota(jnp.int32, (s_len, 1), 0)
        pos_lo = pos & ((1 << POS_LOW_BITS) - 1)
        p_hi, p_lo = (pos - pos_lo).astype(F32), pos_lo.astype(F32)
        kex = extras(c2_parts + c2_parts + [p_hi] * 3 + [p_lo] * 3)
        qex = extras([-p_hi] * 3 + [-p_lo] * 3 + c2_parts + c2_parts)
        kex_ref[0], kex_ref[1] = kex, pltpu.roll(kex, dh, 1)
        qex_ref[0], qex_ref[1] = qex, pltpu.roll(qex, dh, 1)
        off = lax.broadcasted_iota(jnp.int32, (qb, qb), 0) - lax.broadcasted_iota(jnp.int32, (qb, qb), 1)
        diag_ref[...] = jnp.abs(off).astype(F32) * (-c2)

    kf = k_ref[0].astype(F32)
    for c in range(2):
        kx_ref[c] = jnp.where(data_lanes[c], kf, kex_ref[c]).astype(BF16)

    def norm2_bound(t):
        sq = (t * t).reshape(NORM_FOLD, s_len // NORM_FOLD, hv)
        return jnp.max(jnp.sum(jnp.max(sq, axis=0), axis=1, keepdims=True), axis=0, keepdims=True)

    v_max = jnp.max(jnp.max(jnp.abs(vin_ref[0, 0].astype(F32)), axis=1, keepdims=True), axis=0, keepdims=True)
    small_scores = jnp.logical_and(
        (norm2_bound(q_ref[0].astype(F32)) * norm2_bound(kf))[0, 0] <= SAFE_LOG2_SCORE ** 2,
        v_max[0, 0] <= SAFE_VALUE)

    n_blocks = s_len // qb

    def score_pieces(n, q_side):
        qi, c = divmod(n, 2)
        lo, hi = qi * qb, (qi + 1) * qb
        if qi not in q_side:
            q_side[qi] = q_ref[0, lo:hi, :].astype(F32)
        qf = q_side[qi]
        aug_t = lambda ex: jnp.where(data_lanes[c], qf, ex).T.astype(BF16)
        dot = lambda r0, r1, qt: jnp.dot(kx_ref[c, r0:r1, :], qt, preferred_element_type=F32)
        q_extra = qex_ref[c, lo:hi, :]
        q_left, q_right = aug_t(q_extra), aug_t(-q_extra)
        for r0 in range(0, s_len, qb):
            if r0 == lo:
                yield lo, hi, dot(lo, hi, aug_t(jnp.zeros_like(q_extra))) + diag_ref[...]
            else:
                yield r0, r0 + qb, dot(r0, r0 + qb, q_left if r0 < lo else q_right)

    def weighted_values(n, denom):
        return jnp.dot(vin_ref[0, 0], e_refs[n % 2][...], preferred_element_type=F32) / denom

    def finish_block(qi, out_t):
        o = (out_t[0] - lam * out_t[1]).T
        o = _rms(o, sg_ref[...]) * (1.0 - lambda_init)
        o_ref[0, qi * qb:(qi + 1) * qb, :] = o.astype(o_ref.dtype)

    @pl.when(small_scores)
    def _():
        q_side = {}
        for qi in range(n_blocks):
            out_t = []
            for n in (2 * qi, 2 * qi + 1):
                denom = 0.0
                for r0, r1, sc in score_pieces(n, q_side):
                    e = jnp.exp2(sc)
                    denom = denom + jnp.sum(e, axis=0, keepdims=True)
                    e_refs[n % 2][r0:r1, :] = e.astype(BF16)
                out_t.append(weighted_values(n, denom))
            finish_block(qi, out_t)

    @pl.when(jnp.logical_not(small_scores))
    def _():
        q_side = {}

        def scores(n):
            for r0, r1, sc in score_pieces(n, q_side):
                s_refs[n % 4][r0:r1, :] = sc

        def exponentials(n):
            sc = s_refs[n % 4][...]
            e = jnp.exp2(sc - jnp.max(sc, axis=0, keepdims=True))
            e_refs[n % 2][...] = e.astype(BF16)
            return jnp.sum(e, axis=0, keepdims=True)

        scores(0), scores(1)
        for qi in range(n_blocks):
            if qi + 1 < n_blocks:
                scores(2 * qi + 2), scores(2 * qi + 3)
            out_t = []
            for n in (2 * qi, 2 * qi + 1):
                out_t.append(weighted_values(n, exponentials(n)))
            finish_block(qi, out_t)


def _attention(qk3, v_t, lq1, lk1, lq2, lk2, subln_g, *, lambda_init, qb=512):
    b, s, w = qk3.shape
    hv = w // (2 * ATTN_HEADS)
    assert s <= 256 << POS_LOW_BITS and s % qb == 0
    kern = functools.partial(_attn_kernel, qb=qb, lambda_init=lambda_init)
    small = lambda shape: pl.BlockSpec(shape, lambda hi, bi: (0, 0))
    head = lambda off: pl.BlockSpec((1, s, hv), lambda hi, bi: (bi, 0, off + hi))
    return pl.pallas_call(
        kern,
        grid=(ATTN_HEADS, b),
        in_specs=[
            small(lq1.shape), small(lk1.shape), small(lq2.shape), small(lk2.shape), small(subln_g.shape),
            head(0), head(ATTN_HEADS), pl.BlockSpec((1, 1, hv, s), lambda hi, bi: (bi, hi, 0, 0)),
        ],
        out_specs=head(0),
        out_shape=jax.ShapeDtypeStruct((b, s, ATTN_HEADS * hv), BF16),
        scratch_shapes=[
            pltpu.VMEM((2, s, hv), F32), pltpu.VMEM((2, s, hv), F32), pltpu.VMEM((qb, qb), F32),
            pltpu.VMEM((2, s, hv), BF16),
            pltpu.VMEM((s, qb), F32), pltpu.VMEM((s, qb), F32), pltpu.VMEM((s, qb), F32), pltpu.VMEM((s, qb), F32),
            pltpu.VMEM((s, qb), BF16), pltpu.VMEM((s, qb), BF16),
        ],
        compiler_params=pltpu.CompilerParams(
            dimension_semantics=("arbitrary", "arbitrary"), vmem_limit_bytes=VMEM_LIMIT),
        name="diff_attn",
    )(lq1, lk1, lq2, lk2, subln_g, qk3, qk3, v_t)


S5_CHUNK = 8


def _s5_disc_kernel(lr_ref, li_ref, ldt_ref, bre_ref, bim_ref, cre_ref, cim_ref,
                    pbr_ref, pbi_ref, pcr_ref, pci_ref, akr_ref, aki_ref):
    cmul = lambda ar, ai, xr, xi: (ar * xr - ai * xi, ar * xi + ai * xr)

    lr, li = lr_ref[...], li_ref[...]
    dt = jnp.exp(ldt_ref[...])
    mag = jnp.exp(lr * dt)
    ang = li * dt
    ar, ai = mag * jnp.cos(ang), mag * jnp.sin(ang)
    nr = ar - 1.0
    den = lr * lr + li * li
    cr = (nr * lr + ai * li) / den
    ci = (ai * lr - nr * li) / den
    b_re, b_im = bre_ref[...], bim_ref[...]
    pr, pi = cr * b_re - ci * b_im, cr * b_im + ci * b_re
    kr, ki = ar, ai
    for n in range(pbr_ref.shape[0]):
        pbr_ref[n] = pr
        pbi_ref[n] = pi
        pr, pi = cmul(ar, ai, pr, pi)
        if n > 0:
            kr, ki = cmul(ar, ai, kr, ki)
    akr_ref[...] = kr
    aki_ref[...] = ki

    qr, qi = cre_ref[...], cim_ref[...]
    for n in range(pcr_ref.shape[0]):
        pcr_ref[n] = qr
        pci_ref[n] = -qi
        qr, qi = cmul(ar, ai, qr, qi)


def _s5_discretise(lam_re, lam_im, log_dt, b_re, b_im, c_re, c_im):
    nd, g, p, hc = b_re.shape
    nt = g // S5_SLAB_GROUPS
    rows, width = nd * nt * hc, S5_SLAB_GROUPS * p
    arrange = lambda t: (jnp.broadcast_to(t.astype(F32), (nd, g, hc, p)).reshape(nd, nt, S5_SLAB_GROUPS, hc, p)
                         .transpose(0, 1, 3, 2, 4).reshape(rows, width))
    args = (arrange(lam_re[:, :, None, :]), arrange(lam_im[:, :, None, :]), arrange(log_dt[..., None, None]),
            arrange(jnp.swapaxes(b_re, 2, 3)), arrange(jnp.swapaxes(b_im, 2, 3)), arrange(c_re), arrange(c_im))
    stack = lambda n: jax.ShapeDtypeStruct((n, rows, width), F32)
    one = jax.ShapeDtypeStruct((rows, width), F32)
    pbr, pbi, pcr, pci, akr, aki = pl.pallas_call(
        _s5_disc_kernel,
        out_shape=[stack(S5_CHUNK), stack(S5_CHUNK), stack(S5_CHUNK + 1), stack(S5_CHUNK + 1), one, one],
        name="s5_disc")(*args)
    tiles = lambda t: t.reshape(t.shape[0], nd, nt, hc, width)
    ak = jnp.stack([akr, aki]).reshape(2, nd, nt, hc, width)[:, :, :, 0]
    ak = ak.transpose(2, 1, 0, 3).reshape(nt, 2 * nd, width)
    return tiles(pbr), tiles(pbi), tiles(pcr), tiles(pci), ak


def _s5_weights_kernel(pbr_ref, pbi_ref, pcr_ref, pci_ref, win_ref, wcat_ref):
    chunk, nd, hc, half = pbr_ref.shape[0], pbr_ref.shape[1], pbr_ref.shape[3], pbr_ref.shape[4]
    blk = S5_SLAB_GROUPS * hc
    p = half // S5_SLAB_GROUPS
    same_group = (lax.broadcasted_iota(jnp.int32, (blk, half), 0) // hc
                  == lax.broadcasted_iota(jnp.int32, (blk, half), 1) // p)

    def expand(re, im):
        grid = lambda t: jnp.where(same_group, jnp.tile(t, (S5_SLAB_GROUPS, 1)), 0.0)
        return jnp.concatenate([grid(re), grid(im)], axis=1)

    def split_dot(a, b):
        a_hi, b_hi = a.astype(BF16), b.astype(BF16)
        a_lo, b_lo = (a - a_hi.astype(F32)).astype(BF16), (b - b_hi.astype(F32)).astype(BF16)
        return jnp.dot(jnp.concatenate([a_hi, a_hi, a_lo], axis=1), jnp.concatenate([b_hi, b_lo, b_hi], axis=0),
                       preferred_element_type=F32)

    n_state = 2 * half
    for o in range(pbr_ref.shape[2]):
        wall = [[expand(pbr_ref[n, d, o], pbi_ref[n, d, o]) for n in range(chunk)] for d in range(nd)]
        mall = [[expand(pcr_ref[n, d, o], pci_ref[n, d, o]).T for n in range(chunk + 1)] for d in range(nd)]

        for d in range(nd):
            for j in range(chunk):
                n_in = chunk - 1 - j if d == 0 else j
                win_ref[d, o, j * blk:(j + 1) * blk, :] = wall[d][n_in].astype(win_ref.dtype)
            n_out = [j + 1 if d == 0 else chunk - j for j in range(chunk)]
            r0 = chunk * blk + d * n_state
            wcat_ref[o, r0:r0 + n_state, :] = jnp.concatenate([mall[d][n] for n in n_out],
                                                              axis=1).astype(wcat_ref.dtype)

        resp = [[split_dot(wall[d][n], mall[d][0]) for n in range(chunk)] for d in range(nd)]
        both = resp[0][0] + resp[1][0]
        for j in range(chunk):
            row = [resp[0][jp - j] if jp > j else resp[1][j - jp] if jp < j else both for jp in range(chunk)]
            wcat_ref[o, j * blk:(j + 1) * blk, :] = jnp.concatenate(row, axis=1).astype(wcat_ref.dtype)


def _s5_chunk_weights(pbr, pbi, pcr, pci, *, slabs_per_step=4):
    chunk, nd, nt, hc, half = pbr.shape
    width, n_state = chunk * S5_SLAB_GROUPS * hc, 2 * half
    sps = slabs_per_step
    spec = lambda t: pl.BlockSpec((t.shape[0], nd, sps, hc, half), lambda o: (0, 0, o, 0, 0))
    return pl.pallas_call(
        _s5_weights_kernel,
        grid=(nt // sps,),
        in_specs=[spec(pbr), spec(pbi), spec(pcr), spec(pci)],
        out_specs=[pl.BlockSpec((nd, sps, width, n_state), lambda o: (0, o, 0, 0)),
                   pl.BlockSpec((sps, width + nd * n_state, width), lambda o: (o, 0, 0))],
        out_shape=[jax.ShapeDtypeStruct((nd, nt, width, n_state), BF16),
                   jax.ShapeDtypeStruct((nt, width + nd * n_state, width), BF16)],
        compiler_params=pltpu.CompilerParams(dimension_semantics=("arbitrary",), vmem_limit_bytes=VMEM_LIMIT),
        name="s5_weights",
    )(pbr, pbi, pcr, pci)


def _s5_chunk_kernel(u_ref, win_ref, wcat_ref, ak_ref, dsk_ref, y_ref, xf_ref, xb_ref, s0_ref, s1_ref, *, nb, rb):
    n_sl, n_rows, half = u_ref.shape[0], u_ref.shape[1], ak_ref.shape[2]
    n_blk, pair = n_rows // rb, 2 * nb
    s_refs = (s0_ref, s1_ref)
    order = {0: list(range(n_blk)), 1: list(reversed(range(n_blk)))}

    def increments(d, i):
        blk = order[d][i]
        for sl in range(n_sl):
            s_refs[i % 2][sl] = jnp.dot(u_ref[sl, blk * rb:(blk + 1) * rb, :].astype(BF16), win_ref[d, sl],
                                        preferred_element_type=F32)

    def scan_block(d, i, xs):
        blk, s_ref, x_ref = order[d][i], s_refs[i % 2], (xf_ref, xb_ref)[d]
        ak = [(jnp.broadcast_to(ak_ref[sl, 2 * d:2 * d + 1, :], (nb, half)),
               jnp.broadcast_to(ak_ref[sl, 2 * d + 1:2 * d + 2, :], (nb, half))) for sl in range(n_sl)]

        def advance(sl, x, rows):
            ar, ai = ak[sl]
            return (ar * x[0] - ai * x[1] + s_ref[sl, rows, :half], ar * x[1] + ai * x[0] + s_ref[sl, rows, half:])

        def body(it, xs):
            r0 = pl.multiple_of((it if d == 0 else rb // pair - 1 - it) * pair, pair)
            early, late = pl.ds(r0, nb), pl.ds(r0 + nb, nb)
            dst = pl.ds(pl.multiple_of(blk * rb + r0, pair), pair)
            out = []
            for sl, x in enumerate(xs):
                x1 = advance(sl, x, early if d == 0 else late)
                x2 = advance(sl, x1, late if d == 0 else early)
                enter_early, enter_late = (x, x1) if d == 0 else (x1, x)
                x_ref[sl, dst, :half] = jnp.concatenate([enter_early[0], enter_late[0]], axis=0).astype(BF16)
                x_ref[sl, dst, half:] = jnp.concatenate([enter_early[1], enter_late[1]], axis=0).astype(BF16)
                out.append(x2)
            return tuple(out)

        return lax.fori_loop(0, rb // pair, body, xs, unroll=True)

    zero = jnp.zeros((nb, half), F32)
    for d in range(2):
        xs = ((zero, zero),) * n_sl
        increments(d, 0)
        for i in range(n_blk):
            if i + 1 < n_blk:
                increments(d, i + 1)
            xs = scan_block(d, i, xs)

    for blk in range(n_blk):
        rows = slice(blk * rb, (blk + 1) * rb)
        for sl in range(n_sl):
            u = u_ref[sl, rows, :]
            lhs = jnp.concatenate([u.astype(BF16), xf_ref[sl, rows, :], xb_ref[sl, rows, :]], axis=1)
            y_ref[sl, rows, :] = jnp.dot(lhs, wcat_ref[sl], preferred_element_type=F32) + dsk_ref[sl] * u


def _s5_chunk_scan(u_tiles, win, wcat, ak, dsk, *, nb, rb=256, slabs_per_step=4):
    nt, n_rows, width = u_tiles.shape
    n_state = win.shape[3]
    sps = slabs_per_step
    kern = functools.partial(_s5_chunk_kernel, nb=nb, rb=rb)
    return pl.pallas_call(
        kern,
        grid=(nt // sps,),
        in_specs=[
            pl.BlockSpec((sps, n_rows, width), lambda o: (o, 0, 0)),
            pl.BlockSpec((win.shape[0], sps, width, n_state), lambda o: (0, o, 0, 0)),
            pl.BlockSpec((sps,) + wcat.shape[1:], lambda o: (o, 0, 0)),
            pl.BlockSpec((sps,) + ak.shape[1:], lambda o: (o, 0, 0)),
            pl.BlockSpec((sps,) + dsk.shape[1:], lambda o: (o, 0, 0)),
        ],
        out_specs=pl.BlockSpec((sps, n_rows, width), lambda o: (o, 0, 0)),
        out_shape=jax.ShapeDtypeStruct(u_tiles.shape, F32),
        scratch_shapes=[
            pltpu.VMEM((sps, n_rows, n_state), BF16), pltpu.VMEM((sps, n_rows, n_state), BF16),
            pltpu.VMEM((sps, rb, n_state), F32), pltpu.VMEM((sps, rb, n_state), F32),
        ],
        compiler_params=pltpu.CompilerParams(
            dimension_semantics=("arbitrary",), vmem_limit_bytes=VMEM_LIMIT),
        name="s5_scan",
    )(u_tiles, win, wcat, ak, dsk)


def _post_kernel(x_ref, att_ref, ys_first_ref, ys_next_ref, wg_ref, bg_ref, wo_ref, g2_ref,
                 w1_ref, w2_ref, gf_ref, o_ref, y_ref, *, ff_chunk, final_norm):
    nb, tt, d = x_ref.shape
    aw = att_ref.shape[2]
    unslab = lambda ref: _from_slabs(ref, nb).reshape(nb * tt, -1)

    @pl.when(pl.program_id(0) == 0)
    def _():
        y_ref[...] = unslab(ys_first_ref)

    y = y_ref[...]
    z = 0.5 * y * (1.0 + lax.erf(y * (1.0 / math.sqrt(2.0))))
    gate = jax.nn.sigmoid(jnp.dot(z.astype(BF16), wg_ref[...], preferred_element_type=F32) + bg_ref[...])
    ssm = (z * gate).astype(BF16)
    h = (x_ref[...].reshape(nb * tt, d)
         + jnp.dot(att_ref[...].reshape(nb * tt, aw), wo_ref[:aw, :], preferred_element_type=F32)
         + jnp.dot(ssm, wo_ref[aw:, :], preferred_element_type=F32))
    hn2 = _rms(h, g2_ref[...]).astype(BF16)
    y_ref[...] = unslab(ys_next_ref)
    mlp = jnp.zeros_like(h)
    for f0 in range(0, w1_ref.shape[1], ff_chunk):
        a = jnp.maximum(jnp.dot(hn2, w1_ref[:, f0:f0 + ff_chunk].astype(BF16), preferred_element_type=F32), 0.0)
        mlp = mlp + jnp.dot((a * a).astype(BF16), w2_ref[f0:f0 + ff_chunk, :].astype(BF16),
                            preferred_element_type=F32)
    h = h + mlp
    o_ref[...] = (_rms(h, gf_ref[...]) if final_norm else h).reshape(nb, tt, d)


def _post(x3, att, y_slabs, w_glu, b_glu, w_out, g2, w1, w2, gf, *, final_norm, tt=64):
    nb, s, d = x3.shape
    aw = att.shape[2]
    n_slabs, _, slab_w = y_slabs.shape
    rows = lambda width: pl.BlockSpec((nb, tt, width), lambda i: (0, i, 0))
    const = lambda arr: pl.BlockSpec(arr.shape, lambda i: (0, 0), pipeline_mode=pl.Buffered(1))
    kern = functools.partial(_post_kernel, ff_chunk=1024, final_norm=final_norm)
    n_steps = s // tt
    slab_block = (n_slabs, tt // S5_CHUNK * nb, slab_w)
    return pl.pallas_call(
        kern,
        grid=(n_steps,),
        in_specs=[rows(d), rows(aw),
                  pl.BlockSpec(slab_block, lambda i: (0, 0, 0)),
                  pl.BlockSpec(slab_block, lambda i: (0, jnp.minimum(i + 1, n_steps - 1), 0)),
                  const(w_glu), const(b_glu), const(w_out), const(g2), const(w1), const(w2), const(gf)],
        out_specs=rows(d),
        out_shape=jax.ShapeDtypeStruct((nb, s, d), F32),
        scratch_shapes=[pltpu.VMEM((nb * tt, n_slabs * S5_SLAB_LANES), F32)],
        compiler_params=pltpu.CompilerParams(
            dimension_semantics=("arbitrary",), vmem_limit_bytes=VMEM_LIMIT),
        name="post_mlp",
    )(x3, att, y_slabs, y_slabs, w_glu, b_glu, w_out, g2, w1, w2, gf)


def kernel(x, norm1_g, w_in, lam_q1, lam_k1, lam_q2, lam_k2, subln_g, ssm_lam_re, ssm_lam_im, ssm_log_dt,
           ssm_b_re, ssm_b_im, ssm_c_re, ssm_c_im, ssm_d, w_glu, b_glu, w_out, norm2_g, w_mlp1, w_mlp2,
           final_g):
    bsz, s, d = x.shape
    depth = w_in.shape[0]
    ssm_w = ssm_d.shape[-1]
    attn_w = w_out.shape[1] - ssm_w
    qk_dim = attn_w // ATTN_HEADS // 2
    scale = float(qk_dim) ** -0.5 * LOG2_E

    h3 = x.astype(F32)
    for l in range(depth):
        lambda_init = 0.8 - 0.6 * math.exp(-0.3 * l)
        qk, v_t, u_slabs = _inproj(h3, norm1_g[l][None].astype(F32), w_in[l],
                                   q_width=attn_w, n_heads=ATTN_HEADS, ssm_width=ssm_w, scale=scale)

        row = lambda t: t[l][None].astype(F32)
        att = _attention(qk, v_t, row(lam_q1), row(lam_k1), row(lam_q2), row(lam_k2), row(subln_g),
                         lambda_init=lambda_init)

        pbr, pbi, pcr, pci, ak_slabs = _s5_discretise(ssm_lam_re[l], ssm_lam_im[l], ssm_log_dt[l],
                                                      ssm_b_re[l], ssm_b_im[l], ssm_c_re[l], ssm_c_im[l])
        win, wcat = _s5_chunk_weights(pbr, pbi, pcr, pci)
        dsk_slabs = jnp.tile(ssm_d[l].astype(F32).reshape(-1, 1, S5_SLAB_LANES), (1, 1, S5_CHUNK))
        y_slabs = _s5_chunk_scan(u_slabs, win, wcat, ak_slabs, dsk_slabs, nb=bsz)

        h3 = _post(h3, att, y_slabs, w_glu[l].astype(BF16), row(b_glu), w_out[l].astype(BF16), row(norm2_g),
                   w_mlp1[l], w_mlp2[l], final_g[None].astype(F32),
                   final_norm=(l == depth - 1))
    return h3.astype(x.dtype)
```

```python
import functools
import math

import jax
import jax.numpy as jnp
from jax import lax
from jax.experimental import pallas as pl
from jax.experimental.pallas import tpu as pltpu

F32 = jnp.float32
BF16 = jnp.bfloat16

EPS = 1e-6
ATTN_HEADS = 4
SSM_GROUP = 16
SSM_STATE = 64
S5_SLAB_GROUPS = 2
S5_SLAB_LANES = S5_SLAB_GROUPS * SSM_GROUP
VMEM_LIMIT = 56 * 1024 * 1024


def _rms(x, g):
    return x * lax.rsqrt(jnp.mean(x * x, axis=-1, keepdims=True) + EPS) * g


def _to_slabs(u3):
    nb, tt, w = u3.shape
    n_ch = tt // S5_CHUNK
    t = jnp.transpose(u3.reshape(nb, n_ch, S5_CHUNK, w), (1, 2, 0, 3))
    slabs = []
    for q in range(w // S5_SLAB_LANES):
        lanes = slice(q * S5_SLAB_LANES, (q + 1) * S5_SLAB_LANES)
        rows = [jnp.concatenate([t[c, j][:, lanes] for j in range(S5_CHUNK)], axis=1) for c in range(n_ch)]
        slabs.append(jnp.concatenate(rows, axis=0))
    return slabs


def _from_slabs(slab_ref, nb):
    n_slabs, rows, _ = slab_ref.shape
    n_ch = rows // nb
    per_chunk = []
    for c in range(n_ch):
        steps = [jnp.concatenate([slab_ref[q, c * nb:(c + 1) * nb, j * S5_SLAB_LANES:(j + 1) * S5_SLAB_LANES]
                                  for q in range(n_slabs)], axis=1) for j in range(S5_CHUNK)]
        per_chunk.append(jnp.stack(steps, axis=0))
    t = jnp.stack(per_chunk, axis=0)
    return jnp.transpose(t, (2, 0, 1, 3)).reshape(nb, n_ch * S5_CHUNK, n_slabs * S5_SLAB_LANES)


def _inproj_kernel(x_ref, g_ref, w_ref, qk_ref, vt_ref, us_ref, *, q_width, scale, chunk):
    nb, tt, d = x_ref.shape
    qk_width, n_heads, hv = qk_ref.shape[2], vt_ref.shape[1], vt_ref.shape[2]
    v_width = n_heads * hv
    hn = _rms(x_ref[...].reshape(nb * tt, d), g_ref[...]).astype(BF16)
    w = lambda c0, c1: w_ref[:, c0:c1].astype(BF16)
    u = jnp.dot(hn, w(qk_width + v_width, w_ref.shape[1]), preferred_element_type=F32)
    for q, slab in enumerate(_to_slabs(u.reshape(nb, tt, u.shape[1]))):
        us_ref[q] = slab
    v = jnp.dot(hn, w(qk_width, qk_width + v_width), preferred_element_type=F32)
    for b in range(nb):
        for hh in range(n_heads):
            vt_ref[b, hh] = v[b * tt:(b + 1) * tt, hh * hv:(hh + 1) * hv].T.astype(BF16)
    for c0 in range(0, qk_width, chunk):
        p = jnp.dot(hn, w(c0, c0 + chunk), preferred_element_type=F32)
        if c0 < q_width:
            p = p * scale
        qk_ref[:, :, c0:c0 + chunk] = p.astype(BF16).reshape(nb, tt, chunk)


def _inproj(x3, g1, w_in, *, q_width, n_heads, ssm_width, scale, tt=128):
    nb, s, d = x3.shape
    n_out = w_in.shape[1]
    v_width = n_out - 2 * q_width - ssm_width
    n_slabs = ssm_width // S5_SLAB_LANES
    kern = functools.partial(_inproj_kernel, q_width=q_width, scale=scale, chunk=512)
    return pl.pallas_call(
        kern,
        grid=(s // tt,),
        in_specs=[
            pl.BlockSpec((nb, tt, d), lambda i: (0, i, 0)),
            pl.BlockSpec((1, d), lambda i: (0, 0)),
            pl.BlockSpec((d, n_out), lambda i: (0, 0)),
        ],
        out_specs=[
            pl.BlockSpec((nb, tt, 2 * q_width), lambda i: (0, i, 0)),
            pl.BlockSpec((nb, n_heads, v_width // n_heads, tt), lambda i: (0, 0, 0, i)),
            pl.BlockSpec((n_slabs, tt // S5_CHUNK * nb, S5_CHUNK * S5_SLAB_LANES), lambda i: (0, i, 0)),
        ],
        out_shape=[
            jax.ShapeDtypeStruct((nb, s, 2 * q_width), BF16),
            jax.ShapeDtypeStruct((nb, n_heads, v_width // n_heads, s), BF16),
            jax.ShapeDtypeStruct((n_slabs, s // S5_CHUNK * nb, S5_CHUNK * S5_SLAB_LANES), F32),
        ],
        compiler_params=pltpu.CompilerParams(
            dimension_semantics=("arbitrary",), vmem_limit_bytes=VMEM_LIMIT),
        name="inproj",
    )(x3, g1, w_in)


POS_LOW_BITS = 3
LOG2_E = 1.4426950408889634
SAFE_LOG2_SCORE = 96.0
SAFE_VALUE = 2.0 ** 16
NORM_FOLD = 4


def _attn_kernel(lq1_ref, lk1_ref, lq2_ref, lk2_ref, sg_ref, q_ref, k_ref, vin_ref, o_ref,
                 kex_ref, qex_ref, diag_ref, kx_ref, s0_ref, s1_ref, s2_ref, s3_ref, e0_ref, e1_ref,
                 *, qb, lambda_init):
    h = pl.program_id(0)
    s_refs, e_refs = (s0_ref, s1_ref, s2_ref, s3_ref), (e0_ref, e1_ref)
    s_len, hv = k_ref.shape[1], k_ref.shape[2]
    dh = hv // 2

    lam = (jnp.exp(jnp.sum(lq1_ref[...] * lk1_ref[...], axis=-1, keepdims=True))
           - jnp.exp(jnp.sum(lq2_ref[...] * lk2_ref[...], axis=-1, keepdims=True))
           + lambda_init)

    lane = lax.broadcasted_iota(jnp.int32, (1, hv), 1)
    data_lanes = (lane < dh, lane >= dh)

    @pl.when(pl.program_id(1) == 0)
    def _():
        slope = jnp.float32(2.0 ** -8.0)
        for hh in range(ATTN_HEADS - 1):
            slope = jnp.where(h == hh, jnp.float32(2.0 ** (-8.0 * (hh + 1) / ATTN_HEADS)), slope)
        c2 = jnp.full((1, 1), slope, F32) * LOG2_E
        c2_parts = []
        rest = c2
        for _ in range(3):
            part = rest.astype(BF16).astype(F32)
            c2_parts.append(part)
            rest = rest - part

        def extras(vals):
            out = 0.0
            for e, val in enumerate(vals):
                out = jnp.where(lane == dh + e, val, out)
            return out

        pos = lax.broadcasted_iota(jnp.int32, (s_len, 1), 0)
        pos_lo = pos & ((1 << POS_LOW_BITS) - 1)
        p_hi, p_lo = (pos - pos_lo).astype(F32), pos_lo.astype(F32)
        kex = extras(c2_parts + c2_parts + [p_hi] * 3 + [p_lo] * 3)
        qex = extras([-p_hi] * 3 + [-p_lo] * 3 + c2_parts + c2_parts)
        kex_ref[0], kex_ref[1] = kex, pltpu.roll(kex, dh, 1)
        qex_ref[0], qex_ref[1] = qex, pltpu.roll(qex, dh, 1)
        off = lax.broadcasted_iota(jnp.int32, (qb, qb), 0) - lax.broadcasted_iota(jnp.int32, (qb, qb), 1)
        diag_ref[...] = jnp.abs(off).astype(F32) * (-c2)

    kf = k_ref[0].astype(F32)
    for c in range(2):
        kx_ref[c] = jnp.where(data_lanes[c], kf, kex_ref[c]).astype(BF16)

    def norm2_bound(t):
        sq = (t * t).reshape(NORM_FOLD, s_len // NORM_FOLD, hv)
        return jnp.max(jnp.sum(jnp.max(sq, axis=0), axis=1, keepdims=True), axis=0, keepdims=True)

    v_max = jnp.max(jnp.max(jnp.abs(vin_ref[0, 0].astype(F32)), axis=1, keepdims=True), axis=0, keepdims=True)
    small_scores = jnp.logical_and(
        (norm2_bound(q_ref[0].astype(F32)) * norm2_bound(kf))[0, 0] <= SAFE_LOG2_SCORE ** 2,
        v_max[0, 0] <= SAFE_VALUE)

    n_blocks = s_len // qb

    def score_pieces(n, q_side):
        qi, c = divmod(n, 2)
        lo, hi = qi * qb, (qi + 1) * qb
        if qi not in q_side:
            q_side[qi] = q_ref[0, lo:hi, :].astype(F32)
        qf = q_side[qi]
        aug_t = lambda ex: jnp.where(data_lanes[c], qf, ex).T.astype(BF16)
        dot = lambda r0, r1, qt: jnp.dot(kx_ref[c, r0:r1, :], qt, preferred_element_type=F32)
        q_extra = qex_ref[c, lo:hi, :]
        q_left, q_right = aug_t(q_extra), aug_t(-q_extra)
        for r0 in range(0, s_len, qb):
            if r0 == lo:
                yield lo, hi, dot(lo, hi, aug_t(jnp.zeros_like(q_extra))) + diag_ref[...]
            else:
                yield r0, r0 + qb, dot(r0, r0 + qb, q_left if r0 < lo else q_right)

    def weighted_values(n, denom):
        return jnp.dot(vin_ref[0, 0], e_refs[n % 2][...], preferred_element_type=F32) / denom

    def finish_block(qi, out_t):
        o = (out_t[0] - lam * out_t[1]).T
        o = _rms(o, sg_ref[...]) * (1.0 - lambda_init)
        o_ref[0, qi * qb:(qi + 1) * qb, :] = o.astype(o_ref.dtype)

    @pl.when(small_scores)
    def _():
        q_side = {}
        for qi in range(n_blocks):
            out_t = []
            for n in (2 * qi, 2 * qi + 1):
                denom = 0.0
                for r0, r1, sc in score_pieces(n, q_side):
                    e = jnp.exp2(sc)
                    denom = denom + jnp.sum(e, axis=0, keepdims=True)
                    e_refs[n % 2][r0:r1, :] = e.astype(BF16)
                out_t.append(weighted_values(n, denom))
            finish_block(qi, out_t)

    @pl.when(jnp.logical_not(small_scores))
    def _():
        q_side = {}

        def scores(n):
            for r0, r1, sc in score_pieces(n, q_side):
                s_refs[n % 4][r0:r1, :] = sc

        def exponentials(n):
            sc = s_refs[n % 4][...]
            e = jnp.exp2(sc - jnp.max(sc, axis=0, keepdims=True))
            e_refs[n % 2][...] = e.astype(BF16)
            return jnp.sum(e, axis=0, keepdims=True)

        scores(0), scores(1)
        for qi in range(n_blocks):
            if qi + 1 < n_blocks:
                scores(2 * qi + 2), scores(2 * qi + 3)
            out_t = []
            for n in (2 * qi, 2 * qi + 1):
                out_t.append(weighted_values(n, exponentials(n)))
            finish_block(qi, out_t)


def _attention(qk3, v_t, lq1, lk1, lq2, lk2, subln_g, *, lambda_init, qb=512):
    b, s, w = qk3.shape
    hv = w // (2 * ATTN_HEADS)
    assert s <= 256 << POS_LOW_BITS and s % qb == 0
    kern = functools.partial(_attn_kernel, qb=qb, lambda_init=lambda_init)
    small = lambda shape: pl.BlockSpec(shape, lambda hi, bi: (0, 0))
    head = lambda off: pl.BlockSpec((1, s, hv), lambda hi, bi: (bi, 0, off + hi))
    return pl.pallas_call(
        kern,
        grid=(ATTN_HEADS, b),
        in_specs=[
            small(lq1.shape), small(lk1.shape), small(lq2.shape), small(lk2.shape), small(subln_g.shape),
            head(0), head(ATTN_HEADS), pl.BlockSpec((1, 1, hv, s), lambda hi, bi: (bi, hi, 0, 0)),
        ],
        out_specs=head(0),
        out_shape=jax.ShapeDtypeStruct((b, s, ATTN_HEADS * hv), BF16),
        scratch_shapes=[
            pltpu.VMEM((2, s, hv), F32), pltpu.VMEM((2, s, hv), F32), pltpu.VMEM((qb, qb), F32),
            pltpu.VMEM((2, s, hv), BF16),
            pltpu.VMEM((s, qb), F32), pltpu.VMEM((s, qb), F32), pltpu.VMEM((s, qb), F32), pltpu.VMEM((s, qb), F32),
            pltpu.VMEM((s, qb), BF16), pltpu.VMEM((s, qb), BF16),
        ],
        compiler_params=pltpu.CompilerParams(
            dimension_semantics=("arbitrary", "arbitrary"), vmem_limit_bytes=VMEM_LIMIT),
        name="diff_attn",
    )(lq1, lk1, lq2, lk2, subln_g, qk3, qk3, v_t)


S5_CHUNK = 8


def _s5_disc_kernel(par_ref, pbr_ref, pbi_ref, pcr_ref, pci_ref, akr_ref, aki_ref):
    cmul = lambda ar, ai, xr, xi: (ar * xr - ai * xi, ar * xi + ai * xr)
    lr_ref, li_ref, ldt_ref, bre_ref, bim_ref, cre_ref, cim_ref = (par_ref.at[i] for i in range(7))

    lr, li = lr_ref[...], li_ref[...]
    dt = jnp.exp(ldt_ref[...])
    mag = jnp.exp(lr * dt)
    ang = li * dt
    ar, ai = mag * jnp.cos(ang), mag * jnp.sin(ang)
    nr = ar - 1.0
    den = lr * lr + li * li
    cr = (nr * lr + ai * li) / den
    ci = (ai * lr - nr * li) / den
    b_re, b_im = bre_ref[...], bim_ref[...]
    pr, pi = cr * b_re - ci * b_im, cr * b_im + ci * b_re
    kr, ki = ar, ai
    for n in range(pbr_ref.shape[0]):
        pbr_ref[n] = pr
        pbi_ref[n] = pi
        pr, pi = cmul(ar, ai, pr, pi)
        if n > 0:
            kr, ki = cmul(ar, ai, kr, ki)
    akr_ref[...] = kr
    aki_ref[...] = ki

    qr, qi = cre_ref[...], cim_ref[...]
    for n in range(pcr_ref.shape[0]):
        pcr_ref[n] = qr
        pci_ref[n] = -qi
        qr, qi = cmul(ar, ai, qr, qi)


def _s5_discretise(lam_re, lam_im, log_dt, b_re, b_im, c_re, c_im):
    nd, g, p, hc = b_re.shape
    nt = g // S5_SLAB_GROUPS
    rows, width = nd * nt * hc, S5_SLAB_GROUPS * p
    full = lambda t: jnp.broadcast_to(t.astype(F32), (nd, g, hc, p))
    params = jnp.stack([full(lam_re[:, :, None, :]), full(lam_im[:, :, None, :]), full(log_dt[..., None, None]),
                        full(jnp.swapaxes(b_re, 2, 3)), full(jnp.swapaxes(b_im, 2, 3)), full(c_re), full(c_im)])
    params = (params.reshape(7, nd, nt, S5_SLAB_GROUPS, hc, p).transpose(0, 1, 2, 4, 3, 5)
              .reshape(7, rows, width))
    stack = lambda n: jax.ShapeDtypeStruct((n, rows, width), F32)
    one = jax.ShapeDtypeStruct((rows, width), F32)
    pbr, pbi, pcr, pci, akr, aki = pl.pallas_call(
        _s5_disc_kernel,
        out_shape=[stack(S5_CHUNK), stack(S5_CHUNK), stack(S5_CHUNK + 1), stack(S5_CHUNK + 1), one, one],
        name="s5_disc")(params)
    tiles = lambda t: t.reshape(t.shape[0], nd, nt, hc, width)
    ak = jnp.stack([akr, aki]).reshape(2, nd, nt, hc, width)[:, :, :, 0]
    ak = ak.transpose(2, 1, 0, 3).reshape(nt, 2 * nd, width)
    return tiles(pbr), tiles(pbi), tiles(pcr), tiles(pci), ak


def _s5_weights_kernel(pbr_ref, pbi_ref, pcr_ref, pci_ref, win_ref, wcat_ref):
    chunk, nd, hc, half = pbr_ref.shape[0], pbr_ref.shape[1], pbr_ref.shape[3], pbr_ref.shape[4]
    blk = S5_SLAB_GROUPS * hc
    p = half // S5_SLAB_GROUPS
    same_group = (lax.broadcasted_iota(jnp.int32, (blk, half), 0) // hc
                  == lax.broadcasted_iota(jnp.int32, (blk, half), 1) // p)

    def expand(re, im):
        grid = lambda t: jnp.where(same_group, jnp.tile(t, (S5_SLAB_GROUPS, 1)), 0.0)
        return jnp.concatenate([grid(re), grid(im)], axis=1)

    def split_dot(a, b):
        a_hi, b_hi = a.astype(BF16), b.astype(BF16)
        a_lo, b_lo = (a - a_hi.astype(F32)).astype(BF16), (b - b_hi.astype(F32)).astype(BF16)
        return jnp.dot(jnp.concatenate([a_hi, a_hi, a_lo], axis=1), jnp.concatenate([b_hi, b_lo, b_hi], axis=0),
                       preferred_element_type=F32)

    n_state = 2 * half
    for o in range(pbr_ref.shape[2]):
        wall = [[expand(pbr_ref[n, d, o], pbi_ref[n, d, o]) for n in range(chunk)] for d in range(nd)]
        mall = [[expand(pcr_ref[n, d, o], pci_ref[n, d, o]).T for n in range(chunk + 1)] for d in range(nd)]

        for d in range(nd):
            for j in range(chunk):
                n_in = chunk - 1 - j if d == 0 else j
                win_ref[d, o, j * blk:(j + 1) * blk, :] = wall[d][n_in].astype(win_ref.dtype)
            n_out = [j + 1 if d == 0 else chunk - j for j in range(chunk)]
            r0 = chunk * blk + d * n_state
            wcat_ref[o, r0:r0 + n_state, :] = jnp.concatenate([mall[d][n] for n in n_out],
                                                              axis=1).astype(wcat_ref.dtype)

        resp = [[split_dot(wall[d][n], mall[d][0]) for n in range(chunk)] for d in range(nd)]
        both = resp[0][0] + resp[1][0]
        for j in range(chunk):
            row = [resp[0][jp - j] if jp > j else resp[1][j - jp] if jp < j else both for jp in range(chunk)]
            wcat_ref[o, j * blk:(j + 1) * blk, :] = jnp.concatenate(row, axis=1).astype(wcat_ref.dtype)


def _s5_chunk_weights(pbr, pbi, pcr, pci, *, slabs_per_step=4):
    chunk, nd, nt, hc, half = pbr.shape
    width, n_state = chunk * S5_SLAB_GROUPS * hc, 2 * half
    sps = slabs_per_step
    spec = lambda t: pl.BlockSpec((t.shape[0], nd, sps, hc, half), lambda o: (0, 0, o, 0, 0))
    return pl.pallas_call(
        _s5_weights_kernel,
        grid=(nt // sps,),
        in_specs=[spec(pbr), spec(pbi), spec(pcr), spec(pci)],
        out_specs=[pl.BlockSpec((nd, sps, width, n_state), lambda o: (0, o, 0, 0)),
                   pl.BlockSpec((sps, width + nd * n_state, width), lambda o: (o, 0, 0))],
        out_shape=[jax.ShapeDtypeStruct((nd, nt, width, n_state), BF16),
                   jax.ShapeDtypeStruct((nt, width + nd * n_state, width), BF16)],
        compiler_params=pltpu.CompilerParams(dimension_semantics=("arbitrary",), vmem_limit_bytes=VMEM_LIMIT),
        name="s5_weights",
    )(pbr, pbi, pcr, pci)


def _s5_chunk_kernel(u_ref, win_ref, wcat_ref, ak_ref, dsk_ref, y_ref, xf_ref, xb_ref, s0_ref, s1_ref, *, nb, rb):
    n_sl, n_rows, half = u_ref.shape[0], u_ref.shape[1], ak_ref.shape[2]
    n_blk, pair = n_rows // rb, 2 * nb
    s_refs = (s0_ref, s1_ref)
    order = {0: list(range(n_blk)), 1: list(reversed(range(n_blk)))}

    def increments(d, i):
        blk = order[d][i]
        for sl in range(n_sl):
            s_refs[i % 2][sl] = jnp.dot(u_ref[sl, blk * rb:(blk + 1) * rb, :].astype(BF16), win_ref[d, sl],
                                        preferred_element_type=F32)

    def scan_block(d, i, xs):
        blk, s_ref, x_ref = order[d][i], s_refs[i % 2], (xf_ref, xb_ref)[d]
        ak = [(jnp.broadcast_to(ak_ref[sl, 2 * d:2 * d + 1, :], (nb, half)),
               jnp.broadcast_to(ak_ref[sl, 2 * d + 1:2 * d + 2, :], (nb, half))) for sl in range(n_sl)]

        def advance(sl, x, rows):
            ar, ai = ak[sl]
            return (ar * x[0] - ai * x[1] + s_ref[sl, rows, :half], ar * x[1] + ai * x[0] + s_ref[sl, rows, half:])

        def body(it, xs):
            r0 = pl.multiple_of((it if d == 0 else rb // pair - 1 - it) * pair, pair)
            early, late = pl.ds(r0, nb), pl.ds(r0 + nb, nb)
            dst = pl.ds(pl.multiple_of(blk * rb + r0, pair), pair)
            out = []
            for sl, x in enumerate(xs):
                x1 = advance(sl, x, early if d == 0 else late)
                x2 = advance(sl, x1, late if d == 0 else early)
                enter_early, enter_late = (x, x1) if d == 0 else (x1, x)
                x_ref[sl, dst, :half] = jnp.concatenate([enter_early[0], enter_late[0]], axis=0).astype(BF16)
                x_ref[sl, dst, half:] = jnp.concatenate([enter_early[1], enter_late[1]], axis=0).astype(BF16)
                out.append(x2)
            return tuple(out)

        return lax.fori_loop(0, rb // pair, body, xs, unroll=True)

    zero = jnp.zeros((nb, half), F32)
    for d in range(2):
        xs = ((zero, zero),) * n_sl
        increments(d, 0)
        for i in range(n_blk):
            if i + 1 < n_blk:
                increments(d, i + 1)
            xs = scan_block(d, i, xs)

    for blk in range(n_blk):
        rows = slice(blk * rb, (blk + 1) * rb)
        for sl in range(n_sl):
            u = u_ref[sl, rows, :]
            lhs = jnp.concatenate([u.astype(BF16), xf_ref[sl, rows, :], xb_ref[sl, rows, :]], axis=1)
            y_ref[sl, rows, :] = jnp.dot(lhs, wcat_ref[sl], preferred_element_type=F32) + dsk_ref[sl] * u


def _s5_chunk_scan(u_tiles, win, wcat, ak, dsk, *, nb, rb=256, slabs_per_step=4):
    nt, n_rows, width = u_tiles.shape
    n_state = win.shape[3]
    sps = slabs_per_step
    kern = functools.partial(_s5_chunk_kernel, nb=nb, rb=rb)
    return pl.pallas_call(
        kern,
        grid=(nt // sps,),
        in_specs=[
            pl.BlockSpec((sps, n_rows, width), lambda o: (o, 0, 0)),
            pl.BlockSpec((win.shape[0], sps, width, n_state), lambda o: (0, o, 0, 0)),
            pl.BlockSpec((sps,) + wcat.shape[1:], lambda o: (o, 0, 0)),
            pl.BlockSpec((sps,) + ak.shape[1:], lambda o: (o, 0, 0)),
            pl.BlockSpec((sps,) + dsk.shape[1:], lambda o: (o, 0, 0)),
        ],
        out_specs=pl.BlockSpec((sps, n_rows, width), lambda o: (o, 0, 0)),
        out_shape=jax.ShapeDtypeStruct(u_tiles.shape, F32),
        scratch_shapes=[
            pltpu.VMEM((sps, n_rows, n_state), BF16), pltpu.VMEM((sps, n_rows, n_state), BF16),
            pltpu.VMEM((sps, rb, n_state), F32), pltpu.VMEM((sps, rb, n_state), F32),
        ],
        compiler_params=pltpu.CompilerParams(
            dimension_semantics=("arbitrary",), vmem_limit_bytes=VMEM_LIMIT),
        name="s5_scan",
    )(u_tiles, win, wcat, ak, dsk)


def _post_kernel(x_ref, att_ref, ys_first_ref, ys_next_ref, wg_ref, bg_ref, wo_ref, g2_ref,
                 w1_ref, w2_ref, gf_ref, o_ref, y_ref, *, ff_chunk, final_norm):
    nb, tt, d = x_ref.shape
    aw = att_ref.shape[2]
    unslab = lambda ref: _from_slabs(ref, nb).reshape(nb * tt, -1)

    @pl.when(pl.program_id(0) == 0)
    def _():
        y_ref[...] = unslab(ys_first_ref)

    y = y_ref[...]
    z = 0.5 * y * (1.0 + lax.erf(y * (1.0 / math.sqrt(2.0))))
    gate = jax.nn.sigmoid(jnp.dot(z.astype(BF16), wg_ref[...].astype(BF16), preferred_element_type=F32)
                          + bg_ref[...])
    ssm = (z * gate).astype(BF16)
    h = (x_ref[...].reshape(nb * tt, d)
         + jnp.dot(att_ref[...].reshape(nb * tt, aw), wo_ref[:aw, :].astype(BF16), preferred_element_type=F32)
         + jnp.dot(ssm, wo_ref[aw:, :].astype(BF16), preferred_element_type=F32))
    hn2 = _rms(h, g2_ref[...]).astype(BF16)
    y_ref[...] = unslab(ys_next_ref)
    mlp = jnp.zeros_like(h)
    for f0 in range(0, w1_ref.shape[1], ff_chunk):
        a = jnp.maximum(jnp.dot(hn2, w1_ref[:, f0:f0 + ff_chunk].astype(BF16), preferred_element_type=F32), 0.0)
        mlp = mlp + jnp.dot((a * a).astype(BF16), w2_ref[f0:f0 + ff_chunk, :].astype(BF16),
                            preferred_element_type=F32)
    h = h + mlp
    o_ref[...] = (_rms(h, gf_ref[...]) if final_norm else h).reshape(nb, tt, d)


def _post(x3, att, y_slabs, w_glu, b_glu, w_out, g2, w1, w2, gf, *, final_norm, tt=64):
    nb, s, d = x3.shape
    aw = att.shape[2]
    n_slabs, _, slab_w = y_slabs.shape
    rows = lambda width: pl.BlockSpec((nb, tt, width), lambda i: (0, i, 0))
    const = lambda arr: pl.BlockSpec(arr.shape, lambda i: (0, 0), pipeline_mode=pl.Buffered(1))
    kern = functools.partial(_post_kernel, ff_chunk=1024, final_norm=final_norm)
    n_steps = s // tt
    slab_block = (n_slabs, tt // S5_CHUNK * nb, slab_w)
    return pl.pallas_call(
        kern,
        grid=(n_steps,),
        in_specs=[rows(d), rows(aw),
                  pl.BlockSpec(slab_block, lambda i: (0, 0, 0), pipeline_mode=pl.Buffered(1)),
                  pl.BlockSpec(slab_block, lambda i: (0, jnp.minimum(i + 1, n_steps - 1), 0)),
                  const(w_glu), const(b_glu), const(w_out), const(g2), const(w1), const(w2), const(gf)],
        out_specs=rows(d),
        out_shape=jax.ShapeDtypeStruct((nb, s, d), F32),
        scratch_shapes=[pltpu.VMEM((nb * tt, n_slabs * S5_SLAB_LANES), F32)],
        compiler_params=pltpu.CompilerParams(
            dimension_semantics=("arbitrary",), vmem_limit_bytes=VMEM_LIMIT),
        name="post_mlp",
    )(x3, att, y_slabs, y_slabs, w_glu, b_glu, w_out, g2, w1, w2, gf)


def kernel(x, norm1_g, w_in, lam_q1, lam_k1, lam_q2, lam_k2, subln_g, ssm_lam_re, ssm_lam_im, ssm_log_dt,
           ssm_b_re, ssm_b_im, ssm_c_re, ssm_c_im, ssm_d, w_glu, b_glu, w_out, norm2_g, w_mlp1, w_mlp2,
           final_g):
    bsz, s, d = x.shape
    depth = w_in.shape[0]
    ssm_w = ssm_d.shape[-1]
    attn_w = w_out.shape[1] - ssm_w
    qk_dim = attn_w // ATTN_HEADS // 2
    scale = float(qk_dim) ** -0.5 * LOG2_E

    h3 = x.astype(F32)
    for l in range(depth):
        lambda_init = 0.8 - 0.6 * math.exp(-0.3 * l)
        qk, v_t, u_slabs = _inproj(h3, norm1_g[l][None].astype(F32), w_in[l],
                                   q_width=attn_w, n_heads=ATTN_HEADS, ssm_width=ssm_w, scale=scale)

        row = lambda t: t[l][None].astype(F32)
        att = _attention(qk, v_t, row(lam_q1), row(lam_k1), row(lam_q2), row(lam_k2), row(subln_g),
                         lambda_init=lambda_init)

        pbr, pbi, pcr, pci, ak_slabs = _s5_discretise(ssm_lam_re[l], ssm_lam_im[l], ssm_log_dt[l],
                                                      ssm_b_re[l], ssm_b_im[l], ssm_c_re[l], ssm_c_im[l])
        win, wcat = _s5_chunk_weights(pbr, pbi, pcr, pci)
        dsk_slabs = jnp.tile(ssm_d[l].astype(F32).reshape(-1, 1, S5_SLAB_LANES), (1, 1, S5_CHUNK))
        y_slabs = _s5_chunk_scan(u_slabs, win, wcat, ak_slabs, dsk_slabs, nb=bsz)

        h3 = _post(h3, att, y_slabs, w_glu[l], row(b_glu), w_out[l], row(norm2_g),
                   w_mlp1[l], w_mlp2[l], final_g[None].astype(F32),
                   final_norm=(l == depth - 1))
    return h3.astype(x.dtype)
```

```python
import functools
import math

import jax
import jax.numpy as jnp
from jax import lax
from jax.experimental import pallas as pl
from jax.experimental.pallas import tpu as pltpu

F32 = jnp.float32
BF16 = jnp.bfloat16

EPS = 1e-6
ATTN_HEADS = 4
SSM_GROUP = 16
SSM_STATE = 64
S5_SLAB_GROUPS = 2
S5_SLAB_LANES = S5_SLAB_GROUPS * SSM_GROUP
VMEM_LIMIT = 56 * 1024 * 1024


def _rms(x, g):
    return x * lax.rsqrt(jnp.mean(x * x, axis=-1, keepdims=True) + EPS) * g


def _to_slabs(u3):
    nb, tt, w = u3.shape
    n_ch = tt // S5_CHUNK
    t = jnp.transpose(u3.reshape(nb, n_ch, S5_CHUNK, w), (1, 2, 0, 3))
    slabs = []
    for q in range(w // S5_SLAB_LANES):
        lanes = slice(q * S5_SLAB_LANES, (q + 1) * S5_SLAB_LANES)
        rows = [jnp.concatenate([t[c, j][:, lanes] for j in range(S5_CHUNK)], axis=1) for c in range(n_ch)]
        slabs.append(jnp.concatenate(rows, axis=0))
    return slabs


def _from_slabs(slab_ref, nb):
    n_slabs, rows, _ = slab_ref.shape
    n_ch = rows // nb
    per_chunk = []
    for c in range(n_ch):
        steps = [jnp.concatenate([slab_ref[q, c * nb:(c + 1) * nb, j * S5_SLAB_LANES:(j + 1) * S5_SLAB_LANES]
                                  for q in range(n_slabs)], axis=1) for j in range(S5_CHUNK)]
        per_chunk.append(jnp.stack(steps, axis=0))
    t = jnp.stack(per_chunk, axis=0)
    return jnp.transpose(t, (2, 0, 1, 3)).reshape(nb, n_ch * S5_CHUNK, n_slabs * S5_SLAB_LANES)


def _inproj_kernel(x_ref, g_ref, w_ref, qk_ref, vt_ref, us_ref, *, q_width, scale, chunk):
    nb, tt, d = x_ref.shape
    qk_width, n_heads, hv = qk_ref.shape[2], vt_ref.shape[1], vt_ref.shape[2]
    v_width = n_heads * hv
    hn = _rms(x_ref[...].reshape(nb * tt, d), g_ref[...]).astype(BF16)
    w = lambda c0, c1: w_ref[:, c0:c1].astype(BF16)
    u = jnp.dot(hn, w(qk_width + v_width, w_ref.shape[1]), preferred_element_type=F32)
    for q, slab in enumerate(_to_slabs(u.reshape(nb, tt, u.shape[1]))):
        us_ref[q] = slab
    v = jnp.dot(hn, w(qk_width, qk_width + v_width), preferred_element_type=F32)
    for b in range(nb):
        for hh in range(n_heads):
            vt_ref[b, hh] = v[b * tt:(b + 1) * tt, hh * hv:(hh + 1) * hv].T.astype(BF16)
    for c0 in range(0, qk_width, chunk):
        p = jnp.dot(hn, w(c0, c0 + chunk), preferred_element_type=F32)
        if c0 < q_width:
            p = p * scale
        qk_ref[:, :, c0:c0 + chunk] = p.astype(BF16).reshape(nb, tt, chunk)


def _inproj(x3, g1, w_in, *, q_width, n_heads, ssm_width, scale, tt=128):
    nb, s, d = x3.shape
    n_out = w_in.shape[1]
    v_width = n_out - 2 * q_width - ssm_width
    n_slabs = ssm_width // S5_SLAB_LANES
    kern = functools.partial(_inproj_kernel, q_width=q_width, scale=scale, chunk=512)
    return pl.pallas_call(
        kern,
        grid=(s // tt,),
        in_specs=[
            pl.BlockSpec((nb, tt, d), lambda i: (0, i, 0)),
            pl.BlockSpec((1, d), lambda i: (0, 0)),
            pl.BlockSpec((d, n_out), lambda i: (0, 0)),
        ],
        out_specs=[
            pl.BlockSpec((nb, tt, 2 * q_width), lambda i: (0, i, 0)),
            pl.BlockSpec((nb, n_heads, v_width // n_heads, tt), lambda i: (0, 0, 0, i)),
            pl.BlockSpec((n_slabs, tt // S5_CHUNK * nb, S5_CHUNK * S5_SLAB_LANES), lambda i: (0, i, 0)),
        ],
        out_shape=[
            jax.ShapeDtypeStruct((nb, s, 2 * q_width), BF16),
            jax.ShapeDtypeStruct((nb, n_heads, v_width // n_heads, s), BF16),
            jax.ShapeDtypeStruct((n_slabs, s // S5_CHUNK * nb, S5_CHUNK * S5_SLAB_LANES), F32),
        ],
        compiler_params=pltpu.CompilerParams(
            dimension_semantics=("arbitrary",), vmem_limit_bytes=VMEM_LIMIT),
        name="inproj",
    )(x3, g1, w_in)


POS_LOW_BITS = 3
LOG2_E = 1.4426950408889634
SAFE_LOG2_SCORE = 96.0
SAFE_VALUE = 2.0 ** 16
NORM_FOLD = 4


def _attn_kernel(lq1_ref, lk1_ref, lq2_ref, lk2_ref, sg_ref, q_ref, k_ref, vin_ref, o_ref,
                 kex_ref, qex_ref, diag_ref, kx_ref, s0_ref, s1_ref, s2_ref, s3_ref, e0_ref, e1_ref,
                 *, qb, lambda_init):
    h = pl.program_id(0)
    s_refs, e_refs = (s0_ref, s1_ref, s2_ref, s3_ref), (e0_ref, e1_ref)
    s_len, hv = k_ref.shape[1], k_ref.shape[2]
    dh = hv // 2

    lam = (jnp.exp(jnp.sum(lq1_ref[...] * lk1_ref[...], axis=-1, keepdims=True))
           - jnp.exp(jnp.sum(lq2_ref[...] * lk2_ref[...], axis=-1, keepdims=True))
           + lambda_init)

    lane = lax.broadcasted_iota(jnp.int32, (1, hv), 1)
    data_lanes = (lane < dh, lane >= dh)

    @pl.when(pl.program_id(1) == 0)
    def _():
        slope = jnp.float32(2.0 ** -8.0)
        for hh in range(ATTN_HEADS - 1):
            slope = jnp.where(h == hh, jnp.float32(2.0 ** (-8.0 * (hh + 1) / ATTN_HEADS)), slope)
        c2 = jnp.full((1, 1), slope, F32) * LOG2_E
        c2_parts = []
        rest = c2
        for _ in range(3):
            part = rest.astype(BF16).astype(F32)
            c2_parts.append(part)
            rest = rest - part

        def extras(vals):
            out = 0.0
            for e, val in enumerate(vals):
                out = jnp.where(lane == dh + e, val, out)
            return out

        pos = lax.broadcasted_iota(jnp.int32, (s_len, 1), 0)
        pos_lo = pos & ((1 << POS_LOW_BITS) - 1)
        p_hi, p_lo = (pos - pos_lo).astype(F32), pos_lo.astype(F32)
        kex = extras(c2_parts + c2_parts + [p_hi] * 3 + [p_lo] * 3)
        qex = extras([-p_hi] * 3 + [-p_lo] * 3 + c2_parts + c2_parts)
        kex_ref[0], kex_ref[1] = kex, pltpu.roll(kex, dh, 1)
        qex_ref[0], qex_ref[1] = qex, pltpu.roll(qex, dh, 1)
        off = lax.broadcasted_iota(jnp.int32, (qb, qb), 0) - lax.broadcasted_iota(jnp.int32, (qb, qb), 1)
        diag_ref[...] = jnp.abs(off).astype(F32) * (-c2)

    kf = k_ref[0].astype(F32)
    for c in range(2):
        kx_ref[c] = jnp.where(data_lanes[c], kf, kex_ref[c]).astype(BF16)

    def norm2_bound(t):
        sq = (t * t).reshape(NORM_FOLD, s_len // NORM_FOLD, hv)
        return jnp.max(jnp.sum(jnp.max(sq, axis=0), axis=1, keepdims=True), axis=0, keepdims=True)

    v_max = jnp.max(jnp.max(jnp.abs(vin_ref[0, 0].astype(F32)), axis=1, keepdims=True), axis=0, keepdims=True)
    small_scores = jnp.logical_and(
        (norm2_bound(q_ref[0].astype(F32)) * norm2_bound(kf))[0, 0] <= SAFE_LOG2_SCORE ** 2,
        v_max[0, 0] <= SAFE_VALUE)

    n_blocks = s_len // qb

    def score_pieces(n, q_side):
        qi, c = divmod(n, 2)
        lo, hi = qi * qb, (qi + 1) * qb
        if qi not in q_side:
            q_side[qi] = q_ref[0, lo:hi, :].astype(F32)
        qf = q_side[qi]
        aug_t = lambda ex: jnp.where(data_lanes[c], qf, ex).T.astype(BF16)
        dot = lambda r0, r1, qt: jnp.dot(kx_ref[c, r0:r1, :], qt, preferred_element_type=F32)
        q_extra = qex_ref[c, lo:hi, :]
        q_left, q_right = aug_t(q_extra), aug_t(-q_extra)
        for r0 in range(0, s_len, qb):
            if r0 == lo:
                yield lo, hi, dot(lo, hi, aug_t(jnp.zeros_like(q_extra))) + diag_ref[...]
            else:
                yield r0, r0 + qb, dot(r0, r0 + qb, q_left if r0 < lo else q_right)

    def weighted_values(n, denom):
        return jnp.dot(vin_ref[0, 0], e_refs[n % 2][...], preferred_element_type=F32) / denom

    def finish_block(qi, out_t):
        o = (out_t[0] - lam * out_t[1]).T
        o = _rms(o, sg_ref[...]) * (1.0 - lambda_init)
        o_ref[0, qi * qb:(qi + 1) * qb, :] = o.astype(o_ref.dtype)

    @pl.when(small_scores)
    def _():
        q_side = {}
        for qi in range(n_blocks):
            out_t = []
            for n in (2 * qi, 2 * qi + 1):
                denom = 0.0
                for r0, r1, sc in score_pieces(n, q_side):
                    e = jnp.exp2(sc)
                    denom = denom + jnp.sum(e, axis=0, keepdims=True)
                    e_refs[n % 2][r0:r1, :] = e.astype(BF16)
                out_t.append(weighted_values(n, denom))
            finish_block(qi, out_t)

    @pl.when(jnp.logical_not(small_scores))
    def _():
        q_side = {}

        def scores(n):
            for r0, r1, sc in score_pieces(n, q_side):
                s_refs[n % 4][r0:r1, :] = sc

        def exponentials(n):
            sc = s_refs[n % 4][...]
            e = jnp.exp2(sc - jnp.max(sc, axis=0, keepdims=True))
            e_refs[n % 2][...] = e.astype(BF16)
            return jnp.sum(e, axis=0, keepdims=True)

        scores(0), scores(1)
        for qi in range(n_blocks):
            if qi + 1 < n_blocks:
                scores(2 * qi + 2), scores(2 * qi + 3)
            out_t = []
            for n in (2 * qi, 2 * qi + 1):
                out_t.append(weighted_values(n, exponentials(n)))
            finish_block(qi, out_t)


def _attention(qk3, v_t, lq1, lk1, lq2, lk2, subln_g, *, lambda_init, qb=512):
    b, s, w = qk3.shape
    hv = w // (2 * ATTN_HEADS)
    assert s <= 256 << POS_LOW_BITS and s % qb == 0
    kern = functools.partial(_attn_kernel, qb=qb, lambda_init=lambda_init)
    small = lambda shape: pl.BlockSpec(shape, lambda hi, bi: (0, 0))
    head = lambda off: pl.BlockSpec((1, s, hv), lambda hi, bi: (bi, 0, off + hi))
    return pl.pallas_call(
        kern,
        grid=(ATTN_HEADS, b),
        in_specs=[
            small(lq1.shape), small(lk1.shape), small(lq2.shape), small(lk2.shape), small(subln_g.shape),
            head(0), head(ATTN_HEADS), pl.BlockSpec((1, 1, hv, s), lambda hi, bi: (bi, hi, 0, 0)),
        ],
        out_specs=head(0),
        out_shape=jax.ShapeDtypeStruct((b, s, ATTN_HEADS * hv), BF16),
        scratch_shapes=[
            pltpu.VMEM((2, s, hv), F32), pltpu.VMEM((2, s, hv), F32), pltpu.VMEM((qb, qb), F32),
            pltpu.VMEM((2, s, hv), BF16),
            pltpu.VMEM((s, qb), F32), pltpu.VMEM((s, qb), F32), pltpu.VMEM((s, qb), F32), pltpu.VMEM((s, qb), F32),
            pltpu.VMEM((s, qb), BF16), pltpu.VMEM((s, qb), BF16),
        ],
        compiler_params=pltpu.CompilerParams(
            dimension_semantics=("arbitrary", "arbitrary"), vmem_limit_bytes=VMEM_LIMIT),
        name="diff_attn",
    )(lq1, lk1, lq2, lk2, subln_g, qk3, qk3, v_t)


S5_CHUNK = 8


def _s5_disc_kernel(par_ref, pbr_ref, pbi_ref, pcr_ref, pci_ref, akr_ref, aki_ref):
    cmul = lambda ar, ai, xr, xi: (ar * xr - ai * xi, ar * xi + ai * xr)
    lr_ref, li_ref, ldt_ref, bre_ref, bim_ref, cre_ref, cim_ref = (par_ref.at[i] for i in range(7))

    lr, li = lr_ref[...], li_ref[...]
    dt = jnp.exp(ldt_ref[...])
    mag = jnp.exp(lr * dt)
    ang = li * dt
    ar, ai = mag * jnp.cos(ang), mag * jnp.sin(ang)
    nr = ar - 1.0
    den = lr * lr + li * li
    cr = (nr * lr + ai * li) / den
    ci = (ai * lr - nr * li) / den
    b_re, b_im = bre_ref[...], bim_ref[...]
    pr, pi = cr * b_re - ci * b_im, cr * b_im + ci * b_re
    kr, ki = ar, ai
    for n in range(pbr_ref.shape[0]):
        pbr_ref[n] = pr
        pbi_ref[n] = pi
        pr, pi = cmul(ar, ai, pr, pi)
        if n > 0:
            kr, ki = cmul(ar, ai, kr, ki)
    akr_ref[...] = kr
    aki_ref[...] = ki

    qr, qi = cre_ref[...], cim_ref[...]
    for n in range(pcr_ref.shape[0]):
        pcr_ref[n] = qr
        pci_ref[n] = -qi
        qr, qi = cmul(ar, ai, qr, qi)


def _s5_discretise(lam_re, lam_im, log_dt, b_re, b_im, c_re, c_im):
    nd, g, p, hc = b_re.shape
    nt = g // S5_SLAB_GROUPS
    rows, width = nd * nt * hc, S5_SLAB_GROUPS * p
    full = lambda t: jnp.broadcast_to(t.astype(F32), (nd, g, hc, p))
    params = jnp.stack([full(lam_re[:, :, None, :]), full(lam_im[:, :, None, :]), full(log_dt[..., None, None]),
                        full(jnp.swapaxes(b_re, 2, 3)), full(jnp.swapaxes(b_im, 2, 3)), full(c_re), full(c_im)])
    params = (params.reshape(7, nd, nt, S5_SLAB_GROUPS, hc, p).transpose(0, 1, 2, 4, 3, 5)
              .reshape(7, rows, width))
    stack = lambda n: jax.ShapeDtypeStruct((n, rows, width), F32)
    one = jax.ShapeDtypeStruct((rows, width), F32)
    pbr, pbi, pcr, pci, akr, aki = pl.pallas_call(
        _s5_disc_kernel,
        out_shape=[stack(S5_CHUNK), stack(S5_CHUNK), stack(S5_CHUNK + 1), stack(S5_CHUNK + 1), one, one],
        name="s5_disc")(params)
    tiles = lambda t: t.reshape(t.shape[0], nd, nt, hc, width)
    ak = jnp.stack([akr, aki]).reshape(2, nd, nt, hc, width)[:, :, :, 0]
    ak = ak.transpose(2, 1, 0, 3).reshape(nt, 2 * nd, width)
    return tiles(pbr), tiles(pbi), tiles(pcr), tiles(pci), ak


def _s5_weights_kernel(pbr_ref, pbi_ref, pcr_ref, pci_ref, win_ref, wcat_ref):
    chunk, nd, hc, half = pbr_ref.shape[0], pbr_ref.shape[1], pbr_ref.shape[3], pbr_ref.shape[4]
    blk = S5_SLAB_GROUPS * hc
    p = half // S5_SLAB_GROUPS
    same_group = (lax.broadcasted_iota(jnp.int32, (blk, half), 0) // hc
                  == lax.broadcasted_iota(jnp.int32, (blk, half), 1) // p)

    def expand(re, im):
        grid = lambda t: jnp.where(same_group, jnp.tile(t, (S5_SLAB_GROUPS, 1)), 0.0)
        return jnp.concatenate([grid(re), grid(im)], axis=1)

    def split_dot(a, b):
        a_hi, b_hi = a.astype(BF16), b.astype(BF16)
        a_lo, b_lo = (a - a_hi.astype(F32)).astype(BF16), (b - b_hi.astype(F32)).astype(BF16)
        return jnp.dot(jnp.concatenate([a_hi, a_hi, a_lo], axis=1), jnp.concatenate([b_hi, b_lo, b_hi], axis=0),
                       preferred_element_type=F32)

    n_state = 2 * half
    for o in range(pbr_ref.shape[2]):
        wall = [[expand(pbr_ref[n, d, o], pbi_ref[n, d, o]) for n in range(chunk)] for d in range(nd)]
        mall = [[expand(pcr_ref[n, d, o], pci_ref[n, d, o]).T for n in range(chunk + 1)] for d in range(nd)]

        for d in range(nd):
            for j in range(chunk):
                n_in = chunk - 1 - j if d == 0 else j
                win_ref[d, o, j * blk:(j + 1) * blk, :] = wall[d][n_in].astype(win_ref.dtype)
            n_out = [j + 1 if d == 0 else chunk - j for j in range(chunk)]
            r0 = chunk * blk + d * n_state
            wcat_ref[o, r0:r0 + n_state, :] = jnp.concatenate([mall[d][n] for n in n_out],
                                                              axis=1).astype(wcat_ref.dtype)

        resp = [[split_dot(wall[d][n], mall[d][0]) for n in range(chunk)] for d in range(nd)]
        both = resp[0][0] + resp[1][0]
        for j in range(chunk):
            row = [resp[0][jp - j] if jp > j else resp[1][j - jp] if jp < j else both for jp in range(chunk)]
            wcat_ref[o, j * blk:(j + 1) * blk, :] = jnp.concatenate(row, axis=1).astype(wcat_ref.dtype)


def _s5_chunk_weights(pbr, pbi, pcr, pci, *, slabs_per_step=4):
    chunk, nd, nt, hc, half = pbr.shape
    width, n_state = chunk * S5_SLAB_GROUPS * hc, 2 * half
    sps = slabs_per_step
    spec = lambda t: pl.BlockSpec((t.shape[0], nd, sps, hc, half), lambda o: (0, 0, o, 0, 0))
    return pl.pallas_call(
        _s5_weights_kernel,
        grid=(nt // sps,),
        in_specs=[spec(pbr), spec(pbi), spec(pcr), spec(pci)],
        out_specs=[pl.BlockSpec((nd, sps, width, n_state), lambda o: (0, o, 0, 0)),
                   pl.BlockSpec((sps, width + nd * n_state, width), lambda o: (o, 0, 0))],
        out_shape=[jax.ShapeDtypeStruct((nd, nt, width, n_state), BF16),
                   jax.ShapeDtypeStruct((nt, width + nd * n_state, width), BF16)],
        compiler_params=pltpu.CompilerParams(dimension_semantics=("arbitrary",), vmem_limit_bytes=VMEM_LIMIT),
        name="s5_weights",
    )(pbr, pbi, pcr, pci)


def _s5_chunk_kernel(u_ref, win_ref, wcat_ref, ak_ref, dsk_ref, y_ref, xf_ref, xb_ref, s0_ref, s1_ref, *, nb, rb):
    n_sl, n_rows, half = u_ref.shape[0], u_ref.shape[1], ak_ref.shape[2]
    n_blk, pair = n_rows // rb, 2 * nb
    s_refs = (s0_ref, s1_ref)
    order = {0: list(range(n_blk)), 1: list(reversed(range(n_blk)))}

    def increments(d, i):
        blk = order[d][i]
        for sl in range(n_sl):
            s_refs[i % 2][sl] = jnp.dot(u_ref[sl, blk * rb:(blk + 1) * rb, :].astype(BF16), win_ref[d, sl],
                                        preferred_element_type=F32)

    def scan_block(d, i, xs):
        blk, s_ref, x_ref = order[d][i], s_refs[i % 2], (xf_ref, xb_ref)[d]
        ak = [(jnp.broadcast_to(ak_ref[sl, 2 * d:2 * d + 1, :], (nb, half)),
               jnp.broadcast_to(ak_ref[sl, 2 * d + 1:2 * d + 2, :], (nb, half))) for sl in range(n_sl)]

        def advance(sl, x, rows):
            ar, ai = ak[sl]
            return (ar * x[0] - ai * x[1] + s_ref[sl, rows, :half], ar * x[1] + ai * x[0] + s_ref[sl, rows, half:])

        def body(it, xs):
            r0 = pl.multiple_of((it if d == 0 else rb // pair - 1 - it) * pair, pair)
            early, late = pl.ds(r0, nb), pl.ds(r0 + nb, nb)
            dst = pl.ds(pl.multiple_of(blk * rb + r0, pair), pair)
            out = []
            for sl, x in enumerate(xs):
                x1 = advance(sl, x, early if d == 0 else late)
                x2 = advance(sl, x1, late if d == 0 else early)
                enter_early, enter_late = (x, x1) if d == 0 else (x1, x)
                x_ref[sl, dst, :half] = jnp.concatenate([enter_early[0], enter_late[0]], axis=0).astype(BF16)
                x_ref[sl, dst, half:] = jnp.concatenate([enter_early[1], enter_late[1]], axis=0).astype(BF16)
                out.append(x2)
            return tuple(out)

        return lax.fori_loop(0, rb // pair, body, xs, unroll=True)

    zero = jnp.zeros((nb, half), F32)
    for d in range(2):
        xs = ((zero, zero),) * n_sl
        increments(d, 0)
        for i in range(n_blk):
            if i + 1 < n_blk:
                increments(d, i + 1)
            xs = scan_block(d, i, xs)

    for blk in range(n_blk):
        rows = slice(blk * rb, (blk + 1) * rb)
        for sl in range(n_sl):
            u = u_ref[sl, rows, :]
            lhs = jnp.concatenate([u.astype(BF16), xf_ref[sl, rows, :], xb_ref[sl, rows, :]], axis=1)
            y_ref[sl, rows, :] = jnp.dot(lhs, wcat_ref[sl], preferred_element_type=F32) + dsk_ref[sl] * u


def _s5_chunk_scan(u_tiles, win, wcat, ak, dsk, *, nb, rb=512, slabs_per_step=4):
    nt, n_rows, width = u_tiles.shape
    n_state = win.shape[3]
    sps = slabs_per_step
    kern = functools.partial(_s5_chunk_kernel, nb=nb, rb=rb)
    return pl.pallas_call(
        kern,
        grid=(nt // sps,),
        in_specs=[
            pl.BlockSpec((sps, n_rows, width), lambda o: (o, 0, 0)),
            pl.BlockSpec((win.shape[0], sps, width, n_state), lambda o: (0, o, 0, 0)),
            pl.BlockSpec((sps,) + wcat.shape[1:], lambda o: (o, 0, 0)),
            pl.BlockSpec((sps,) + ak.shape[1:], lambda o: (o, 0, 0)),
            pl.BlockSpec((sps,) + dsk.shape[1:], lambda o: (o, 0, 0)),
        ],
        out_specs=pl.BlockSpec((sps, n_rows, width), lambda o: (o, 0, 0)),
        out_shape=jax.ShapeDtypeStruct(u_tiles.shape, F32),
        scratch_shapes=[
            pltpu.VMEM((sps, n_rows, n_state), BF16), pltpu.VMEM((sps, n_rows, n_state), BF16),
            pltpu.VMEM((sps, rb, n_state), F32), pltpu.VMEM((sps, rb, n_state), F32),
        ],
        compiler_params=pltpu.CompilerParams(
            dimension_semantics=("arbitrary",), vmem_limit_bytes=VMEM_LIMIT),
        name="s5_scan",
    )(u_tiles, win, wcat, ak, dsk)


def _post_kernel(x_ref, att_ref, ys_first_ref, ys_next_ref, wg_ref, bg_ref, wo_ref, g2_ref,
                 w1_ref, w2_ref, gf_ref, o_ref, y_ref, *, ff_chunk, final_norm):
    nb, tt, d = x_ref.shape
    aw = att_ref.shape[2]
    unslab = lambda ref: _from_slabs(ref, nb).reshape(nb * tt, -1)

    @pl.when(pl.program_id(0) == 0)
    def _():
        y_ref[...] = unslab(ys_first_ref)

    y = y_ref[...]
    z = 0.5 * y * (1.0 + lax.erf(y * (1.0 / math.sqrt(2.0))))
    gate = jax.nn.sigmoid(jnp.dot(z.astype(BF16), wg_ref[...].astype(BF16), preferred_element_type=F32)
                          + bg_ref[...])
    ssm = (z * gate).astype(BF16)
    h = (x_ref[...].reshape(nb * tt, d)
         + jnp.dot(att_ref[...].reshape(nb * tt, aw), wo_ref[:aw, :].astype(BF16), preferred_element_type=F32)
         + jnp.dot(ssm, wo_ref[aw:, :].astype(BF16), preferred_element_type=F32))
    hn2 = _rms(h, g2_ref[...]).astype(BF16)
    y_ref[...] = unslab(ys_next_ref)
    mlp = jnp.zeros_like(h)
    for f0 in range(0, w1_ref.shape[1], ff_chunk):
        a = jnp.maximum(jnp.dot(hn2, w1_ref[:, f0:f0 + ff_chunk].astype(BF16), preferred_element_type=F32), 0.0)
        mlp = mlp + jnp.dot((a * a).astype(BF16), w2_ref[f0:f0 + ff_chunk, :].astype(BF16),
                            preferred_element_type=F32)
    h = h + mlp
    o_ref[...] = (_rms(h, gf_ref[...]) if final_norm else h).reshape(nb, tt, d)


def _post(x3, att, y_slabs, w_glu, b_glu, w_out, g2, w1, w2, gf, *, final_norm, tt=64):
    nb, s, d = x3.shape
    aw = att.shape[2]
    n_slabs, _, slab_w = y_slabs.shape
    rows = lambda width: pl.BlockSpec((nb, tt, width), lambda i: (0, i, 0))
    const = lambda arr: pl.BlockSpec(arr.shape, lambda i: (0, 0), pipeline_mode=pl.Buffered(1))
    kern = functools.partial(_post_kernel, ff_chunk=1024, final_norm=final_norm)
    n_steps = s // tt
    slab_block = (n_slabs, tt // S5_CHUNK * nb, slab_w)
    return pl.pallas_call(
        kern,
        grid=(n_steps,),
        in_specs=[rows(d), rows(aw),
                  pl.BlockSpec(slab_block, lambda i: (0, 0, 0), pipeline_mode=pl.Buffered(1)),
                  pl.BlockSpec(slab_block, lambda i: (0, jnp.minimum(i + 1, n_steps - 1), 0)),
                  const(w_glu), const(b_glu), const(w_out), const(g2), const(w1), const(w2), const(gf)],
        out_specs=rows(d),
        out_shape=jax.ShapeDtypeStruct((nb, s, d), F32),
        scratch_shapes=[pltpu.VMEM((nb * tt, n_slabs * S5_SLAB_LANES), F32)],
        compiler_params=pltpu.CompilerParams(
            dimension_semantics=("arbitrary",), vmem_limit_bytes=VMEM_LIMIT),
        name="post_mlp",
    )(x3, att, y_slabs, y_slabs, w_glu, b_glu, w_out, g2, w1, w2, gf)


def kernel(x, norm1_g, w_in, lam_q1, lam_k1, lam_q2, lam_k2, subln_g, ssm_lam_re, ssm_lam_im, ssm_log_dt,
           ssm_b_re, ssm_b_im, ssm_c_re, ssm_c_im, ssm_d, w_glu, b_glu, w_out, norm2_g, w_mlp1, w_mlp2,
           final_g):
    bsz, s, d = x.shape
    depth = w_in.shape[0]
    ssm_w = ssm_d.shape[-1]
    attn_w = w_out.shape[1] - ssm_w
    qk_dim = attn_w // ATTN_HEADS // 2
    scale = float(qk_dim) ** -0.5 * LOG2_E

    h3 = x.astype(F32)
    for l in range(depth):
        lambda_init = 0.8 - 0.6 * math.exp(-0.3 * l)
        qk, v_t, u_slabs = _inproj(h3, norm1_g[l][None].astype(F32), w_in[l],
                                   q_width=attn_w, n_heads=ATTN_HEADS, ssm_width=ssm_w, scale=scale)

        row = lambda t: t[l][None].astype(F32)
        att = _attention(qk, v_t, row(lam_q1), row(lam_k1), row(lam_q2), row(lam_k2), row(subln_g),
                         lambda_init=lambda_init)

        pbr, pbi, pcr, pci, ak_slabs = _s5_discretise(ssm_lam_re[l], ssm_lam_im[l], ssm_log_dt[l],
                                                      ssm_b_re[l], ssm_b_im[l], ssm_c_re[l], ssm_c_im[l])
        win, wcat = _s5_chunk_weights(pbr, pbi, pcr, pci)
        dsk_slabs = jnp.tile(ssm_d[l].astype(F32).reshape(-1, 1, S5_SLAB_LANES), (1, 1, S5_CHUNK))
        y_slabs = _s5_chunk_scan(u_slabs, win, wcat, ak_slabs, dsk_slabs, nb=bsz)

        h3 = _post(h3, att, y_slabs, w_glu[l], row(b_glu), w_out[l], row(norm2_g),
                   w_mlp1[l], w_mlp2[l], final_g[None].astype(F32),
                   final_norm=(l == depth - 1))
    return h3.astype(x.dtype)
```

```python
import functools
import math

import jax
import jax.numpy as jnp
from jax import lax
from jax.experimental import pallas as pl
from jax.experimental.pallas import tpu as pltpu

F32 = jnp.float32
BF16 = jnp.bfloat16

EPS = 1e-6
ATTN_HEADS = 4
SSM_GROUP = 16
SSM_STATE = 64
S5_SLAB_GROUPS = 2
S5_SLAB_LANES = S5_SLAB_GROUPS * SSM_GROUP
VMEM_LIMIT = 56 * 1024 * 1024


def _rms(x, g):
    return x * lax.rsqrt(jnp.mean(x * x, axis=-1, keepdims=True) + EPS) * g


def _to_slabs(u3):
    nb, tt, w = u3.shape
    n_ch = tt // S5_CHUNK
    t = jnp.transpose(u3.reshape(nb, n_ch, S5_CHUNK, w), (1, 2, 0, 3))
    slabs = []
    for q in range(w // S5_SLAB_LANES):
        lanes = slice(q * S5_SLAB_LANES, (q + 1) * S5_SLAB_LANES)
        rows = [jnp.concatenate([t[c, j][:, lanes] for j in range(S5_CHUNK)], axis=1) for c in range(n_ch)]
        slabs.append(jnp.concatenate(rows, axis=0))
    return slabs


def _from_slabs(slab_ref, nb):
    n_slabs, rows, _ = slab_ref.shape
    n_ch = rows // nb
    per_chunk = []
    for c in range(n_ch):
        steps = [jnp.concatenate([slab_ref[q, c * nb:(c + 1) * nb, j * S5_SLAB_LANES:(j + 1) * S5_SLAB_LANES]
                                  for q in range(n_slabs)], axis=1) for j in range(S5_CHUNK)]
        per_chunk.append(jnp.stack(steps, axis=0))
    t = jnp.stack(per_chunk, axis=0)
    return jnp.transpose(t, (2, 0, 1, 3)).reshape(nb, n_ch * S5_CHUNK, n_slabs * S5_SLAB_LANES)


def _alibi_log2_slope_parts(slope):
    parts, rest = [], slope * LOG2_E
    for _ in range(3):
        part = rest.astype(BF16).astype(F32)
        parts.append(part)
        rest = rest - part
    return parts


def _bias_columns(lane, first_lane, vals):
    out = 0.0
    for e, val in enumerate(vals):
        out = jnp.where(lane == first_lane + e, val, out)
    return out


def _split_positions(pos):
    low = pos & ((1 << POS_LOW_BITS) - 1)
    return (pos - low).astype(F32), low.astype(F32)


def _inproj_kernel(x_ref, g_ref, w_ref, q_ref, kx_ref, vt_ref, us_ref, nrm_ref, *, scale):
    nb, tt, d = x_ref.shape
    q_width, n_heads, hv = q_ref.shape[2], vt_ref.shape[1], vt_ref.shape[2]
    dh = hv // 2
    step = pl.program_id(0)
    hn = _rms(x_ref[...].reshape(nb * tt, d), g_ref[...]).astype(BF16)
    w = lambda c0, c1: w_ref[:, c0:c1].astype(BF16)
    u = jnp.dot(hn, w(3 * q_width, w_ref.shape[1]), preferred_element_type=F32)
    for q, slab in enumerate(_to_slabs(u.reshape(nb, tt, u.shape[1]))):
        us_ref[q] = slab
    v = jnp.dot(hn, w(2 * q_width, 3 * q_width), preferred_element_type=F32)
    for b in range(nb):
        for hh in range(n_heads):
            vt_ref[b, hh] = v[b * tt:(b + 1) * tt, hh * hv:(hh + 1) * hv].T.astype(BF16)

    lane = lax.broadcasted_iota(jnp.int32, (1, hv), 1)

    def folded_norms(p):
        sq = jnp.max((p * p).reshape(nb, NORM_FOLD, tt // NORM_FOLD, q_width), axis=1)
        cols = [jnp.max(jnp.sum(sq[:, :, hh * hv:(hh + 1) * hv], axis=2, keepdims=True), axis=1)
                for hh in range(n_heads)]
        return cols

    kp = jnp.dot(hn, w(q_width, 2 * q_width), preferred_element_type=F32)
    p_hi, p_lo = _split_positions(step * tt + lax.broadcasted_iota(jnp.int32, (tt, 1), 0))
    for hh in range(n_heads):
        c2_parts = _alibi_log2_slope_parts(jnp.full((1, 1), 2.0 ** (-8.0 * (hh + 1) / n_heads), F32))
        kex = _bias_columns(lane, dh, c2_parts + c2_parts + [p_hi] * 3 + [p_lo] * 3)
        k_h = kp[:, hh * hv:(hh + 1) * hv].reshape(nb, tt, hv)
        kx_ref[:, hh, 0] = jnp.where(lane < dh, k_h, kex).astype(BF16)
        kx_ref[:, hh, 1] = jnp.where(lane >= dh, k_h, pltpu.roll(kex, dh, 1)).astype(BF16)

    k_norms = folded_norms(kp)
    qp = jnp.dot(hn, w(0, q_width), preferred_element_type=F32) * scale
    q_ref[...] = qp.astype(BF16).reshape(nb, tt, q_width)

    norms = 0.0
    for i, col in enumerate(folded_norms(qp) + k_norms):
        norms = jnp.where(lane == i, col, norms)

    @pl.when(step == 0)
    def _():
        nrm_ref[...] = jnp.zeros_like(nrm_ref)

    nrm_ref[...] = jnp.maximum(nrm_ref[...], norms)


def _inproj(x3, g1, w_in, *, q_width, n_heads, ssm_width, scale, tt=128):
    nb, s, d = x3.shape
    n_out = w_in.shape[1]
    hv = q_width // n_heads
    assert n_out == 3 * q_width + ssm_width
    n_slabs = ssm_width // S5_SLAB_LANES
    kern = functools.partial(_inproj_kernel, scale=scale)
    return pl.pallas_call(
        kern,
        grid=(s // tt,),
        in_specs=[
            pl.BlockSpec((nb, tt, d), lambda i: (0, i, 0)),
            pl.BlockSpec((1, d), lambda i: (0, 0)),
            pl.BlockSpec((d, n_out), lambda i: (0, 0)),
        ],
        out_specs=[
            pl.BlockSpec((nb, tt, q_width), lambda i: (0, i, 0)),
            pl.BlockSpec((nb, n_heads, 2, tt, hv), lambda i: (0, 0, 0, i, 0)),
            pl.BlockSpec((nb, n_heads, hv, tt), lambda i: (0, 0, 0, i)),
            pl.BlockSpec((n_slabs, tt // S5_CHUNK * nb, S5_CHUNK * S5_SLAB_LANES), lambda i: (0, i, 0)),
            pl.BlockSpec((nb, hv), lambda i: (0, 0)),
        ],
        out_shape=[
            jax.ShapeDtypeStruct((nb, s, q_width), BF16),
            jax.ShapeDtypeStruct((nb, n_heads, 2, s, hv), BF16),
            jax.ShapeDtypeStruct((nb, n_heads, hv, s), BF16),
            jax.ShapeDtypeStruct((n_slabs, s // S5_CHUNK * nb, S5_CHUNK * S5_SLAB_LANES), F32),
            jax.ShapeDtypeStruct((nb, hv), F32),
        ],
        compiler_params=pltpu.CompilerParams(
            dimension_semantics=("arbitrary",), vmem_limit_bytes=VMEM_LIMIT),
        name="inproj",
    )(x3, g1, w_in)


POS_LOW_BITS = 3
LOG2_E = 1.4426950408889634
SAFE_LOG2_SCORE = 96.0
SAFE_VALUE = 2.0 ** 16
NORM_FOLD = 4
NORM_ROUNDING_MARGIN = 1.02


def _attn_kernel(lq1_ref, lk1_ref, lq2_ref, lk2_ref, sg_ref, nrm_ref, q_ref, kx_ref, vin_ref, o_ref,
                 qex_ref, diag_ref, s0_ref, s1_ref, s2_ref, s3_ref, e0_ref, e1_ref,
                 *, qb, lambda_init):
    h = pl.program_id(0)
    s_refs, e_refs = (s0_ref, s1_ref, s2_ref, s3_ref), (e0_ref, e1_ref)
    s_len, hv = q_ref.shape[1], q_ref.shape[2]
    dh = hv // 2

    lam = (jnp.exp(jnp.sum(lq1_ref[...] * lk1_ref[...], axis=-1, keepdims=True))
           - jnp.exp(jnp.sum(lq2_ref[...] * lk2_ref[...], axis=-1, keepdims=True))
           + lambda_init)

    lane = lax.broadcasted_iota(jnp.int32, (1, hv), 1)
    data_lanes = (lane < dh, lane >= dh)

    @pl.when(pl.program_id(1) == 0)
    def _():
        slope = jnp.float32(2.0 ** -8.0)
        for hh in range(ATTN_HEADS - 1):
            slope = jnp.where(h == hh, jnp.float32(2.0 ** (-8.0 * (hh + 1) / ATTN_HEADS)), slope)
        slope = jnp.full((1, 1), slope, F32)
        c2_parts = _alibi_log2_slope_parts(slope)
        p_hi, p_lo = _split_positions(lax.broadcasted_iota(jnp.int32, (s_len, 1), 0))
        qex = _bias_columns(lane, dh, [-p_hi] * 3 + [-p_lo] * 3 + c2_parts + c2_parts)
        qex_ref[0], qex_ref[1] = qex, pltpu.roll(qex, dh, 1)
        off = lax.broadcasted_iota(jnp.int32, (qb, qb), 0) - lax.broadcasted_iota(jnp.int32, (qb, qb), 1)
        diag_ref[...] = jnp.abs(off).astype(F32) * (-(slope * LOG2_E))

    mine = lax.broadcasted_iota(jnp.int32, nrm_ref.shape, 0) == pl.program_id(1)
    lane_n = lax.broadcasted_iota(jnp.int32, nrm_ref.shape, 1)
    pick = lambda col: jnp.max(jnp.max(jnp.where(mine & (lane_n == col), nrm_ref[...], 0.0),
                                       axis=1, keepdims=True), axis=0, keepdims=True)
    v_max = jnp.max(jnp.max(jnp.abs(vin_ref[0, 0].astype(F32)), axis=1, keepdims=True), axis=0, keepdims=True)
    small_scores = jnp.logical_and(
        (pick(h) * pick(h + ATTN_HEADS))[0, 0] * NORM_ROUNDING_MARGIN <= SAFE_LOG2_SCORE ** 2,
        v_max[0, 0] <= SAFE_VALUE)

    n_blocks = s_len // qb

    def score_pieces(n, q_side):
        qi, c = divmod(n, 2)
        lo, hi = qi * qb, (qi + 1) * qb
        if qi not in q_side:
            q_side[qi] = q_ref[0, lo:hi, :].astype(F32)
        qf = q_side[qi]
        aug_t = lambda ex: jnp.where(data_lanes[c], qf, ex).T.astype(BF16)
        dot = lambda r0, r1, qt: jnp.dot(kx_ref[0, 0, c, r0:r1, :], qt, preferred_element_type=F32)
        q_extra = qex_ref[c, lo:hi, :]
        q_left, q_right = aug_t(q_extra), aug_t(-q_extra)
        for r0 in range(0, s_len, qb):
            if r0 == lo:
                yield lo, hi, dot(lo, hi, aug_t(jnp.zeros_like(q_extra))) + diag_ref[...]
            else:
                yield r0, r0 + qb, dot(r0, r0 + qb, q_left if r0 < lo else q_right)

    def weighted_values(n, denom):
        return jnp.dot(vin_ref[0, 0], e_refs[n % 2][...], preferred_element_type=F32) / denom

    def finish_block(qi, out_t):
        o = (out_t[0] - lam * out_t[1]).T
        o = _rms(o, sg_ref[...]) * (1.0 - lambda_init)
        o_ref[0, qi * qb:(qi + 1) * qb, :] = o.astype(o_ref.dtype)

    @pl.when(small_scores)
    def _():
        q_side = {}
        for qi in range(n_blocks):
            out_t = []
            for n in (2 * qi, 2 * qi + 1):
                denom = 0.0
                for r0, r1, sc in score_pieces(n, q_side):
                    e = jnp.exp2(sc)
                    denom = denom + jnp.sum(e, axis=0, keepdims=True)
                    e_refs[n % 2][r0:r1, :] = e.astype(BF16)
                out_t.append(weighted_values(n, denom))
            finish_block(qi, out_t)

    @pl.when(jnp.logical_not(small_scores))
    def _():
        q_side = {}

        def scores(n):
            for r0, r1, sc in score_pieces(n, q_side):
                s_refs[n % 4][r0:r1, :] = sc

        def exponentials(n):
            sc = s_refs[n % 4][...]
            e = jnp.exp2(sc - jnp.max(sc, axis=0, keepdims=True))
            e_refs[n % 2][...] = e.astype(BF16)
            return jnp.sum(e, axis=0, keepdims=True)

        scores(0), scores(1)
        for qi in range(n_blocks):
            if qi + 1 < n_blocks:
                scores(2 * qi + 2), scores(2 * qi + 3)
            out_t = []
            for n in (2 * qi, 2 * qi + 1):
                out_t.append(weighted_values(n, exponentials(n)))
            finish_block(qi, out_t)


def _attention(q3, kx, v_t, norms, lq1, lk1, lq2, lk2, subln_g, *, lambda_init, qb=512):
    b, s, w = q3.shape
    hv = w // ATTN_HEADS
    assert s <= 256 << POS_LOW_BITS and s % qb == 0
    kern = functools.partial(_attn_kernel, qb=qb, lambda_init=lambda_init)
    small = lambda shape: pl.BlockSpec(shape, lambda hi, bi: (0, 0))
    head = pl.BlockSpec((1, s, hv), lambda hi, bi: (bi, 0, hi))
    return pl.pallas_call(
        kern,
        grid=(ATTN_HEADS, b),
        in_specs=[
            small(lq1.shape), small(lk1.shape), small(lq2.shape), small(lk2.shape), small(subln_g.shape),
            small(norms.shape), head,
            pl.BlockSpec((1, 1, 2, s, hv), lambda hi, bi: (bi, hi, 0, 0, 0)),
            pl.BlockSpec((1, 1, hv, s), lambda hi, bi: (bi, hi, 0, 0)),
        ],
        out_specs=head,
        out_shape=jax.ShapeDtypeStruct((b, s, ATTN_HEADS * hv), BF16),
        scratch_shapes=[
            pltpu.VMEM((2, s, hv), F32), pltpu.VMEM((qb, qb), F32),
            pltpu.VMEM((s, qb), F32), pltpu.VMEM((s, qb), F32), pltpu.VMEM((s, qb), F32), pltpu.VMEM((s, qb), F32),
            pltpu.VMEM((s, qb), BF16), pltpu.VMEM((s, qb), BF16),
        ],
        compiler_params=pltpu.CompilerParams(
            dimension_semantics=("arbitrary", "arbitrary"), vmem_limit_bytes=VMEM_LIMIT),
        name="diff_attn",
    )(lq1, lk1, lq2, lk2, subln_g, norms, q3, kx, v_t)


S5_CHUNK = 8


def _s5_disc_kernel(lr_ref, li_ref, ldt_ref, bre_ref, bim_ref, cre_ref, cim_ref,
                    pbr_ref, pbi_ref, pcr_ref, pci_ref, akr_ref, aki_ref):
    cmul = lambda ar, ai, xr, xi: (ar * xr - ai * xi, ar * xi + ai * xr)

    lr, li = lr_ref[...], li_ref[...]
    dt = jnp.exp(ldt_ref[...])
    mag = jnp.exp(lr * dt)
    ang = li * dt
    ar, ai = mag * jnp.cos(ang), mag * jnp.sin(ang)
    nr = ar - 1.0
    den = lr * lr + li * li
    cr = (nr * lr + ai * li) / den
    ci = (ai * lr - nr * li) / den
    b_re, b_im = bre_ref[...], bim_ref[...]
    pr, pi = cr * b_re - ci * b_im, cr * b_im + ci * b_re
    kr, ki = ar, ai
    for n in range(pbr_ref.shape[0]):
        pbr_ref[n] = pr
        pbi_ref[n] = pi
        pr, pi = cmul(ar, ai, pr, pi)
        if n > 0:
            kr, ki = cmul(ar, ai, kr, ki)
    akr_ref[...] = kr
    aki_ref[...] = ki

    qr, qi = cre_ref[...], cim_ref[...]
    for n in range(pcr_ref.shape[0]):
        pcr_ref[n] = qr
        pci_ref[n] = -qi
        qr, qi = cmul(ar, ai, qr, qi)


def _s5_discretise(lam_re, lam_im, log_dt, b_re, b_im, c_re, c_im):
    nd, g, p, hc = b_re.shape
    nt = g // S5_SLAB_GROUPS
    rows, width = nd * nt * hc, S5_SLAB_GROUPS * p
    arrange = lambda t: (jnp.broadcast_to(t.astype(F32), (nd, g, hc, p)).reshape(nd, nt, S5_SLAB_GROUPS, hc, p)
                         .transpose(0, 1, 3, 2, 4).reshape(rows, width))
    args = (arrange(lam_re[:, :, None, :]), arrange(lam_im[:, :, None, :]), arrange(log_dt[..., None, None]),
            arrange(jnp.swapaxes(b_re, 2, 3)), arrange(jnp.swapaxes(b_im, 2, 3)), arrange(c_re), arrange(c_im))
    stack = lambda n: jax.ShapeDtypeStruct((n, rows, width), F32)
    one = jax.ShapeDtypeStruct((rows, width), F32)
    pbr, pbi, pcr, pci, akr, aki = pl.pallas_call(
        _s5_disc_kernel,
        out_shape=[stack(S5_CHUNK), stack(S5_CHUNK), stack(S5_CHUNK + 1), stack(S5_CHUNK + 1), one, one],
        name="s5_disc")(*args)
    tiles = lambda t: t.reshape(t.shape[0], nd, nt, hc, width)
    ak = jnp.stack([akr, aki]).reshape(2, nd, nt, hc, width)[:, :, :, 0]
    ak = ak.transpose(2, 1, 0, 3).reshape(nt, 2 * nd, width)
    return tiles(pbr), tiles(pbi), tiles(pcr), tiles(pci), ak


def _s5_weights_kernel(pbr_ref, pbi_ref, pcr_ref, pci_ref, win_ref, wcat_ref):
    chunk, nd, hc, half = pbr_ref.shape[0], pbr_ref.shape[1], pbr_ref.shape[3], pbr_ref.shape[4]
    blk = S5_SLAB_GROUPS * hc
    p = half // S5_SLAB_GROUPS
    same_group = (lax.broadcasted_iota(jnp.int32, (blk, half), 0) // hc
                  == lax.broadcasted_iota(jnp.int32, (blk, half), 1) // p)

    def expand(re, im):
        grid = lambda t: jnp.where(same_group, jnp.tile(t, (S5_SLAB_GROUPS, 1)), 0.0)
        return jnp.concatenate([grid(re), grid(im)], axis=1)

    def split_dot(a, b):
        a_hi, b_hi = a.astype(BF16), b.astype(BF16)
        a_lo, b_lo = (a - a_hi.astype(F32)).astype(BF16), (b - b_hi.astype(F32)).astype(BF16)
        return jnp.dot(jnp.concatenate([a_hi, a_hi, a_lo], axis=1), jnp.concatenate([b_hi, b_lo, b_hi], axis=0),
                       preferred_element_type=F32)

    n_state = 2 * half
    for o in range(pbr_ref.shape[2]):
        wall = [[expand(pbr_ref[n, d, o], pbi_ref[n, d, o]) for n in range(chunk)] for d in range(nd)]
        mall = [[expand(pcr_ref[n, d, o], pci_ref[n, d, o]).T for n in range(chunk + 1)] for d in range(nd)]

        for d in range(nd):
            for j in range(chunk):
                n_in = chunk - 1 - j if d == 0 else j
                win_ref[d, o, j * blk:(j + 1) * blk, :] = wall[d][n_in].astype(win_ref.dtype)
            n_out = [j + 1 if d == 0 else chunk - j for j in range(chunk)]
            r0 = chunk * blk + d * n_state
            wcat_ref[o, r0:r0 + n_state, :] = jnp.concatenate([mall[d][n] for n in n_out],
                                                              axis=1).astype(wcat_ref.dtype)

        resp = [[split_dot(wall[d][n], mall[d][0]) for n in range(chunk)] for d in range(nd)]
        both = resp[0][0] + resp[1][0]
        for j in range(chunk):
            row = [resp[0][jp - j] if jp > j else resp[1][j - jp] if jp < j else both for jp in range(chunk)]
            wcat_ref[o, j * blk:(j + 1) * blk, :] = jnp.concatenate(row, axis=1).astype(wcat_ref.dtype)


def _s5_chunk_weights(pbr, pbi, pcr, pci, *, slabs_per_step=4):
    chunk, nd, nt, hc, half = pbr.shape
    width, n_state = chunk * S5_SLAB_GROUPS * hc, 2 * half
    sps = slabs_per_step
    spec = lambda t: pl.BlockSpec((t.shape[0], nd, sps, hc, half), lambda o: (0, 0, o, 0, 0))
    return pl.pallas_call(
        _s5_weights_kernel,
        grid=(nt // sps,),
        in_specs=[spec(pbr), spec(pbi), spec(pcr), spec(pci)],
        out_specs=[pl.BlockSpec((nd, sps, width, n_state), lambda o: (0, o, 0, 0)),
                   pl.BlockSpec((sps, width + nd * n_state, width), lambda o: (o, 0, 0))],
        out_shape=[jax.ShapeDtypeStruct((nd, nt, width, n_state), BF16),
                   jax.ShapeDtypeStruct((nt, width + nd * n_state, width), BF16)],
        compiler_params=pltpu.CompilerParams(dimension_semantics=("arbitrary",), vmem_limit_bytes=VMEM_LIMIT),
        name="s5_weights",
    )(pbr, pbi, pcr, pci)


def _s5_chunk_kernel(u_ref, win_ref, wcat_ref, ak_ref, dsk_ref, y_ref, xf_ref, xb_ref, s0_ref, s1_ref, *, nb, rb):
    n_sl, n_rows, half = u_ref.shape[0], u_ref.shape[1], ak_ref.shape[2]
    n_blk, pair = n_rows // rb, 2 * nb
    s_refs = (s0_ref, s1_ref)
    order = {0: list(range(n_blk)), 1: list(reversed(range(n_blk)))}

    def increments(d, i):
        blk = order[d][i]
        for sl in range(n_sl):
            s_refs[i % 2][sl] = jnp.dot(u_ref[sl, blk * rb:(blk + 1) * rb, :].astype(BF16), win_ref[d, sl],
                                        preferred_element_type=F32)

    def scan_block(d, i, xs):
        blk, s_ref, x_ref = order[d][i], s_refs[i % 2], (xf_ref, xb_ref)[d]
        ak = [(jnp.broadcast_to(ak_ref[sl, 2 * d:2 * d + 1, :], (nb, half)),
               jnp.broadcast_to(ak_ref[sl, 2 * d + 1:2 * d + 2, :], (nb, half))) for sl in range(n_sl)]

        def advance(sl, x, rows):
            ar, ai = ak[sl]
            return (ar * x[0] - ai * x[1] + s_ref[sl, rows, :half], ar * x[1] + ai * x[0] + s_ref[sl, rows, half:])

        def body(it, xs):
            r0 = pl.multiple_of((it if d == 0 else rb // pair - 1 - it) * pair, pair)
            early, late = pl.ds(r0, nb), pl.ds(r0 + nb, nb)
            dst = pl.ds(pl.multiple_of(blk * rb + r0, pair), pair)
            out = []
            for sl, x in enumerate(xs):
                x1 = advance(sl, x, early if d == 0 else late)
                x2 = advance(sl, x1, late if d == 0 else early)
                enter_early, enter_late = (x, x1) if d == 0 else (x1, x)
                x_ref[sl, dst, :half] = jnp.concatenate([enter_early[0], enter_late[0]], axis=0).astype(BF16)
                x_ref[sl, dst, half:] = jnp.concatenate([enter_early[1], enter_late[1]], axis=0).astype(BF16)
                out.append(x2)
            return tuple(out)

        return lax.fori_loop(0, rb // pair, body, xs, unroll=True)

    zero = jnp.zeros((nb, half), F32)
    for d in range(2):
        xs = ((zero, zero),) * n_sl
        increments(d, 0)
        for i in range(n_blk):
            if i + 1 < n_blk:
                increments(d, i + 1)
            xs = scan_block(d, i, xs)

    for blk in range(n_blk):
        rows = slice(blk * rb, (blk + 1) * rb)
        for sl in range(n_sl):
            u = u_ref[sl, rows, :]
            lhs = jnp.concatenate([u.astype(BF16), xf_ref[sl, rows, :], xb_ref[sl, rows, :]], axis=1)
            y_ref[sl, rows, :] = jnp.dot(lhs, wcat_ref[sl], preferred_element_type=F32) + dsk_ref[sl] * u


def _s5_chunk_scan(u_tiles, win, wcat, ak, dsk, *, nb, rb=256, slabs_per_step=4):
    nt, n_rows, width = u_tiles.shape
    n_state = win.shape[3]
    sps = slabs_per_step
    kern = functools.partial(_s5_chunk_kernel, nb=nb, rb=rb)
    return pl.pallas_call(
        kern,
        grid=(nt // sps,),
        in_specs=[
            pl.BlockSpec((sps, n_rows, width), lambda o: (o, 0, 0)),
            pl.BlockSpec((win.shape[0], sps, width, n_state), lambda o: (0, o, 0, 0)),
            pl.BlockSpec((sps,) + wcat.shape[1:], lambda o: (o, 0, 0)),
            pl.BlockSpec((sps,) + ak.shape[1:], lambda o: (o, 0, 0)),
            pl.BlockSpec((sps,) + dsk.shape[1:], lambda o: (o, 0, 0)),
        ],
        out_specs=pl.BlockSpec((sps, n_rows, width), lambda o: (o, 0, 0)),
        out_shape=jax.ShapeDtypeStruct(u_tiles.shape, F32),
        scratch_shapes=[
            pltpu.VMEM((sps, n_rows, n_state), BF16), pltpu.VMEM((sps, n_rows, n_state), BF16),
            pltpu.VMEM((sps, rb, n_state), F32), pltpu.VMEM((sps, rb, n_state), F32),
        ],
        compiler_params=pltpu.CompilerParams(
            dimension_semantics=("arbitrary",), vmem_limit_bytes=VMEM_LIMIT),
        name="s5_scan",
    )(u_tiles, win, wcat, ak, dsk)


def _post_kernel(x_ref, att_ref, ys_first_ref, ys_next_ref, wg_ref, bg_ref, wo_ref, g2_ref,
                 w1_ref, w2_ref, gf_ref, o_ref, y_ref, *, ff_chunk, final_norm):
    nb, tt, d = x_ref.shape
    aw = att_ref.shape[2]
    unslab = lambda ref: _from_slabs(ref, nb).reshape(nb * tt, -1)

    @pl.when(pl.program_id(0) == 0)
    def _():
        y_ref[...] = unslab(ys_first_ref)

    y = y_ref[...]
    z = 0.5 * y * (1.0 + lax.erf(y * (1.0 / math.sqrt(2.0))))
    gate = jax.nn.sigmoid(jnp.dot(z.astype(BF16), wg_ref[...], preferred_element_type=F32) + bg_ref[...])
    ssm = (z * gate).astype(BF16)
    h = (x_ref[...].reshape(nb * tt, d)
         + jnp.dot(att_ref[...].reshape(nb * tt, aw), wo_ref[:aw, :], preferred_element_type=F32)
         + jnp.dot(ssm, wo_ref[aw:, :], preferred_element_type=F32))
    hn2 = _rms(h, g2_ref[...]).astype(BF16)
    y_ref[...] = unslab(ys_next_ref)
    mlp = jnp.zeros_like(h)
    for f0 in range(0, w1_ref.shape[1], ff_chunk):
        a = jnp.maximum(jnp.dot(hn2, w1_ref[:, f0:f0 + ff_chunk].astype(BF16), preferred_element_type=F32), 0.0)
        mlp = mlp + jnp.dot((a * a).astype(BF16), w2_ref[f0:f0 + ff_chunk, :].astype(BF16),
                            preferred_element_type=F32)
    h = h + mlp
    o_ref[...] = (_rms(h, gf_ref[...]) if final_norm else h).reshape(nb, tt, d)


def _post(x3, att, y_slabs, w_glu, b_glu, w_out, g2, w1, w2, gf, *, final_norm, tt=64):
    nb, s, d = x3.shape
    aw = att.shape[2]
    n_slabs, _, slab_w = y_slabs.shape
    rows = lambda width: pl.BlockSpec((nb, tt, width), lambda i: (0, i, 0))
    const = lambda arr: pl.BlockSpec(arr.shape, lambda i: (0, 0), pipeline_mode=pl.Buffered(1))
    kern = functools.partial(_post_kernel, ff_chunk=1024, final_norm=final_norm)
    n_steps = s // tt
    slab_block = (n_slabs, tt // S5_CHUNK * nb, slab_w)
    return pl.pallas_call(
        kern,
        grid=(n_steps,),
        in_specs=[rows(d), rows(aw),
                  pl.BlockSpec(slab_block, lambda i: (0, 0, 0)),
                  pl.BlockSpec(slab_block, lambda i: (0, jnp.minimum(i + 1, n_steps - 1), 0)),
                  const(w_glu), const(b_glu), const(w_out), const(g2), const(w1), const(w2), const(gf)],
        out_specs=rows(d),
        out_shape=jax.ShapeDtypeStruct((nb, s, d), F32),
        scratch_shapes=[pltpu.VMEM((nb * tt, n_slabs * S5_SLAB_LANES), F32)],
        compiler_params=pltpu.CompilerParams(
            dimension_semantics=("arbitrary",), vmem_limit_bytes=VMEM_LIMIT),
        name="post_mlp",
    )(x3, att, y_slabs, y_slabs, w_glu, b_glu, w_out, g2, w1, w2, gf)


def kernel(x, norm1_g, w_in, lam_q1, lam_k1, lam_q2, lam_k2, subln_g, ssm_lam_re, ssm_lam_im, ssm_log_dt,
           ssm_b_re, ssm_b_im, ssm_c_re, ssm_c_im, ssm_d, w_glu, b_glu, w_out, norm2_g, w_mlp1, w_mlp2,
           final_g):
    bsz, s, d = x.shape
    depth = w_in.shape[0]
    ssm_w = ssm_d.shape[-1]
    attn_w = w_out.shape[1] - ssm_w
    qk_dim = attn_w // ATTN_HEADS // 2
    scale = float(qk_dim) ** -0.5 * LOG2_E

    h3 = x.astype(F32)
    for l in range(depth):
        lambda_init = 0.8 - 0.6 * math.exp(-0.3 * l)
        q, kx, v_t, u_slabs, norms = _inproj(h3, norm1_g[l][None].astype(F32), w_in[l], q_width=attn_w,
                                             n_heads=ATTN_HEADS, ssm_width=ssm_w, scale=scale)

        row = lambda t: t[l][None].astype(F32)
        att = _attention(q, kx, v_t, norms, row(lam_q1), row(lam_k1), row(lam_q2), row(lam_k2), row(subln_g),
                         lambda_init=lambda_init)

        pbr, pbi, pcr, pci, ak_slabs = _s5_discretise(ssm_lam_re[l], ssm_lam_im[l], ssm_log_dt[l],
                                                      ssm_b_re[l], ssm_b_im[l], ssm_c_re[l], ssm_c_im[l])
        win, wcat = _s5_chunk_weights(pbr, pbi, pcr, pci)
        dsk_slabs = jnp.tile(ssm_d[l].astype(F32).reshape(-1, 1, S5_SLAB_LANES), (1, 1, S5_CHUNK))
        y_slabs = _s5_chunk_scan(u_slabs, win, wcat, ak_slabs, dsk_slabs, nb=bsz)

        h3 = _post(h3, att, y_slabs, w_glu[l].astype(BF16), row(b_glu), w_out[l].astype(BF16), row(norm2_g),
                   w_mlp1[l], w_mlp2[l], final_g[None].astype(F32),
                   final_norm=(l == depth - 1))
    return h3.astype(x.dtype)
```

```python
import functools
import math

import jax
import jax.numpy as jnp
from jax import lax
from jax.experimental import pallas as pl
from jax.experimental.pallas import tpu as pltpu

F32 = jnp.float32
BF16 = jnp.bfloat16

EPS = 1e-6
ATTN_HEADS = 4
SSM_GROUP = 16
SSM_STATE = 64
S5_SLAB_GROUPS = 2
S5_SLAB_LANES = S5_SLAB_GROUPS * SSM_GROUP
VMEM_LIMIT = 56 * 1024 * 1024


def _rms(x, g):
    return x * lax.rsqrt(jnp.mean(x * x, axis=-1, keepdims=True) + EPS) * g


def _to_slabs(u3):
    nb, tt, w = u3.shape
    n_ch = tt // S5_CHUNK
    t = jnp.transpose(u3.reshape(nb, n_ch, S5_CHUNK, w), (1, 2, 0, 3))
    slabs = []
    for q in range(w // S5_SLAB_LANES):
        lanes = slice(q * S5_SLAB_LANES, (q + 1) * S5_SLAB_LANES)
        rows = [jnp.concatenate([t[c, j][:, lanes] for j in range(S5_CHUNK)], axis=1) for c in range(n_ch)]
        slabs.append(jnp.concatenate(rows, axis=0))
    return slabs


def _from_slabs(slab_ref, nb):
    n_slabs, rows, _ = slab_ref.shape
    n_ch = rows // nb
    per_chunk = []
    for c in range(n_ch):
        steps = [jnp.concatenate([slab_ref[q, c * nb:(c + 1) * nb, j * S5_SLAB_LANES:(j + 1) * S5_SLAB_LANES]
                                  for q in range(n_slabs)], axis=1) for j in range(S5_CHUNK)]
        per_chunk.append(jnp.stack(steps, axis=0))
    t = jnp.stack(per_chunk, axis=0)
    return jnp.transpose(t, (2, 0, 1, 3)).reshape(nb, n_ch * S5_CHUNK, n_slabs * S5_SLAB_LANES)


def _alibi_log2_slope_parts(slope):
    parts, rest = [], slope * LOG2_E
    for _ in range(3):
        part = rest.astype(BF16).astype(F32)
        parts.append(part)
        rest = rest - part
    return parts


def _bias_columns(lane, first_lane, vals):
    out = 0.0
    for e, val in enumerate(vals):
        out = jnp.where(lane == first_lane + e, val, out)
    return out


def _split_positions(pos):
    low = pos & ((1 << POS_LOW_BITS) - 1)
    return (pos - low).astype(F32), low.astype(F32)


def _inproj_kernel(x_ref, g_ref, w_ref, q_ref, kx_ref, vt_ref, us_ref, nrm_ref, *, scale):
    nb, tt, d = x_ref.shape
    q_width, n_heads, hv = q_ref.shape[2], vt_ref.shape[1], vt_ref.shape[2]
    dh = hv // 2
    step = pl.program_id(0)
    hn = _rms(x_ref[...].reshape(nb * tt, d), g_ref[...]).astype(BF16)
    w = lambda c0, c1: w_ref[:, c0:c1].astype(BF16)
    u = jnp.dot(hn, w(3 * q_width, w_ref.shape[1]), preferred_element_type=F32)
    for q, slab in enumerate(_to_slabs(u.reshape(nb, tt, u.shape[1]))):
        us_ref[q] = slab
    v = jnp.dot(hn, w(2 * q_width, 3 * q_width), preferred_element_type=F32)
    for b in range(nb):
        for hh in range(n_heads):
            vt_ref[b, hh] = v[b * tt:(b + 1) * tt, hh * hv:(hh + 1) * hv].T.astype(BF16)

    lane = lax.broadcasted_iota(jnp.int32, (1, hv), 1)

    def folded_norms(p):
        sq = jnp.max((p * p).reshape(nb, NORM_FOLD, tt // NORM_FOLD, q_width), axis=1)
        cols = [jnp.max(jnp.sum(sq[:, :, hh * hv:(hh + 1) * hv], axis=2, keepdims=True), axis=1)
                for hh in range(n_heads)]
        return cols

    kp = jnp.dot(hn, w(q_width, 2 * q_width), preferred_element_type=F32)
    p_hi, p_lo = _split_positions(step * tt + lax.broadcasted_iota(jnp.int32, (tt, 1), 0))
    for hh in range(n_heads):
        c2_parts = _alibi_log2_slope_parts(jnp.full((1, 1), 2.0 ** (-8.0 * (hh + 1) / n_heads), F32))
        kex = _bias_columns(lane, dh, c2_parts + c2_parts + [p_hi] * 3 + [p_lo] * 3)
        k_h = kp[:, hh * hv:(hh + 1) * hv].reshape(nb, tt, hv)
        kx_ref[:, hh, 0] = jnp.where(lane < dh, k_h, kex).astype(BF16)
        kx_ref[:, hh, 1] = jnp.where(lane >= dh, k_h, pltpu.roll(kex, dh, 1)).astype(BF16)

    k_norms = folded_norms(kp)
    qp = jnp.dot(hn, w(0, q_width), preferred_element_type=F32) * scale
    q_ref[...] = qp.astype(BF16).reshape(nb, tt, q_width)

    norms = 0.0
    for i, col in enumerate(folded_norms(qp) + k_norms):
        norms = jnp.where(lane == i, col, norms)

    @pl.when(step == 0)
    def _():
        nrm_ref[...] = jnp.zeros_like(nrm_ref)

    nrm_ref[...] = jnp.maximum(nrm_ref[...], norms)


def _inproj(x3, g1, w_in, *, q_width, n_heads, ssm_width, scale, tt=128):
    nb, s, d = x3.shape
    n_out = w_in.shape[1]
    hv = q_width // n_heads
    assert n_out == 3 * q_width + ssm_width
    n_slabs = ssm_width // S5_SLAB_LANES
    kern = functools.partial(_inproj_kernel, scale=scale)
    return pl.pallas_call(
        kern,
        grid=(s // tt,),
        in_specs=[
            pl.BlockSpec((nb, tt, d), lambda i: (0, i, 0)),
            pl.BlockSpec((1, d), lambda i: (0, 0)),
            pl.BlockSpec((d, n_out), lambda i: (0, 0)),
        ],
        out_specs=[
            pl.BlockSpec((nb, tt, q_width), lambda i: (0, i, 0)),
            pl.BlockSpec((nb, n_heads, 2, tt, hv), lambda i: (0, 0, 0, i, 0)),
            pl.BlockSpec((nb, n_heads, hv, tt), lambda i: (0, 0, 0, i)),
            pl.BlockSpec((n_slabs, tt // S5_CHUNK * nb, S5_CHUNK * S5_SLAB_LANES), lambda i: (0, i, 0)),
            pl.BlockSpec((nb, hv), lambda i: (0, 0)),
        ],
        out_shape=[
            jax.ShapeDtypeStruct((nb, s, q_width), BF16),
            jax.ShapeDtypeStruct((nb, n_heads, 2, s, hv), BF16),
            jax.ShapeDtypeStruct((nb, n_heads, hv, s), BF16),
            jax.ShapeDtypeStruct((n_slabs, s // S5_CHUNK * nb, S5_CHUNK * S5_SLAB_LANES), F32),
            jax.ShapeDtypeStruct((nb, hv), F32),
        ],
        compiler_params=pltpu.CompilerParams(
            dimension_semantics=("arbitrary",), vmem_limit_bytes=VMEM_LIMIT),
        name="inproj",
    )(x3, g1, w_in)


POS_LOW_BITS = 3
LOG2_E = 1.4426950408889634
SAFE_LOG2_SCORE = 96.0
SAFE_VALUE = 2.0 ** 16
NORM_FOLD = 4
NORM_ROUNDING_MARGIN = 1.02


def _attn_kernel(lq1_ref, lk1_ref, lq2_ref, lk2_ref, sg_ref, nrm_ref, q_ref, kx_ref, vin_ref, o_ref,
                 qex_ref, diag_ref, s0_ref, s1_ref, s2_ref, s3_ref, e0_ref, e1_ref,
                 *, qb, lambda_init):
    h = pl.program_id(0)
    s_refs, e_refs = (s0_ref, s1_ref, s2_ref, s3_ref), (e0_ref, e1_ref)
    s_len, hv = q_ref.shape[1], q_ref.shape[2]
    dh = hv // 2

    lam = (jnp.exp(jnp.sum(lq1_ref[...] * lk1_ref[...], axis=-1, keepdims=True))
           - jnp.exp(jnp.sum(lq2_ref[...] * lk2_ref[...], axis=-1, keepdims=True))
           + lambda_init)

    lane = lax.broadcasted_iota(jnp.int32, (1, hv), 1)
    data_lanes = (lane < dh, lane >= dh)

    @pl.when(pl.program_id(1) == 0)
    def _():
        slope = jnp.float32(2.0 ** -8.0)
        for hh in range(ATTN_HEADS - 1):
            slope = jnp.where(h == hh, jnp.float32(2.0 ** (-8.0 * (hh + 1) / ATTN_HEADS)), slope)
        slope = jnp.full((1, 1), slope, F32)
        c2_parts = _alibi_log2_slope_parts(slope)
        p_hi, p_lo = _split_positions(lax.broadcasted_iota(jnp.int32, (s_len, 1), 0))
        qex = _bias_columns(lane, dh, [-p_hi] * 3 + [-p_lo] * 3 + c2_parts + c2_parts)
        qex_ref[0], qex_ref[1] = qex, pltpu.roll(qex, dh, 1)
        off = lax.broadcasted_iota(jnp.int32, (qb, qb), 0) - lax.broadcasted_iota(jnp.int32, (qb, qb), 1)
        diag_ref[...] = jnp.abs(off).astype(F32) * (-(slope * LOG2_E))

    mine = lax.broadcasted_iota(jnp.int32, nrm_ref.shape, 0) == pl.program_id(1)
    lane_n = lax.broadcasted_iota(jnp.int32, nrm_ref.shape, 1)
    pick = lambda col: jnp.max(jnp.max(jnp.where(mine & (lane_n == col), nrm_ref[...], 0.0),
                                       axis=1, keepdims=True), axis=0, keepdims=True)
    v_max = jnp.max(jnp.max(jnp.abs(vin_ref[0, 0].astype(F32)), axis=1, keepdims=True), axis=0, keepdims=True)
    small_scores = jnp.logical_and(
        (pick(h) * pick(h + ATTN_HEADS))[0, 0] * NORM_ROUNDING_MARGIN <= SAFE_LOG2_SCORE ** 2,
        v_max[0, 0] <= SAFE_VALUE)

    n_blocks = s_len // qb

    def score_pieces(n, q_side):
        qi, c = divmod(n, 2)
        lo, hi = qi * qb, (qi + 1) * qb
        if qi not in q_side:
            q_side[qi] = q_ref[0, lo:hi, :].astype(F32)
        qf = q_side[qi]
        aug_t = lambda ex: jnp.where(data_lanes[c], qf, ex).T.astype(BF16)
        dot = lambda r0, r1, qt: jnp.dot(kx_ref[0, 0, c, r0:r1, :], qt, preferred_element_type=F32)
        q_extra = qex_ref[c, lo:hi, :]
        q_left, q_right = aug_t(q_extra), aug_t(-q_extra)
        for r0 in range(0, s_len, qb):
            if r0 == lo:
                yield lo, hi, dot(lo, hi, aug_t(jnp.zeros_like(q_extra))) + diag_ref[...]
            else:
                yield r0, r0 + qb, dot(r0, r0 + qb, q_left if r0 < lo else q_right)

    def weighted_values(n, denom):
        return jnp.dot(vin_ref[0, 0], e_refs[n % 2][...], preferred_element_type=F32) / denom

    def finish_block(qi, out_t):
        o = (out_t[0] - lam * out_t[1]).T
        o = _rms(o, sg_ref[...]) * (1.0 - lambda_init)
        o_ref[0, qi * qb:(qi + 1) * qb, :] = o.astype(o_ref.dtype)

    @pl.when(small_scores)
    def _():
        q_side = {}
        for qi in range(n_blocks):
            out_t = []
            for n in (2 * qi, 2 * qi + 1):
                denom = 0.0
                for r0, r1, sc in score_pieces(n, q_side):
                    e = jnp.exp2(sc)
                    denom = denom + jnp.sum(e, axis=0, keepdims=True)
                    e_refs[n % 2][r0:r1, :] = e.astype(BF16)
                out_t.append(weighted_values(n, denom))
            finish_block(qi, out_t)

    @pl.when(jnp.logical_not(small_scores))
    def _():
        q_side = {}

        def scores(n):
            for r0, r1, sc in score_pieces(n, q_side):
                s_refs[n % 4][r0:r1, :] = sc

        def exponentials(n):
            sc = s_refs[n % 4][...]
            e = jnp.exp2(sc - jnp.max(sc, axis=0, keepdims=True))
            e_refs[n % 2][...] = e.astype(BF16)
            return jnp.sum(e, axis=0, keepdims=True)

        scores(0), scores(1)
        for qi in range(n_blocks):
            if qi + 1 < n_blocks:
                scores(2 * qi + 2), scores(2 * qi + 3)
            out_t = []
            for n in (2 * qi, 2 * qi + 1):
                out_t.append(weighted_values(n, exponentials(n)))
            finish_block(qi, out_t)


def _attention(q3, kx, v_t, norms, lq1, lk1, lq2, lk2, subln_g, *, lambda_init, qb=512):
    b, s, w = q3.shape
    hv = w // ATTN_HEADS
    assert s <= 256 << POS_LOW_BITS and s % qb == 0
    kern = functools.partial(_attn_kernel, qb=qb, lambda_init=lambda_init)
    small = lambda shape: pl.BlockSpec(shape, lambda hi, bi: (0, 0))
    head = pl.BlockSpec((1, s, hv), lambda hi, bi: (bi, 0, hi))
    return pl.pallas_call(
        kern,
        grid=(ATTN_HEADS, b),
        in_specs=[
            small(lq1.shape), small(lk1.shape), small(lq2.shape), small(lk2.shape), small(subln_g.shape),
            small(norms.shape), head,
            pl.BlockSpec((1, 1, 2, s, hv), lambda hi, bi: (bi, hi, 0, 0, 0)),
            pl.BlockSpec((1, 1, hv, s), lambda hi, bi: (bi, hi, 0, 0)),
        ],
        out_specs=head,
        out_shape=jax.ShapeDtypeStruct((b, s, ATTN_HEADS * hv), BF16),
        scratch_shapes=[
            pltpu.VMEM((2, s, hv), F32), pltpu.VMEM((qb, qb), F32),
            pltpu.VMEM((s, qb), F32), pltpu.VMEM((s, qb), F32), pltpu.VMEM((s, qb), F32), pltpu.VMEM((s, qb), F32),
            pltpu.VMEM((s, qb), BF16), pltpu.VMEM((s, qb), BF16),
        ],
        compiler_params=pltpu.CompilerParams(
            dimension_semantics=("arbitrary", "arbitrary"), vmem_limit_bytes=VMEM_LIMIT),
        name="diff_attn",
    )(lq1, lk1, lq2, lk2, subln_g, norms, q3, kx, v_t)


S5_CHUNK = 8


def _s5_disc_kernel(par_ref, pbr_ref, pbi_ref, pcr_ref, pci_ref, akr_ref, aki_ref):
    cmul = lambda ar, ai, xr, xi: (ar * xr - ai * xi, ar * xi + ai * xr)
    lr_ref, li_ref, ldt_ref, bre_ref, bim_ref, cre_ref, cim_ref = (par_ref.at[i] for i in range(7))

    lr, li = lr_ref[...], li_ref[...]
    dt = jnp.exp(ldt_ref[...])
    mag = jnp.exp(lr * dt)
    ang = li * dt
    ar, ai = mag * jnp.cos(ang), mag * jnp.sin(ang)
    nr = ar - 1.0
    den = lr * lr + li * li
    cr = (nr * lr + ai * li) / den
    ci = (ai * lr - nr * li) / den
    b_re, b_im = bre_ref[...], bim_ref[...]
    pr, pi = cr * b_re - ci * b_im, cr * b_im + ci * b_re
    kr, ki = ar, ai
    for n in range(pbr_ref.shape[0]):
        pbr_ref[n] = pr
        pbi_ref[n] = pi
        pr, pi = cmul(ar, ai, pr, pi)
        if n > 0:
            kr, ki = cmul(ar, ai, kr, ki)
    akr_ref[...] = kr
    aki_ref[...] = ki

    qr, qi = cre_ref[...], cim_ref[...]
    for n in range(pcr_ref.shape[0]):
        pcr_ref[n] = qr
        pci_ref[n] = -qi
        qr, qi = cmul(ar, ai, qr, qi)


def _s5_discretise(lam_re, lam_im, log_dt, b_re, b_im, c_re, c_im):
    nd, g, p, hc = b_re.shape
    nt = g // S5_SLAB_GROUPS
    rows, width = nd * nt * hc, S5_SLAB_GROUPS * p
    full = lambda t: jnp.broadcast_to(t.astype(F32), (nd, g, hc, p))
    params = jnp.stack([full(lam_re[:, :, None, :]), full(lam_im[:, :, None, :]), full(log_dt[..., None, None]),
                        full(jnp.swapaxes(b_re, 2, 3)), full(jnp.swapaxes(b_im, 2, 3)), full(c_re), full(c_im)])
    params = (params.reshape(7, nd, nt, S5_SLAB_GROUPS, hc, p).transpose(0, 1, 2, 4, 3, 5)
              .reshape(7, rows, width))
    stack = lambda n: jax.ShapeDtypeStruct((n, rows, width), F32)
    one = jax.ShapeDtypeStruct((rows, width), F32)
    pbr, pbi, pcr, pci, akr, aki = pl.pallas_call(
        _s5_disc_kernel,
        out_shape=[stack(S5_CHUNK), stack(S5_CHUNK), stack(S5_CHUNK + 1), stack(S5_CHUNK + 1), one, one],
        name="s5_disc")(params)
    tiles = lambda t: t.reshape(t.shape[0], nd, nt, hc, width)
    ak = jnp.stack([akr, aki]).reshape(2, nd, nt, hc, width)[:, :, :, 0]
    ak = ak.transpose(2, 1, 0, 3).reshape(nt, 2 * nd, width)
    return tiles(pbr), tiles(pbi), tiles(pcr), tiles(pci), ak


def _s5_weights_kernel(pbr_ref, pbi_ref, pcr_ref, pci_ref, win_ref, wcat_ref):
    chunk, nd, hc, half = pbr_ref.shape[0], pbr_ref.shape[1], pbr_ref.shape[3], pbr_ref.shape[4]
    blk = S5_SLAB_GROUPS * hc
    p = half // S5_SLAB_GROUPS
    same_group = (lax.broadcasted_iota(jnp.int32, (blk, half), 0) // hc
                  == lax.broadcasted_iota(jnp.int32, (blk, half), 1) // p)

    def expand(re, im):
        grid = lambda t: jnp.where(same_group, jnp.tile(t, (S5_SLAB_GROUPS, 1)), 0.0)
        return jnp.concatenate([grid(re), grid(im)], axis=1)

    def split_dot(a, b):
        a_hi, b_hi = a.astype(BF16), b.astype(BF16)
        a_lo, b_lo = (a - a_hi.astype(F32)).astype(BF16), (b - b_hi.astype(F32)).astype(BF16)
        return jnp.dot(jnp.concatenate([a_hi, a_hi, a_lo], axis=1), jnp.concatenate([b_hi, b_lo, b_hi], axis=0),
                       preferred_element_type=F32)

    n_state = 2 * half
    for o in range(pbr_ref.shape[2]):
        wall = [[expand(pbr_ref[n, d, o], pbi_ref[n, d, o]) for n in range(chunk)] for d in range(nd)]
        mall = [[expand(pcr_ref[n, d, o], pci_ref[n, d, o]).T for n in range(chunk + 1)] for d in range(nd)]

        for d in range(nd):
            for j in range(chunk):
                n_in = chunk - 1 - j if d == 0 else j
                win_ref[d, o, j * blk:(j + 1) * blk, :] = wall[d][n_in].astype(win_ref.dtype)
            n_out = [j + 1 if d == 0 else chunk - j for j in range(chunk)]
            r0 = chunk * blk + d * n_state
            wcat_ref[o, r0:r0 + n_state, :] = jnp.concatenate([mall[d][n] for n in n_out],
                                                              axis=1).astype(wcat_ref.dtype)

        resp = [[split_dot(wall[d][n], mall[d][0]) for n in range(chunk)] for d in range(nd)]
        both = resp[0][0] + resp[1][0]
        for j in range(chunk):
            row = [resp[0][jp - j] if jp > j else resp[1][j - jp] if jp < j else both for jp in range(chunk)]
            wcat_ref[o, j * blk:(j + 1) * blk, :] = jnp.concatenate(row, axis=1).astype(wcat_ref.dtype)


def _s5_chunk_weights(pbr, pbi, pcr, pci, *, slabs_per_step=4):
    chunk, nd, nt, hc, half = pbr.shape
    width, n_state = chunk * S5_SLAB_GROUPS * hc, 2 * half
    sps = slabs_per_step
    spec = lambda t: pl.BlockSpec((t.shape[0], nd, sps, hc, half), lambda o: (0, 0, o, 0, 0))
    return pl.pallas_call(
        _s5_weights_kernel,
        grid=(nt // sps,),
        in_specs=[spec(pbr), spec(pbi), spec(pcr), spec(pci)],
        out_specs=[pl.BlockSpec((nd, sps, width, n_state), lambda o: (0, o, 0, 0)),
                   pl.BlockSpec((sps, width + nd * n_state, width), lambda o: (o, 0, 0))],
        out_shape=[jax.ShapeDtypeStruct((nd, nt, width, n_state), BF16),
                   jax.ShapeDtypeStruct((nt, width + nd * n_state, width), BF16)],
        compiler_params=pltpu.CompilerParams(dimension_semantics=("arbitrary",), vmem_limit_bytes=VMEM_LIMIT),
        name="s5_weights",
    )(pbr, pbi, pcr, pci)


def _s5_chunk_kernel(u_ref, win_ref, wcat_ref, ak_ref, dsk_ref, y_ref, xf_ref, xb_ref, s0_ref, s1_ref, *, nb, rb):
    n_sl, n_rows, half = u_ref.shape[0], u_ref.shape[1], ak_ref.shape[2]
    n_blk, pair = n_rows // rb, 2 * nb
    s_refs = (s0_ref, s1_ref)
    order = {0: list(range(n_blk)), 1: list(reversed(range(n_blk)))}

    def increments(d, i):
        blk = order[d][i]
        for sl in range(n_sl):
            s_refs[i % 2][sl] = jnp.dot(u_ref[sl, blk * rb:(blk + 1) * rb, :].astype(BF16), win_ref[d, sl],
                                        preferred_element_type=F32)

    def scan_block(d, i, xs):
        blk, s_ref, x_ref = order[d][i], s_refs[i % 2], (xf_ref, xb_ref)[d]
        ak = [(jnp.broadcast_to(ak_ref[sl, 2 * d:2 * d + 1, :], (nb, half)),
               jnp.broadcast_to(ak_ref[sl, 2 * d + 1:2 * d + 2, :], (nb, half))) for sl in range(n_sl)]

        def advance(sl, x, rows):
            ar, ai = ak[sl]
            return (ar * x[0] - ai * x[1] + s_ref[sl, rows, :half], ar * x[1] + ai * x[0] + s_ref[sl, rows, half:])

        def body(it, xs):
            r0 = pl.multiple_of((it if d == 0 else rb // pair - 1 - it) * pair, pair)
            early, late = pl.ds(r0, nb), pl.ds(r0 + nb, nb)
            dst = pl.ds(pl.multiple_of(blk * rb + r0, pair), pair)
            out = []
            for sl, x in enumerate(xs):
                x1 = advance(sl, x, early if d == 0 else late)
                x2 = advance(sl, x1, late if d == 0 else early)
                enter_early, enter_late = (x, x1) if d == 0 else (x1, x)
                x_ref[sl, dst, :half] = jnp.concatenate([enter_early[0], enter_late[0]], axis=0).astype(BF16)
                x_ref[sl, dst, half:] = jnp.concatenate([enter_early[1], enter_late[1]], axis=0).astype(BF16)
                out.append(x2)
            return tuple(out)

        return lax.fori_loop(0, rb // pair, body, xs, unroll=True)

    zero = jnp.zeros((nb, half), F32)
    for d in range(2):
        xs = ((zero, zero),) * n_sl
        increments(d, 0)
        for i in range(n_blk):
            if i + 1 < n_blk:
                increments(d, i + 1)
            xs = scan_block(d, i, xs)

    for blk in range(n_blk):
        rows = slice(blk * rb, (blk + 1) * rb)
        for sl in range(n_sl):
            u = u_ref[sl, rows, :]
            lhs = jnp.concatenate([u.astype(BF16), xf_ref[sl, rows, :], xb_ref[sl, rows, :]], axis=1)
            y_ref[sl, rows, :] = jnp.dot(lhs, wcat_ref[sl], preferred_element_type=F32) + dsk_ref[sl] * u


def _s5_chunk_scan(u_tiles, win, wcat, ak, dsk, *, nb, rb=512, slabs_per_step=4):
    nt, n_rows, width = u_tiles.shape
    n_state = win.shape[3]
    sps = slabs_per_step
    kern = functools.partial(_s5_chunk_kernel, nb=nb, rb=rb)
    return pl.pallas_call(
        kern,
        grid=(nt // sps,),
        in_specs=[
            pl.BlockSpec((sps, n_rows, width), lambda o: (o, 0, 0)),
            pl.BlockSpec((win.shape[0], sps, width, n_state), lambda o: (0, o, 0, 0)),
            pl.BlockSpec((sps,) + wcat.shape[1:], lambda o: (o, 0, 0)),
            pl.BlockSpec((sps,) + ak.shape[1:], lambda o: (o, 0, 0)),
            pl.BlockSpec((sps,) + dsk.shape[1:], lambda o: (o, 0, 0)),
        ],
        out_specs=pl.BlockSpec((sps, n_rows, width), lambda o: (o, 0, 0)),
        out_shape=jax.ShapeDtypeStruct(u_tiles.shape, F32),
        scratch_shapes=[
            pltpu.VMEM((sps, n_rows, n_state), BF16), pltpu.VMEM((sps, n_rows, n_state), BF16),
            pltpu.VMEM((sps, rb, n_state), F32), pltpu.VMEM((sps, rb, n_state), F32),
        ],
        compiler_params=pltpu.CompilerParams(
            dimension_semantics=("arbitrary",), vmem_limit_bytes=VMEM_LIMIT),
        name="s5_scan",
    )(u_tiles, win, wcat, ak, dsk)


def _post_kernel(x_ref, att_ref, ys_first_ref, ys_next_ref, wg_ref, bg_ref, wo_ref, g2_ref,
                 w1_ref, w2_ref, gf_ref, o_ref, y_ref, *, ff_chunk, final_norm):
    nb, tt, d = x_ref.shape
    aw = att_ref.shape[2]
    unslab = lambda ref: _from_slabs(ref, nb).reshape(nb * tt, -1)

    @pl.when(pl.program_id(0) == 0)
    def _():
        y_ref[...] = unslab(ys_first_ref)

    y = y_ref[...]
    z = 0.5 * y * (1.0 + lax.erf(y * (1.0 / math.sqrt(2.0))))
    gate = jax.nn.sigmoid(jnp.dot(z.astype(BF16), wg_ref[...].astype(BF16), preferred_element_type=F32)
                          + bg_ref[...])
    ssm = (z * gate).astype(BF16)
    h = (x_ref[...].reshape(nb * tt, d)
         + jnp.dot(att_ref[...].reshape(nb * tt, aw), wo_ref[:aw, :].astype(BF16), preferred_element_type=F32)
         + jnp.dot(ssm, wo_ref[aw:, :].astype(BF16), preferred_element_type=F32))
    hn2 = _rms(h, g2_ref[...]).astype(BF16)
    y_ref[...] = unslab(ys_next_ref)
    mlp = jnp.zeros_like(h)
    for f0 in range(0, w1_ref.shape[1], ff_chunk):
        a = jnp.maximum(jnp.dot(hn2, w1_ref[:, f0:f0 + ff_chunk].astype(BF16), preferred_element_type=F32), 0.0)
        mlp = mlp + jnp.dot((a * a).astype(BF16), w2_ref[f0:f0 + ff_chunk, :].astype(BF16),
                            preferred_element_type=F32)
    h = h + mlp
    o_ref[...] = (_rms(h, gf_ref[...]) if final_norm else h).reshape(nb, tt, d)


def _post(x3, att, y_slabs, w_glu, b_glu, w_out, g2, w1, w2, gf, *, final_norm, tt=64):
    nb, s, d = x3.shape
    aw = att.shape[2]
    n_slabs, _, slab_w = y_slabs.shape
    rows = lambda width: pl.BlockSpec((nb, tt, width), lambda i: (0, i, 0))
    const = lambda arr: pl.BlockSpec(arr.shape, lambda i: (0, 0), pipeline_mode=pl.Buffered(1))
    kern = functools.partial(_post_kernel, ff_chunk=1024, final_norm=final_norm)
    n_steps = s // tt
    slab_block = (n_slabs, tt // S5_CHUNK * nb, slab_w)
    return pl.pallas_call(
        kern,
        grid=(n_steps,),
        in_specs=[rows(d), rows(aw),
                  pl.BlockSpec(slab_block, lambda i: (0, 0, 0), pipeline_mode=pl.Buffered(1)),
                  pl.BlockSpec(slab_block, lambda i: (0, jnp.minimum(i + 1, n_steps - 1), 0)),
                  const(w_glu), const(b_glu), const(w_out), const(g2), const(w1), const(w2), const(gf)],
        out_specs=rows(d),
        out_shape=jax.ShapeDtypeStruct((nb, s, d), F32),
        scratch_shapes=[pltpu.VMEM((nb * tt, n_slabs * S5_SLAB_LANES), F32)],
        compiler_params=pltpu.CompilerParams(
            dimension_semantics=("arbitrary",), vmem_limit_bytes=VMEM_LIMIT),
        name="post_mlp",
    )(x3, att, y_slabs, y_slabs, w_glu, b_glu, w_out, g2, w1, w2, gf)


def kernel(x, norm1_g, w_in, lam_q1, lam_k1, lam_q2, lam_k2, subln_g, ssm_lam_re, ssm_lam_im, ssm_log_dt,
           ssm_b_re, ssm_b_im, ssm_c_re, ssm_c_im, ssm_d, w_glu, b_glu, w_out, norm2_g, w_mlp1, w_mlp2,
           final_g):
    bsz, s, d = x.shape
    depth = w_in.shape[0]
    ssm_w = ssm_d.shape[-1]
    attn_w = w_out.shape[1] - ssm_w
    qk_dim = attn_w // ATTN_HEADS // 2
    scale = float(qk_dim) ** -0.5 * LOG2_E

    h3 = x.astype(F32)
    for l in range(depth):
        lambda_init = 0.8 - 0.6 * math.exp(-0.3 * l)
        q, kx, v_t, u_slabs, norms = _inproj(h3, norm1_g[l][None].astype(F32), w_in[l], q_width=attn_w,
                                             n_heads=ATTN_HEADS, ssm_width=ssm_w, scale=scale)

        row = lambda t: t[l][None].astype(F32)
        att = _attention(q, kx, v_t, norms, row(lam_q1), row(lam_k1), row(lam_q2), row(lam_k2), row(subln_g),
                         lambda_init=lambda_init)

        pbr, pbi, pcr, pci, ak_slabs = _s5_discretise(ssm_lam_re[l], ssm_lam_im[l], ssm_log_dt[l],
                                                      ssm_b_re[l], ssm_b_im[l], ssm_c_re[l], ssm_c_im[l])
        win, wcat = _s5_chunk_weights(pbr, pbi, pcr, pci)
        dsk_slabs = jnp.tile(ssm_d[l].astype(F32).reshape(-1, 1, S5_SLAB_LANES), (1, 1, S5_CHUNK))
        y_slabs = _s5_chunk_scan(u_slabs, win, wcat, ak_slabs, dsk_slabs, nb=bsz)

        h3 = _post(h3, att, y_slabs, w_glu[l], row(b_glu), w_out[l], row(norm2_g),
                   w_mlp1[l], w_mlp2[l], final_g[None].astype(F32),
                   final_norm=(l == depth - 1))
    return h3.astype(x.dtype)
```
